```python
import math
import jax
import jax.numpy as jnp
from jax import lax
import numpy as np

D_MODEL = 4096
BATCH = 1
SEQ = 8192
DEPTH = 1

MIX_W = D_MODEL
HEAD_DIM = 128
ATTN_W = MIX_W // 2
CONV_W = MIX_W - ATTN_W
N_ATTN_HEADS = ATTN_W // HEAD_DIM
N_CONV_GROUPS = CONV_W // HEAD_DIM
BLOCK = 256
TOPK = 3
QCHUNK = 32
CONV_WIDTH = 3
NUM_BUCKETS = 32
MAX_DISTANCE = 128
EPS = 1e-6
PROJ_W = 4 * ATTN_W + 4 * CONV_W

kernel_name = "hymba_moba_shortconv_t5bias"


def rmsnorm(x, g):
    xf = x.astype(jnp.float32)
    y = xf * lax.rsqrt(jnp.mean(xf * xf, axis=-1, keepdims=True) + EPS)
    return (y * g.astype(jnp.float32)).astype(x.dtype)


def rel_bucket(dist):
    n = jnp.maximum(dist, 0)
    max_exact = NUM_BUCKETS // 2
    nf = jnp.maximum(n, 1).astype(jnp.float32)
    large = max_exact + (jnp.log(nf / max_exact) / math.log(MAX_DISTANCE / max_exact)
                         * (NUM_BUCKETS - max_exact)).astype(jnp.int32)
    large = jnp.minimum(large, NUM_BUCKETS - 1)
    return jnp.where(n < max_exact, n, large)


def moba_attention(q, k, v, rel_bias):
    B, S, H, Dh = q.shape
    nb = -(-S // BLOCK)
    s_pad = nb * BLOCK
    pad = ((0, 0), (0, s_pad - S), (0, 0), (0, 0))
    kp = jnp.pad(k, pad)
    vp = jnp.pad(v, pad)
    kb = kp.reshape(B, nb, BLOCK, H, Dh)
    vb = vp.reshape(B, nb, BLOCK, H, Dh)
    kmean = jnp.mean(kb.astype(jnp.float32), axis=2)

    pos = jnp.arange(S)
    qblk = pos // BLOCK
    gate = jnp.einsum('bshd,bnhd->bhsn', q.astype(jnp.float32), kmean)
    past = jnp.arange(nb)[None, :] < qblk[:, None]
    gate = jnp.where(past[None, None], gate, -jnp.inf)
    k_eff = min(TOPK, nb)
    _, sel = lax.top_k(gate, k_eff)
    valid = jnp.arange(k_eff)[None, :] < qblk[:, None]

    kbh = kb.transpose(0, 3, 1, 2, 4)
    vbh = vb.transpose(0, 3, 1, 2, 4)
    bias_tab = rel_bias.T.astype(jnp.float32)
    b_i = jnp.arange(B)[:, None, None, None]
    h_i = jnp.arange(H)[None, :, None, None]
    kin = jnp.arange(BLOCK)
    scale = Dh ** -0.5

    def chunk(c):
        start = c * QCHUNK
        qpos = start + jnp.arange(QCHUNK)
        q_c = lax.dynamic_slice_in_dim(q, start, QCHUNK, 1)
        sel_c = lax.dynamic_slice_in_dim(sel, start, QCHUNK, 2)
        val_c = lax.dynamic_slice_in_dim(valid, start, QCHUNK, 0)
        k_sel = kbh[b_i, h_i, sel_c]
        v_sel = vbh[b_i, h_i, sel_c]
        kpos = sel_c[..., None] * BLOCK + kin
        dist = qpos[None, None, :, None, None] - kpos
        bias_s = bias_tab[h_i[..., None], rel_bucket(dist)]
        l_sel = jnp.einsum('bqhd,bhqnkd->bhqnk', q_c, k_sel).astype(jnp.float32) * scale + bias_s
        l_sel = jnp.where(val_c[None, None, :, :, None], l_sel, -jnp.inf)
        own = start // BLOCK
        k_own = lax.dynamic_slice_in_dim(kp, own * BLOCK, BLOCK, 1)
        v_own = lax.dynamic_slice_in_dim(vp, own * BLOCK, BLOCK, 1)
        dist_o = qpos[:, None] - (own * BLOCK + kin)[None, :]
        bias_o = bias_tab[:, rel_bucket(dist_o)]
        l_own = jnp.einsum('bqhd,bkhd->bhqk', q_c, k_own).astype(jnp.float32) * scale + bias_o[None]
        l_own = jnp.where((dist_o >= 0)[None, None], l_own, -jnp.inf)
        logits = jnp.concatenate([l_sel.reshape(B, H, QCHUNK, k_eff * BLOCK), l_own], axis=-1)
        p = jax.nn.softmax(logits, axis=-1).astype(v.dtype)
        p_sel = p[..., :k_eff * BLOCK].reshape(B, H, QCHUNK, k_eff, BLOCK)
        p_own = p[..., k_eff * BLOCK:]
        return (jnp.einsum('bhqnk,bhqnkd->bqhd', p_sel, v_sel)
                + jnp.einsum('bhqk,bkhd->bqhd', p_own, v_own))

    out = lax.map(chunk, jnp.arange(S // QCHUNK))
    return jnp.moveaxis(out, 0, 1).reshape(B, S, H, Dh)


def short_conv(u, w):
    S = u.shape[1]
    up = jnp.pad(u, ((0, 0), (CONV_WIDTH - 1, 0), (0, 0)))
    y = up[:, 0:S] * w[0]
    for j in range(1, CONV_WIDTH):
        y = y + up[:, j:j + S] * w[j]
    return y


def setup_inputs(seed: int = 0) -> dict:
    key = jax.random.key(seed)
    ks = jax.random.split(key, 8)
    x = jax.random.normal(ks[0], (BATCH, SEQ, D_MODEL), jnp.float32)
    norm_gain = 1.0 + 0.02 * jax.random.normal(ks[1], (DEPTH, D_MODEL), jnp.float32)
    w_in = jax.random.normal(ks[2], (DEPTH, D_MODEL, PROJ_W), jnp.float32) * D_MODEL ** -0.5
    conv_w = jax.random.normal(ks[3], (DEPTH, CONV_WIDTH, CONV_W), jnp.float32) * CONV_WIDTH ** -0.5
    w_out = jax.random.normal(ks[4], (DEPTH, ATTN_W + CONV_W, D_MODEL), jnp.float32) * (ATTN_W + CONV_W) ** -0.5
    rel_bias = 0.5 * jax.random.normal(ks[5], (NUM_BUCKETS, N_ATTN_HEADS), jnp.float32)
    final_gain = 1.0 + 0.02 * jax.random.normal(ks[6], (D_MODEL,), jnp.float32)
    return {"x": x, "norm_gain": norm_gain, "w_in": w_in, "conv_w": conv_w,
            "w_out": w_out, "rel_bias": rel_bias, "final_gain": final_gain}


def reference(x, norm_gain, w_in, conv_w, w_out, rel_bias, final_gain):
    B, S, _ = x.shape
    split_at = [ATTN_W, 2 * ATTN_W, 3 * ATTN_W, 4 * ATTN_W,
                4 * ATTN_W + CONV_W, 4 * ATTN_W + 2 * CONV_W, 4 * ATTN_W + 3 * CONV_W]
    h = x
    for l in range(DEPTH):
        u = rmsnorm(h, norm_gain[l])
        proj = jnp.einsum('bsd,df->bsf', u, w_in[l])
        q, k, v, z_a, hc, b_gate, c_gate, z_c = jnp.split(proj, split_at, axis=-1)
        q = q.reshape(B, S, N_ATTN_HEADS, HEAD_DIM)
        k = k.reshape(B, S, N_ATTN_HEADS, HEAD_DIM)
        v = v.reshape(B, S, N_ATTN_HEADS, HEAD_DIM)
        attn = moba_attention(q, k, v, rel_bias).reshape(B, S, ATTN_W) * jax.nn.silu(z_a)
        conv = b_gate * short_conv(c_gate * hc, conv_w[l]) * jax.nn.silu(z_c)
        mixed = jnp.concatenate([attn, conv], axis=-1)
        h = h + jnp.einsum('bsf,fd->bsd', mixed, w_out[l])
    return rmsnorm(h, final_gain)
```

```python
import functools
import math

import numpy as np
import jax
import jax.numpy as jnp
from jax import lax
from jax.experimental import pallas as pl
from jax.experimental.pallas import tpu as pltpu

HEAD_DIM = 128
BLOCK = 256
TOPK = 3
NUM_BUCKETS = 32
MAX_DISTANCE = 128
EPS = 1e-6
NEG = -1e30

F32 = jnp.float32
BF16 = jnp.bfloat16

V7X_VMEM_LIMIT_BYTES = 56 * 1024 * 1024


def _silu(z):
    return z * (1.0 / (1.0 + jnp.exp(-z)))


def _rmsnorm_kernel(x_ref, g_ref, o_ref):
    x = x_ref[...]
    ms = jnp.mean(x * x, axis=-1, keepdims=True)
    o_ref[...] = (x * lax.rsqrt(ms + EPS) * g_ref[...]).astype(o_ref.dtype)


def _rmsnorm_bf16(x, g, tm=256):
    s, d = x.shape
    return pl.pallas_call(
        _rmsnorm_kernel,
        grid=(s // tm,),
        in_specs=[pl.BlockSpec((tm, d), lambda i: (i, 0)),
                  pl.BlockSpec((1, d), lambda i: (0, 0))],
        out_specs=pl.BlockSpec((tm, d), lambda i: (i, 0)),
        out_shape=jax.ShapeDtypeStruct((s, d), BF16),
        compiler_params=pltpu.CompilerParams(dimension_semantics=("arbitrary",)),
        name="rmsnorm_in",
    )(x, g.reshape(1, d))


def _proj_attn_kernel(u_ref, wq_ref, wk_ref, wv_ref, wz_ref,
                      q_ref, ks_ref, v_ref, sz_ref, km_ref, *, scale):
    u = u_ref[...]
    tm = u.shape[0]
    q = jnp.dot(u, wq_ref[...], preferred_element_type=F32)
    q_ref[...] = q.astype(BF16)
    k = jnp.dot(u, wk_ref[...], preferred_element_type=F32)
    ks_ref[...] = (k * scale).astype(BF16)
    km_ref[0] = jnp.mean(k.reshape(tm // BLOCK, BLOCK, k.shape[1]), axis=1)
    v = jnp.dot(u, wv_ref[...], preferred_element_type=F32)
    v_ref[...] = v.astype(BF16)
    z = jnp.dot(u, wz_ref[...], preferred_element_type=F32)
    sz_ref[...] = _silu(z)


def _proj_attn(u, w, attn_w, tm=1024, tn=256):
    s, d = u.shape
    nj = attn_w // tn
    w_spec = lambda part: pl.BlockSpec((d, tn), lambda j, i, part=part: (0, part * nj + j))
    tile = pl.BlockSpec((tm, tn), lambda j, i: (i, j))
    return pl.pallas_call(
        functools.partial(_proj_attn_kernel, scale=HEAD_DIM ** -0.5),
        grid=(nj, s // tm),
        in_specs=[pl.BlockSpec((tm, d), lambda j, i: (i, 0)),
                  w_spec(0), w_spec(1), w_spec(2), w_spec(3)],
        out_specs=[tile, tile, tile, tile,
                   pl.BlockSpec((1, tm // BLOCK, tn), lambda j, i: (i, 0, j))],
        out_shape=[jax.ShapeDtypeStruct((s, attn_w), BF16),
                   jax.ShapeDtypeStruct((s, attn_w), BF16),
                   jax.ShapeDtypeStruct((s, attn_w), BF16),
                   jax.ShapeDtypeStruct((s, attn_w), F32),
                   jax.ShapeDtypeStruct((s // tm, tm // BLOCK, attn_w), F32)],
        compiler_params=pltpu.CompilerParams(
            dimension_semantics=("arbitrary", "arbitrary"),
            vmem_limit_bytes=V7X_VMEM_LIMIT_BYTES),
        name="proj_attn",
    )(u, w, w, w, w)


def _proj_conv_kernel(u_ref, wh_ref, wb_ref, wc_ref, wz_ref, cw_ref, o_ref, carry_ref):
    @pl.when(pl.program_id(1) == 0)
    def _():
        carry_ref[...] = jnp.zeros_like(carry_ref)

    u = u_ref[...]
    tm = u.shape[0]
    hc = jnp.dot(u, wh_ref[...], preferred_element_type=F32)
    c = jnp.dot(u, wc_ref[...], preferred_element_type=F32)
    p = c * hc
    rows = lax.broadcasted_iota(jnp.int32, p.shape, 0)
    prev = carry_ref[...]
    prev1 = prev[7:8, :]
    prev2 = prev[6:7, :]
    p1 = jnp.where(rows == 0, prev1, pltpu.roll(p, 1, 0))
    p2 = jnp.where(rows == 0, prev2, jnp.where(rows == 1, prev1, pltpu.roll(p, 2, 0)))
    carry_ref[...] = p[tm - 8:, :]
    cw = cw_ref[...]
    y = p2 * cw[0:1, :] + p1 * cw[1:2, :] + p * cw[2:3, :]
    b = jnp.dot(u, wb_ref[...], preferred_element_type=F32)
    z = jnp.dot(u, wz_ref[...], preferred_element_type=F32)
    o_ref[...] = (b * y * _silu(z)).astype(o_ref.dtype)


def _proj_conv(u, w, conv_w, col0, conv_width, tm=1024, tn=256):
    s, d = u.shape
    nj = conv_width // tn
    base = col0 // tn
    w_spec = lambda part: pl.BlockSpec((d, tn), lambda j, i, part=part: (0, base + part * nj + j))
    return pl.pallas_call(
        _proj_conv_kernel,
        grid=(nj, s // tm),
        in_specs=[pl.BlockSpec((tm, d), lambda j, i: (i, 0)),
                  w_spec(0), w_spec(1), w_spec(2), w_spec(3),
                  pl.BlockSpec((conv_w.shape[0], tn), lambda j, i: (0, j))],
        out_specs=pl.BlockSpec((tm, tn), lambda j, i: (i, j)),
        out_shape=jax.ShapeDtypeStruct((s, conv_width), BF16),
        scratch_shapes=[pltpu.VMEM((8, tn), F32)],
        compiler_params=pltpu.CompilerParams(
            dimension_semantics=("arbitrary", "arbitrary"),
            vmem_limit_bytes=V7X_VMEM_LIMIT_BYTES),
        name="proj_conv",
    )(u, w, w, w, w, conv_w)


def _rel_bucket_np(dist):
    n = np.maximum(dist, 0)
    max_exact = NUM_BUCKETS // 2
    nf = np.maximum(n, 1).astype(np.float32)
    ratio = np.log(nf / np.float32(max_exact)) / np.float32(math.log(MAX_DISTANCE / max_exact))
    large = max_exact + (ratio * np.float32(NUM_BUCKETS - max_exact)).astype(np.int32)
    large = np.minimum(large, NUM_BUCKETS - 1)
    return np.where(n < max_exact, n, large).astype(np.int32)


def _bucket_tables():
    key = np.arange(BLOCK)[:, None]
    qry = np.arange(BLOCK)[None, :]
    d_own = qry - key
    own = np.where(d_own >= 0, _rel_bucket_np(d_own), NUM_BUCKETS).astype(np.int32)
    adj = _rel_bucket_np(BLOCK + qry - key)
    return own, adj


def _attn_kernel(relb_ref, idx_own_ref, idx_adj_ref, q_ref, ks_ref, v_ref, km_ref, sz_ref,
                 o_ref,
                 kaug_ref, vt_ref, bown_ref, badj_ref, qat_ref, m_ref, l_ref, acc_ref, *, nb):
    h = pl.program_id(0)
    qb = pl.program_id(1)

    @pl.when((h == 0) & (qb == 0))
    def _():
        lane = lax.broadcasted_iota(jnp.int32, (BLOCK, HEAD_DIM), 1)
        for blk in range(nb):
            kaug_ref[blk, :, HEAD_DIM:] = jnp.where(lane == blk, 1.0, 0.0).astype(BF16)

    @pl.when(qb == 0)
    def _():
        for blk in range(nb):
            rows = slice(blk * BLOCK, (blk + 1) * BLOCK)
            kaug_ref[blk, :, :HEAD_DIM] = ks_ref[rows, :]
            vt_ref[blk] = v_ref[rows, :].astype(F32).T.astype(BF16)
        far = relb_ref[NUM_BUCKETS - 1, h]
        io = idx_own_ref[...]
        ia = idx_adj_ref[...]
        own = jnp.full((BLOCK, BLOCK), NEG, F32)
        adj = jnp.zeros((BLOCK, BLOCK), F32)
        for b in range(NUM_BUCKETS):
            t = relb_ref[b, h] - far
            own = jnp.where(io == b, t, own)
            adj = jnp.where(ia == b, t, adj)
        bown_ref[...] = own
        badj_ref[...] = adj

    q_t = q_ref[...].astype(F32).T.astype(BF16)
    km = km_ref[...]
    km_hi = km.astype(BF16)
    km_lo = (km - km_hi.astype(F32)).astype(BF16)
    gate = (jnp.dot(km_hi, q_t, preferred_element_type=F32)
            + jnp.dot(km_lo, q_t, preferred_element_type=F32))
    blk_id = lax.broadcasted_iota(jnp.int32, gate.shape, 0)
    gate = jnp.where(blk_id < qb, gate, -jnp.inf)
    sel = blk_id == qb
    for r in range(min(TOPK, nb)):
        mx = jnp.max(gate, axis=0, keepdims=True)
        first = jnp.min(jnp.where(gate == mx, blk_id, nb), axis=0, keepdims=True)
        pick = (blk_id == first) & (r < qb)
        sel = sel | pick
        gate = jnp.where(pick, -jnp.inf, gate)
    selneg = jnp.where(sel, 0.0, NEG)
    qat_ref[:HEAD_DIM, :] = q_t
    qat_ref[HEAD_DIM:, :] = jnp.concatenate(
        [selneg, jnp.zeros((HEAD_DIM - nb, BLOCK), F32)], axis=0).astype(BF16)

    def logits(kb):
        return jnp.dot(kaug_ref[kb], qat_ref[...], preferred_element_type=F32)

    s = logits(qb) + bown_ref[...]
    m = jnp.max(s, axis=0, keepdims=True)
    p = jnp.exp(s - m)
    m_ref[...] = m
    l_ref[...] = jnp.sum(p, axis=0, keepdims=True)
    acc_ref[...] = jnp.dot(vt_ref[qb], p.astype(BF16), preferred_element_type=F32)

    def update(kb, bias):
        s = logits(kb)
        if bias is not None:
            s = s + bias
        m_old = m_ref[...]
        m_new = jnp.maximum(m_old, jnp.max(s, axis=0, keepdims=True))
        alpha = jnp.exp(m_old - m_new)
        p = jnp.exp(s - m_new)
        l_ref[...] = alpha * l_ref[...] + jnp.sum(p, axis=0, keepdims=True)
        acc_ref[...] = alpha * acc_ref[...] + jnp.dot(vt_ref[kb], p.astype(BF16),
                                                      preferred_element_type=F32)
        m_ref[...] = m_new

    @pl.when(qb >= 1)
    def _():
        update(qb - 1, badj_ref[...])

    def far_body(kb, carry):
        update(kb, None)
        return carry

    lax.fori_loop(0, qb - 1, far_body, 0)

    out_t = acc_ref[...] / l_ref[...]
    o_ref[...] = (out_t.T * sz_ref[...]).astype(o_ref.dtype)


def _moba_attention(q, ks, v, kmean, sz, rel_bias):
    s, attn_w = q.shape
    nh = attn_w // HEAD_DIM
    nb = s // BLOCK
    idx_own, idx_adj = _bucket_tables()
    head_cols = pl.BlockSpec((s, HEAD_DIM), lambda h, qb: (0, h))
    q_tile = pl.BlockSpec((BLOCK, HEAD_DIM), lambda h, qb: (qb, h))
    const_tile = pl.BlockSpec((BLOCK, BLOCK), lambda h, qb: (0, 0))
    return pl.pallas_call(
        functools.partial(_attn_kernel, nb=nb),
        grid=(nh, nb),
        in_specs=[pl.BlockSpec(memory_space=pltpu.SMEM),
                  const_tile, const_tile,
                  q_tile, head_cols, head_cols,
                  pl.BlockSpec((nb, HEAD_DIM), lambda h, qb: (0, h)),
                  q_tile],
        out_specs=q_tile,
        out_shape=jax.ShapeDtypeStruct((s, attn_w), BF16),
        scratch_shapes=[pltpu.VMEM((nb, BLOCK, 2 * HEAD_DIM), BF16),
                        pltpu.VMEM((nb, HEAD_DIM, BLOCK), BF16),
                        pltpu.VMEM((BLOCK, BLOCK), F32),
                        pltpu.VMEM((BLOCK, BLOCK), F32),
                        pltpu.VMEM((2 * HEAD_DIM, BLOCK), BF16),
                        pltpu.VMEM((1, BLOCK), F32),
                        pltpu.VMEM((1, BLOCK), F32),
                        pltpu.VMEM((HEAD_DIM, BLOCK), F32)],
        compiler_params=pltpu.CompilerParams(
            dimension_semantics=("arbitrary", "arbitrary"),
            vmem_limit_bytes=V7X_VMEM_LIMIT_BYTES),
        name="moba_attn",
    )(rel_bias, jnp.asarray(idx_own), jnp.asarray(idx_adj), q, ks, v, kmean, sz)


def _out_kernel(a_ref, c_ref, wa_ref, wc_ref, x_ref, g_ref, o_ref, h_ref, *, nj, tn):
    j = pl.program_id(1)
    acc = (jnp.dot(a_ref[...], wa_ref[...], preferred_element_type=F32)
           + jnp.dot(c_ref[...], wc_ref[...], preferred_element_type=F32))
    h_ref[j] = x_ref[...] + acc

    @pl.when(j == nj - 1)
    def _():
        ss = jnp.zeros((h_ref.shape[1], 1), F32)
        for jj in range(nj):
            hj = h_ref[jj]
            ss = ss + jnp.sum(hj * hj, axis=-1, keepdims=True)
        inv = lax.rsqrt(ss / (nj * tn) + EPS)
        for jj in range(nj):
            cols = slice(jj * tn, (jj + 1) * tn)
            o_ref[:, cols] = h_ref[jj] * inv * g_ref[:, cols]


def _out_proj(attn, conv, w_out, x, g, tm=512, tn=512):
    s, d = x.shape
    ka = attn.shape[1]
    kc = conv.shape[1]
    assert ka == kc
    nj = d // tn
    return pl.pallas_call(
        functools.partial(_out_kernel, nj=nj, tn=tn),
        grid=(s // tm, nj),
        in_specs=[pl.BlockSpec((tm, ka), lambda i, j: (i, 0)),
                  pl.BlockSpec((tm, kc), lambda i, j: (i, 0)),
                  pl.BlockSpec((ka, tn), lambda i, j: (0, j)),
                  pl.BlockSpec((kc, tn), lambda i, j: (1, j)),
                  pl.BlockSpec((tm, tn), lambda i, j: (i, j)),
                  pl.BlockSpec((1, d), lambda i, j: (0, 0))],
        out_specs=pl.BlockSpec((tm, d), lambda i, j: (i, 0)),
        out_shape=jax.ShapeDtypeStruct((s, d), F32),
        scratch_shapes=[pltpu.VMEM((nj, tm, tn), F32)],
        compiler_params=pltpu.CompilerParams(
            dimension_semantics=("arbitrary", "arbitrary"),
            vmem_limit_bytes=V7X_VMEM_LIMIT_BYTES),
        name="out_proj_norm",
    )(attn, conv, w_out, w_out, x, g.reshape(1, d))


def kernel(x, norm_gain, w_in, conv_w, w_out, rel_bias, final_gain):
    b, s, d = x.shape
    depth = norm_gain.shape[0]
    assert b == 1 and depth == 1, "kernel is written for one sequence and one layer"
    attn_w = d // 2
    conv_width = d - attn_w
    assert rel_bias.shape == (NUM_BUCKETS, attn_w // HEAD_DIM)
    assert w_in.shape[2] == 4 * attn_w + 4 * conv_width

    x2 = x[0]
    w_in_b = w_in[0].astype(BF16)
    w_out_b = w_out[0].astype(BF16)

    u = _rmsnorm_bf16(x2, norm_gain[0])
    q, ks, v, sz, kmean = _proj_attn(u, w_in_b, attn_w)
    conv = _proj_conv(u, w_in_b, conv_w[0], 4 * attn_w, conv_width)
    attn = _moba_attention(q, ks, v, kmean.reshape(s // BLOCK, attn_w), sz, rel_bias)
    out = _out_proj(attn, conv, w_out_b, x2, final_gain)
    return out[None]
```

```python
import functools
import math

import numpy as np
import jax
import jax.numpy as jnp
from jax import lax
from jax.experimental import pallas as pl
from jax.experimental.pallas import tpu as pltpu

HEAD_DIM = 128
BLOCK = 256
TOPK = 3
NUM_BUCKETS = 32
MAX_DISTANCE = 128
EPS = 1e-6
NEG = -1e30

F32 = jnp.float32
BF16 = jnp.bfloat16

V7X_VMEM_LIMIT_BYTES = 56 * 1024 * 1024


def _silu(z):
    return z * (1.0 / (1.0 + jnp.exp(-z)))


def _rmsnorm_kernel(x_ref, g_ref, o_ref):
    x = x_ref[...]
    ms = jnp.mean(x * x, axis=-1, keepdims=True)
    o_ref[...] = (x * lax.rsqrt(ms + EPS) * g_ref[...]).astype(o_ref.dtype)


def _rmsnorm_bf16(x, g, tm=256):
    s, d = x.shape
    return pl.pallas_call(
        _rmsnorm_kernel,
        grid=(s // tm,),
        in_specs=[pl.BlockSpec((tm, d), lambda i: (i, 0)),
                  pl.BlockSpec((1, d), lambda i: (0, 0))],
        out_specs=pl.BlockSpec((tm, d), lambda i: (i, 0)),
        out_shape=jax.ShapeDtypeStruct((s, d), BF16),
        compiler_params=pltpu.CompilerParams(dimension_semantics=("arbitrary",)),
        name="rmsnorm_in",
    )(x, g.reshape(1, d))


def _proj_attn_kernel(u_ref, wq_ref, wk_ref, wv_ref, wz_ref,
                      q_ref, ks_ref, v_ref, sz_ref, km_ref, *, scale):
    u = u_ref[...]
    tm = u.shape[0]
    q = jnp.dot(u, wq_ref[...], preferred_element_type=F32)
    q_ref[...] = q.astype(BF16)
    k = jnp.dot(u, wk_ref[...], preferred_element_type=F32)
    ks_ref[...] = (k * scale).astype(BF16)
    km_ref[0] = jnp.mean(k.reshape(tm // BLOCK, BLOCK, k.shape[1]), axis=1)
    v = jnp.dot(u, wv_ref[...], preferred_element_type=F32)
    v_ref[...] = v.astype(BF16)
    z = jnp.dot(u, wz_ref[...], preferred_element_type=F32)
    sz_ref[...] = _silu(z)


def _proj_attn(u, w, attn_w, tm=1024, tn=256):
    s, d = u.shape
    nj = attn_w // tn
    w_spec = lambda part: pl.BlockSpec((d, tn), lambda j, i, part=part: (0, part * nj + j))
    tile = pl.BlockSpec((tm, tn), lambda j, i: (i, j))
    return pl.pallas_call(
        functools.partial(_proj_attn_kernel, scale=HEAD_DIM ** -0.5),
        grid=(nj, s // tm),
        in_specs=[pl.BlockSpec((tm, d), lambda j, i: (i, 0)),
                  w_spec(0), w_spec(1), w_spec(2), w_spec(3)],
        out_specs=[tile, tile, tile, tile,
                   pl.BlockSpec((1, tm // BLOCK, tn), lambda j, i: (i, 0, j))],
        out_shape=[jax.ShapeDtypeStruct((s, attn_w), BF16),
                   jax.ShapeDtypeStruct((s, attn_w), BF16),
                   jax.ShapeDtypeStruct((s, attn_w), BF16),
                   jax.ShapeDtypeStruct((s, attn_w), F32),
                   jax.ShapeDtypeStruct((s // tm, tm // BLOCK, attn_w), F32)],
        compiler_params=pltpu.CompilerParams(
            dimension_semantics=("arbitrary", "arbitrary"),
            vmem_limit_bytes=V7X_VMEM_LIMIT_BYTES),
        name="proj_attn",
    )(u, w, w, w, w)


def _proj_conv_kernel(u_ref, wh_ref, wb_ref, wc_ref, wz_ref, cw_ref, o_ref, carry_ref):
    @pl.when(pl.program_id(1) == 0)
    def _():
        carry_ref[...] = jnp.zeros_like(carry_ref)

    u = u_ref[...]
    tm = u.shape[0]
    hc = jnp.dot(u, wh_ref[...], preferred_element_type=F32)
    c = jnp.dot(u, wc_ref[...], preferred_element_type=F32)
    p = c * hc
    rows = lax.broadcasted_iota(jnp.int32, p.shape, 0)
    prev = carry_ref[...]
    prev1 = prev[7:8, :]
    prev2 = prev[6:7, :]
    p1 = jnp.where(rows == 0, prev1, pltpu.roll(p, 1, 0))
    p2 = jnp.where(rows == 0, prev2, jnp.where(rows == 1, prev1, pltpu.roll(p, 2, 0)))
    carry_ref[...] = p[tm - 8:, :]
    cw = cw_ref[...]
    y = p2 * cw[0:1, :] + p1 * cw[1:2, :] + p * cw[2:3, :]
    b = jnp.dot(u, wb_ref[...], preferred_element_type=F32)
    z = jnp.dot(u, wz_ref[...], preferred_element_type=F32)
    o_ref[...] = (b * y * _silu(z)).astype(o_ref.dtype)


def _proj_conv(u, w, conv_w, col0, conv_width, tm=1024, tn=256):
    s, d = u.shape
    nj = conv_width // tn
    base = col0 // tn
    w_spec = lambda part: pl.BlockSpec((d, tn), lambda j, i, part=part: (0, base + part * nj + j))
    return pl.pallas_call(
        _proj_conv_kernel,
        grid=(nj, s // tm),
        in_specs=[pl.BlockSpec((tm, d), lambda j, i: (i, 0)),
                  w_spec(0), w_spec(1), w_spec(2), w_spec(3),
                  pl.BlockSpec((conv_w.shape[0], tn), lambda j, i: (0, j))],
        out_specs=pl.BlockSpec((tm, tn), lambda j, i: (i, j)),
        out_shape=jax.ShapeDtypeStruct((s, conv_width), BF16),
        scratch_shapes=[pltpu.VMEM((8, tn), F32)],
        compiler_params=pltpu.CompilerParams(
            dimension_semantics=("arbitrary", "arbitrary"),
            vmem_limit_bytes=V7X_VMEM_LIMIT_BYTES),
        name="proj_conv",
    )(u, w, w, w, w, conv_w)


def _rel_bucket_np(dist):
    n = np.maximum(dist, 0)
    max_exact = NUM_BUCKETS // 2
    nf = np.maximum(n, 1).astype(np.float32)
    ratio = np.log(nf / np.float32(max_exact)) / np.float32(math.log(MAX_DISTANCE / max_exact))
    large = max_exact + (ratio * np.float32(NUM_BUCKETS - max_exact)).astype(np.int32)
    large = np.minimum(large, NUM_BUCKETS - 1)
    return np.where(n < max_exact, n, large).astype(np.int32)


def _bucket_tables():
    key = np.arange(BLOCK)[:, None]
    qry = np.arange(BLOCK)[None, :]
    d_own = qry - key
    own = np.where(d_own >= 0, _rel_bucket_np(d_own), NUM_BUCKETS).astype(np.int32)
    adj = _rel_bucket_np(BLOCK + qry - key)
    return own, adj


GROUP = 4


def _attn_kernel(relb_ref, idx_own_ref, idx_adj_ref, q_ref, ks_ref, v_ref, km_ref, sz_ref,
                 o_ref,
                 kaug_ref, vt_ref, bown_ref, badj_ref, qat_ref, pen_ref, m_ref, l_ref, acc_ref, *, nb):
    h = pl.program_id(0)
    gk = GROUP * BLOCK
    dot = functools.partial(jnp.dot, preferred_element_type=F32)

    @pl.when(h == 0)
    def _():
        lane = lax.broadcasted_iota(jnp.int32, (BLOCK, HEAD_DIM), 1)
        for blk in range(nb):
            r0 = (blk % GROUP) * BLOCK
            kaug_ref[blk // GROUP, r0:r0 + BLOCK, HEAD_DIM:] = jnp.where(lane == blk, 1.0, 0.0).astype(BF16)

    far_bias = relb_ref[NUM_BUCKETS - 1, h]
    io = idx_own_ref[...]
    ia = idx_adj_ref[...]
    own = jnp.full((BLOCK, BLOCK), NEG, F32)
    adj = jnp.zeros((BLOCK, BLOCK), F32)
    for b in range(NUM_BUCKETS):
        t = relb_ref[b, h] - far_bias
        own = jnp.where(io == b, t, own)
        adj = jnp.where(ia == b, t, adj)
    bown_ref[...] = own
    badj_ref[...] = adj

    def stage(blk, carry):
        rows = pl.ds(pl.multiple_of(blk * BLOCK, BLOCK), BLOCK)
        r0 = pl.multiple_of((blk % GROUP) * BLOCK, BLOCK)
        kaug_ref[blk // GROUP, pl.ds(r0, BLOCK), :HEAD_DIM] = ks_ref[rows, :]
        vt_ref[blk] = v_ref[rows, :].astype(F32).T.astype(BF16)
        return carry

    lax.fori_loop(0, nb, stage, 0)

    km = km_ref[...]
    km_hi = km.astype(BF16)
    km_lo = (km - km_hi.astype(F32)).astype(BF16)

    def route(c, carry):
        rows = pl.ds(pl.multiple_of(c * gk, gk), gk)
        q_t = q_ref[rows, :].astype(F32).T.astype(BF16)
        gate = dot(km_hi, q_t) + dot(km_lo, q_t)
        blk_id = lax.broadcasted_iota(jnp.int32, gate.shape, 0)
        qb = c * GROUP + lax.shift_right_logical(
            lax.broadcasted_iota(jnp.int32, gate.shape, 1), int(math.log2(BLOCK)))
        gate = jnp.where(blk_id < qb, gate, -jnp.inf)
        sel = jnp.zeros(gate.shape, jnp.bool_)
        for r in range(min(TOPK, nb)):
            mx = jnp.max(gate, axis=0, keepdims=True)
            first = jnp.min(jnp.where(gate == mx, blk_id, nb), axis=0, keepdims=True)
            pick = (blk_id == first) & (qb > r)
            sel = sel | pick
            gate = jnp.where(pick, -jnp.inf, gate)
        sel_far = jnp.where(sel & (blk_id < qb - 1), 0.0, NEG)
        sel_rows = jnp.concatenate(
            [sel_far, jnp.zeros((HEAD_DIM - nb, gk), F32)], axis=0).astype(BF16)
        adj_sel = jnp.max(jnp.where(sel & (blk_id == qb - 1), 1.0, 0.0), axis=0, keepdims=True)
        pen = jnp.where(adj_sel > 0.0, 0.0, NEG)
        for j in range(GROUP):
            cols = slice(j * BLOCK, (j + 1) * BLOCK)
            qat_ref[c * GROUP + j, :HEAD_DIM, :] = q_t[:, cols]
            qat_ref[c * GROUP + j, HEAD_DIM:, :] = sel_rows[:, cols]
            pen_ref[c * GROUP + j] = jnp.broadcast_to(pen[:, cols], (8, BLOCK))
        return carry

    lax.fori_loop(0, nb // GROUP, route, 0)

    def colmax(s):
        return jnp.max(s, axis=0, keepdims=True)

    def colsum(p):
        return jnp.sum(p, axis=0, keepdims=True)

    def pv(first_blk, p):
        out = None
        for t in range(p.shape[0] // BLOCK):
            term = dot(vt_ref[first_blk + t], p[t * BLOCK:(t + 1) * BLOCK].astype(BF16))
            out = term if out is None else out + term
        return out

    def do_class(c, carry):
        g_prev = jnp.maximum(c - 1, 0)

        for j in range(GROUP):
            qb = c * GROUP + j
            q_t = qat_ref[qb, :HEAD_DIM, :]
            s_own = dot(kaug_ref[c, j * BLOCK:(j + 1) * BLOCK, :HEAD_DIM], q_t) + bown_ref[...]
            if j == 0:
                k_adj = kaug_ref[g_prev, (GROUP - 1) * BLOCK:, :HEAD_DIM]
                blk_adj = jnp.maximum(qb - 1, 0)
            else:
                k_adj = kaug_ref[c, (j - 1) * BLOCK:j * BLOCK, :HEAD_DIM]
                blk_adj = qb - 1
            s_adj = dot(k_adj, q_t) + (badj_ref[...] + pen_ref[qb, 0:1, :])
            m = jnp.maximum(colmax(s_own), colmax(s_adj))
            if j >= 2:
                s_far = dot(kaug_ref[c, :(j - 1) * BLOCK, :], qat_ref[qb])
                m = jnp.maximum(m, colmax(s_far))
            p_own = jnp.exp(s_own - m)
            p_adj = jnp.exp(s_adj - m)
            l = colsum(p_own) + colsum(p_adj)
            acc = pv(qb, p_own) + pv(blk_adj, p_adj)
            if j >= 2:
                p_far = jnp.exp(s_far - m)
                l = l + colsum(p_far)
                acc = acc + pv(c * GROUP, p_far)
            m_ref[j] = m
            l_ref[j] = l
            acc_ref[j] = acc

        def far(g, carry2):
            for j in range(GROUP):
                s = dot(kaug_ref[g], qat_ref[c * GROUP + j])
                m_old = m_ref[j]
                m_new = jnp.maximum(m_old, colmax(s))
                alpha = jnp.exp(m_old - m_new)
                p = jnp.exp(s - m_new)
                l_ref[j] = alpha * l_ref[j] + colsum(p)
                acc_ref[j] = alpha * acc_ref[j] + pv(g * GROUP, p)
                m_ref[j] = m_new
            return carry2

        lax.fori_loop(0, c, far, 0)

        for j in range(GROUP):
            rows = pl.ds(pl.multiple_of((c * GROUP + j) * BLOCK, BLOCK), BLOCK)
            out_t = acc_ref[j] * (1.0 / l_ref[j])
            o_ref[rows, :] = (out_t.T * sz_ref[rows, :]).astype(o_ref.dtype)
        return carry

    lax.fori_loop(0, nb // GROUP, do_class, 0)


def _moba_attention(q, ks, v, kmean, sz, rel_bias):
    s, attn_w = q.shape
    nh = attn_w // HEAD_DIM
    nb = s // BLOCK
    assert nb % GROUP == 0 and nb <= HEAD_DIM
    idx_own, idx_adj = _bucket_tables()
    head_cols = pl.BlockSpec((s, HEAD_DIM), lambda h: (0, h))
    const_tile = pl.BlockSpec((BLOCK, BLOCK), lambda h: (0, 0))
    return pl.pallas_call(
        functools.partial(_attn_kernel, nb=nb),
        grid=(nh,),
        in_specs=[pl.BlockSpec(memory_space=pltpu.SMEM),
                  const_tile, const_tile,
                  head_cols, head_cols, head_cols,
                  pl.BlockSpec((nb, HEAD_DIM), lambda h: (0, h)),
                  head_cols],
        out_specs=head_cols,
        out_shape=jax.ShapeDtypeStruct((s, attn_w), BF16),
        scratch_shapes=[pltpu.VMEM((nb // GROUP, GROUP * BLOCK, 2 * HEAD_DIM), BF16),
                        pltpu.VMEM((nb, HEAD_DIM, BLOCK), BF16),
                        pltpu.VMEM((BLOCK, BLOCK), F32),
                        pltpu.VMEM((BLOCK, BLOCK), F32),
                        pltpu.VMEM((nb, 2 * HEAD_DIM, BLOCK), BF16),
                        pltpu.VMEM((nb, 8, BLOCK), F32),
                        pltpu.VMEM((GROUP, 1, BLOCK), F32),
                        pltpu.VMEM((GROUP, 1, BLOCK), F32),
                        pltpu.VMEM((GROUP, HEAD_DIM, BLOCK), F32)],
        compiler_params=pltpu.CompilerParams(
            dimension_semantics=("arbitrary",),
            vmem_limit_bytes=V7X_VMEM_LIMIT_BYTES),
        name="moba_attn",
    )(rel_bias, jnp.asarray(idx_own), jnp.asarray(idx_adj), q, ks, v, kmean, sz)


def _out_kernel(a_ref, c_ref, wa_ref, wc_ref, x_ref, g_ref, o_ref, h_ref, *, nj, tn):
    j = pl.program_id(1)
    acc = (jnp.dot(a_ref[...], wa_ref[...], preferred_element_type=F32)
           + jnp.dot(c_ref[...], wc_ref[...], preferred_element_type=F32))
    h_ref[j] = x_ref[...] + acc

    @pl.when(j == nj - 1)
    def _():
        ss = jnp.zeros((h_ref.shape[1], 1), F32)
        for jj in range(nj):
            hj = h_ref[jj]
            ss = ss + jnp.sum(hj * hj, axis=-1, keepdims=True)
        inv = lax.rsqrt(ss / (nj * tn) + EPS)
        for jj in range(nj):
            cols = slice(jj * tn, (jj + 1) * tn)
            o_ref[:, cols] = h_ref[jj] * inv * g_ref[:, cols]


def _out_proj(attn, conv, w_out, x, g, tm=512, tn=512):
    s, d = x.shape
    ka = attn.shape[1]
    kc = conv.shape[1]
    assert ka == kc
    nj = d // tn
    return pl.pallas_call(
        functools.partial(_out_kernel, nj=nj, tn=tn),
        grid=(s // tm, nj),
        in_specs=[pl.BlockSpec((tm, ka), lambda i, j: (i, 0)),
                  pl.BlockSpec((tm, kc), lambda i, j: (i, 0)),
                  pl.BlockSpec((ka, tn), lambda i, j: (0, j)),
                  pl.BlockSpec((kc, tn), lambda i, j: (1, j)),
                  pl.BlockSpec((tm, tn), lambda i, j: (i, j)),
                  pl.BlockSpec((1, d), lambda i, j: (0, 0))],
        out_specs=pl.BlockSpec((tm, d), lambda i, j: (i, 0)),
        out_shape=jax.ShapeDtypeStruct((s, d), F32),
        scratch_shapes=[pltpu.VMEM((nj, tm, tn), F32)],
        compiler_params=pltpu.CompilerParams(
            dimension_semantics=("arbitrary", "arbitrary"),
            vmem_limit_bytes=V7X_VMEM_LIMIT_BYTES),
        name="out_proj_norm",
    )(attn, conv, w_out, w_out, x, g.reshape(1, d))


def kernel(x, norm_gain, w_in, conv_w, w_out, rel_bias, final_gain):
    b, s, d = x.shape
    depth = norm_gain.shape[0]
    assert b == 1 and depth == 1, "kernel is written for one sequence and one layer"
    attn_w = d // 2
    conv_width = d - attn_w
    assert rel_bias.shape == (NUM_BUCKETS, attn_w // HEAD_DIM)
    assert w_in.shape[2] == 4 * attn_w + 4 * conv_width

    x2 = x[0]
    w_in_b = w_in[0].astype(BF16)
    w_out_b = w_out[0].astype(BF16)

    u = _rmsnorm_bf16(x2, norm_gain[0])
    q, ks, v, sz, kmean = _proj_attn(u, w_in_b, attn_w)
    conv = _proj_conv(u, w_in_b, conv_w[0], 4 * attn_w, conv_width)
    attn = _moba_attention(q, ks, v, kmean.reshape(s // BLOCK, attn_w), sz, rel_bias)
    out = _out_proj(attn, conv, w_out_b, x2, final_gain)
    return out[None]
```

```python
import functools
import math

import numpy as np
import jax
import jax.numpy as jnp
from jax import lax
from jax.experimental import pallas as pl
from jax.experimental.pallas import tpu as pltpu

HEAD_DIM = 128
BLOCK = 256
TOPK = 3
NUM_BUCKETS = 32
MAX_DISTANCE = 128
EPS = 1e-6
NEG = -1e30
LOG2E = math.log2(math.e)

F32 = jnp.float32
BF16 = jnp.bfloat16

V7X_VMEM_LIMIT_BYTES = 56 * 1024 * 1024


def _silu(z):
    return z * (1.0 / (1.0 + jnp.exp(-z)))


def _rmsnorm_kernel(x_ref, g_ref, o_ref):
    x = x_ref[...]
    ms = jnp.mean(x * x, axis=-1, keepdims=True)
    o_ref[...] = (x * lax.rsqrt(ms + EPS) * g_ref[...]).astype(o_ref.dtype)


def _rmsnorm_bf16(x, g, tm=256):
    s, d = x.shape
    return pl.pallas_call(
        _rmsnorm_kernel,
        grid=(s // tm,),
        in_specs=[pl.BlockSpec((tm, d), lambda i: (i, 0)),
                  pl.BlockSpec((1, d), lambda i: (0, 0))],
        out_specs=pl.BlockSpec((tm, d), lambda i: (i, 0)),
        out_shape=jax.ShapeDtypeStruct((s, d), BF16),
        compiler_params=pltpu.CompilerParams(dimension_semantics=("arbitrary",)),
        name="rmsnorm_in",
    )(x, g.reshape(1, d))


def _proj_attn_kernel(u_ref, wq_ref, wk_ref, wv_ref, wz_ref,
                      q_ref, ks_ref, v_ref, sz_ref, km_ref, *, scale):
    u = u_ref[...]
    tm = u.shape[0]
    q = jnp.dot(u, wq_ref[...], preferred_element_type=F32)
    q_ref[...] = q.astype(BF16)
    k = jnp.dot(u, wk_ref[...], preferred_element_type=F32)
    ks_ref[...] = (k * scale).astype(BF16)
    km_ref[0] = jnp.mean(k.reshape(tm // BLOCK, BLOCK, k.shape[1]), axis=1)
    v = jnp.dot(u, wv_ref[...], preferred_element_type=F32)
    v_ref[...] = v.astype(BF16)
    z = jnp.dot(u, wz_ref[...], preferred_element_type=F32)
    sz_ref[...] = _silu(z)


def _proj_attn(u, w, attn_w, tm=1024, tn=256):
    s, d = u.shape
    nj = attn_w // tn
    w_spec = lambda part: pl.BlockSpec((d, tn), lambda j, i, part=part: (0, part * nj + j))
    tile = pl.BlockSpec((tm, tn), lambda j, i: (i, j))
    return pl.pallas_call(
        functools.partial(_proj_attn_kernel, scale=HEAD_DIM ** -0.5 * LOG2E),
        grid=(nj, s // tm),
        in_specs=[pl.BlockSpec((tm, d), lambda j, i: (i, 0)),
                  w_spec(0), w_spec(1), w_spec(2), w_spec(3)],
        out_specs=[tile, tile, tile, tile,
                   pl.BlockSpec((1, tm // BLOCK, tn), lambda j, i: (i, 0, j))],
        out_shape=[jax.ShapeDtypeStruct((s, attn_w), BF16),
                   jax.ShapeDtypeStruct((s, attn_w), BF16),
                   jax.ShapeDtypeStruct((s, attn_w), BF16),
                   jax.ShapeDtypeStruct((s, attn_w), F32),
                   jax.ShapeDtypeStruct((s // tm, tm // BLOCK, attn_w), F32)],
        compiler_params=pltpu.CompilerParams(
            dimension_semantics=("arbitrary", "arbitrary"),
            vmem_limit_bytes=V7X_VMEM_LIMIT_BYTES),
        name="proj_attn",
    )(u, w, w, w, w)


def _proj_conv_kernel(u_ref, wh_ref, wb_ref, wc_ref, wz_ref, cw_ref, o_ref, carry_ref):
    @pl.when(pl.program_id(1) == 0)
    def _():
        carry_ref[...] = jnp.zeros_like(carry_ref)

    u = u_ref[...]
    tm = u.shape[0]
    hc = jnp.dot(u, wh_ref[...], preferred_element_type=F32)
    c = jnp.dot(u, wc_ref[...], preferred_element_type=F32)
    p = c * hc
    rows = lax.broadcasted_iota(jnp.int32, p.shape, 0)
    prev = carry_ref[...]
    prev1 = prev[7:8, :]
    prev2 = prev[6:7, :]
    p1 = jnp.where(rows == 0, prev1, pltpu.roll(p, 1, 0))
    p2 = jnp.where(rows == 0, prev2, jnp.where(rows == 1, prev1, pltpu.roll(p, 2, 0)))
    carry_ref[...] = p[tm - 8:, :]
    cw = cw_ref[...]
    y = p2 * cw[0:1, :] + p1 * cw[1:2, :] + p * cw[2:3, :]
    b = jnp.dot(u, wb_ref[...], preferred_element_type=F32)
    z = jnp.dot(u, wz_ref[...], preferred_element_type=F32)
    o_ref[...] = (b * y * _silu(z)).astype(o_ref.dtype)


def _proj_conv(u, w, conv_w, col0, conv_width, tm=1024, tn=256):
    s, d = u.shape
    nj = conv_width // tn
    base = col0 // tn
    w_spec = lambda part: pl.BlockSpec((d, tn), lambda j, i, part=part: (0, base + part * nj + j))
    return pl.pallas_call(
        _proj_conv_kernel,
        grid=(nj, s // tm),
        in_specs=[pl.BlockSpec((tm, d), lambda j, i: (i, 0)),
                  w_spec(0), w_spec(1), w_spec(2), w_spec(3),
                  pl.BlockSpec((conv_w.shape[0], tn), lambda j, i: (0, j))],
        out_specs=pl.BlockSpec((tm, tn), lambda j, i: (i, j)),
        out_shape=jax.ShapeDtypeStruct((s, conv_width), BF16),
        scratch_shapes=[pltpu.VMEM((8, tn), F32)],
        compiler_params=pltpu.CompilerParams(
            dimension_semantics=("arbitrary", "arbitrary"),
            vmem_limit_bytes=V7X_VMEM_LIMIT_BYTES),
        name="proj_conv",
    )(u, w, w, w, w, conv_w)


def _rel_bucket_np(dist):
    n = np.maximum(dist, 0)
    max_exact = NUM_BUCKETS // 2
    nf = np.maximum(n, 1).astype(np.float32)
    ratio = np.log(nf / np.float32(max_exact)) / np.float32(math.log(MAX_DISTANCE / max_exact))
    large = max_exact + (ratio * np.float32(NUM_BUCKETS - max_exact)).astype(np.int32)
    large = np.minimum(large, NUM_BUCKETS - 1)
    return np.where(n < max_exact, n, large).astype(np.int32)


def _bucket_tables():
    key = np.arange(BLOCK)[:, None]
    qry = np.arange(BLOCK)[None, :]
    d_own = qry - key
    own = np.where(d_own >= 0, _rel_bucket_np(d_own), NUM_BUCKETS).astype(np.int32)
    adj = _rel_bucket_np(BLOCK + qry - key)
    return own, adj


GROUP = 4
QPAIR = 2


def _attn_kernel(relb_ref, idx_own_ref, idx_adj_ref, q_ref, ks_ref, v_ref, km_ref, sz_ref,
                 o_ref,
                 kaug_ref, vt_ref, bown_ref, badj_ref, qat_ref, pen_ref, m_ref, l_ref, acc_ref,
                 sa_ref, sb_ref, *, nb):
    h = pl.program_id(0)
    gk = GROUP * BLOCK
    pw = QPAIR * BLOCK
    dot = functools.partial(jnp.dot, preferred_element_type=F32)

    @pl.when(h == 0)
    def _():
        lane = lax.broadcasted_iota(jnp.int32, (BLOCK, HEAD_DIM), 1)
        for blk in range(nb):
            r0 = (blk % GROUP) * BLOCK
            kaug_ref[blk // GROUP, r0:r0 + BLOCK, HEAD_DIM:] = jnp.where(lane == blk, 1.0, 0.0).astype(BF16)

    far_bias = relb_ref[NUM_BUCKETS - 1, h]
    io = idx_own_ref[...]
    ia = idx_adj_ref[...]
    own = jnp.full((BLOCK, BLOCK), NEG, F32)
    adj = jnp.zeros((BLOCK, BLOCK), F32)
    for b in range(NUM_BUCKETS):
        t = (relb_ref[b, h] - far_bias) * LOG2E
        own = jnp.where(io == b, t, own)
        adj = jnp.where(ia == b, t, adj)
    bown_ref[...] = own
    badj_ref[...] = adj

    def stage(g, carry):
        rows = pl.ds(pl.multiple_of(g * gk, gk), gk)
        kaug_ref[g, :, :HEAD_DIM] = ks_ref[rows, :]
        vt_ref[g] = v_ref[rows, :].astype(F32).T.astype(BF16)
        return carry

    lax.fori_loop(0, nb // GROUP, stage, 0)

    km = km_ref[...]
    km_hi = km.astype(BF16)
    km_lo = (km - km_hi.astype(F32)).astype(BF16)

    def route(c, carry):
        rows = pl.ds(pl.multiple_of(c * gk, gk), gk)
        q_t = q_ref[rows, :].astype(F32).T.astype(BF16)
        gate = dot(km_hi, q_t) + dot(km_lo, q_t)
        blk_id = lax.broadcasted_iota(jnp.int32, gate.shape, 0)
        qb = c * GROUP + lax.shift_right_logical(
            lax.broadcasted_iota(jnp.int32, gate.shape, 1), int(math.log2(BLOCK)))
        gate = jnp.where(blk_id < qb, gate, -jnp.inf)
        sel = jnp.zeros(gate.shape, jnp.bool_)
        for r in range(min(TOPK, nb)):
            mx = jnp.max(gate, axis=0, keepdims=True)
            first = jnp.min(jnp.where(gate == mx, blk_id, nb), axis=0, keepdims=True)
            pick = (blk_id == first) & (qb > r)
            sel = sel | pick
            gate = jnp.where(pick, -jnp.inf, gate)
        sel_far = jnp.where(sel & (blk_id < qb - 1), 0.0, NEG)
        sel_rows = jnp.concatenate(
            [sel_far, jnp.zeros((HEAD_DIM - nb, gk), F32)], axis=0).astype(BF16)
        adj_sel = jnp.max(jnp.where(sel & (blk_id == qb - 1), 1.0, 0.0), axis=0, keepdims=True)
        pen = jnp.where(adj_sel > 0.0, 0.0, NEG)
        for t in range(GROUP // QPAIR):
            cols = slice(t * pw, (t + 1) * pw)
            qat_ref[c * (GROUP // QPAIR) + t, :HEAD_DIM, :] = q_t[:, cols]
            qat_ref[c * (GROUP // QPAIR) + t, HEAD_DIM:, :] = sel_rows[:, cols]
        for j in range(GROUP):
            cols = slice(j * BLOCK, (j + 1) * BLOCK)
            pen_ref[c * GROUP + j] = jnp.broadcast_to(pen[:, cols], (8, BLOCK))
        return carry

    lax.fori_loop(0, nb // GROUP, route, 0)

    def colmax(s):
        return jnp.max(s, axis=0, keepdims=True)

    def colsum(p):
        return jnp.sum(p, axis=0, keepdims=True)

    def pv(g, first, p):
        keys = slice(first * BLOCK, first * BLOCK + p.shape[0])
        return dot(vt_ref[g, :, keys], p.astype(BF16))

    def do_class(c, carry):
        g_prev = jnp.maximum(c - 1, 0)

        def far_logits(g, t):
            return dot(kaug_ref[g], qat_ref[c * (GROUP // QPAIR) + t])

        lanes_of = [slice((j % QPAIR) * BLOCK, (j % QPAIR + 1) * BLOCK) for j in range(GROUP)]
        logits = []
        for j in range(GROUP):
            qb = c * GROUP + j
            pair = c * (GROUP // QPAIR) + j // QPAIR
            q_t = qat_ref[pair, :HEAD_DIM, lanes_of[j]]
            s_own = dot(kaug_ref[c, j * BLOCK:(j + 1) * BLOCK, :HEAD_DIM], q_t) + bown_ref[...]
            if j == 0:
                k_adj = kaug_ref[g_prev, (GROUP - 1) * BLOCK:, :HEAD_DIM]
            else:
                k_adj = kaug_ref[c, (j - 1) * BLOCK:j * BLOCK, :HEAD_DIM]
            s_adj = dot(k_adj, q_t) + (badj_ref[...] + pen_ref[qb, 0:1, :])
            s_far = None
            if j >= 2:
                s_far = dot(kaug_ref[c, :(j - 1) * BLOCK, :], qat_ref[pair, :, lanes_of[j]])
            logits.append((s_own, s_adj, s_far))

        sa_ref[...] = far_logits(0, 0)

        for j, (s_own, s_adj, s_far) in enumerate(logits):
            g_adj, t_adj = (g_prev, GROUP - 1) if j == 0 else (c, j - 1)
            m = jnp.maximum(colmax(s_own), colmax(s_adj))
            if s_far is not None:
                m = jnp.maximum(m, colmax(s_far))
            p_own = jnp.exp2(s_own - m)
            p_adj = jnp.exp2(s_adj - m)
            l = colsum(p_own) + colsum(p_adj)
            acc = pv(c, j, p_own) + pv(g_adj, t_adj, p_adj)
            if s_far is not None:
                p_far = jnp.exp2(s_far - m)
                l = l + colsum(p_far)
                acc = acc + pv(c, 0, p_far)
            m_ref[j // QPAIR, :, lanes_of[j]] = m
            l_ref[j // QPAIR, :, lanes_of[j]] = l
            acc_ref[j // QPAIR, :, lanes_of[j]] = acc


        def far_update(g, t, s_ref):
            s = s_ref[...]
            m_old = m_ref[t]
            m_new = jnp.maximum(m_old, colmax(s))
            alpha = jnp.exp2(m_old - m_new)
            p = jnp.exp2(s - m_new)
            l_ref[t] = alpha * l_ref[t] + colsum(p)
            acc_ref[t] = alpha * acc_ref[t] + pv(g, 0, p)
            m_ref[t] = m_new

        def far(g, carry2):
            sb_ref[...] = far_logits(g, 1)
            far_update(g, 0, sa_ref)
            sa_ref[...] = far_logits(g + 1, 0)
            far_update(g, 1, sb_ref)
            return carry2

        lax.fori_loop(0, c, far, 0)

        for j in range(GROUP):
            lanes = slice((j % QPAIR) * BLOCK, (j % QPAIR + 1) * BLOCK)
            rows = pl.ds(pl.multiple_of((c * GROUP + j) * BLOCK, BLOCK), BLOCK)
            out_t = acc_ref[j // QPAIR, :, lanes] * (1.0 / l_ref[j // QPAIR, :, lanes])
            o_ref[rows, :] = (out_t.T * sz_ref[rows, :]).astype(o_ref.dtype)
        return carry

    lax.fori_loop(0, nb // GROUP, do_class, 0)


def _moba_attention(q, ks, v, kmean, sz, rel_bias):
    s, attn_w = q.shape
    nh = attn_w // HEAD_DIM
    nb = s // BLOCK
    assert nb % GROUP == 0 and nb <= HEAD_DIM
    idx_own, idx_adj = _bucket_tables()
    head_cols = pl.BlockSpec((s, HEAD_DIM), lambda h: (0, h))
    const_tile = pl.BlockSpec((BLOCK, BLOCK), lambda h: (0, 0))
    return pl.pallas_call(
        functools.partial(_attn_kernel, nb=nb),
        grid=(nh,),
        in_specs=[pl.BlockSpec(memory_space=pltpu.SMEM),
                  const_tile, const_tile,
                  head_cols, head_cols, head_cols,
                  pl.BlockSpec((nb, HEAD_DIM), lambda h: (0, h)),
                  head_cols],
        out_specs=head_cols,
        out_shape=jax.ShapeDtypeStruct((s, attn_w), BF16),
        scratch_shapes=[pltpu.VMEM((nb // GROUP, GROUP * BLOCK, 2 * HEAD_DIM), BF16),
                        pltpu.VMEM((nb // GROUP, HEAD_DIM, GROUP * BLOCK), BF16),
                        pltpu.VMEM((BLOCK, BLOCK), F32),
                        pltpu.VMEM((BLOCK, BLOCK), F32),
                        pltpu.VMEM((nb // QPAIR, 2 * HEAD_DIM, QPAIR * BLOCK), BF16),
                        pltpu.VMEM((nb, 8, BLOCK), F32),
                        pltpu.VMEM((GROUP // QPAIR, 1, QPAIR * BLOCK), F32),
                        pltpu.VMEM((GROUP // QPAIR, 1, QPAIR * BLOCK), F32),
                        pltpu.VMEM((GROUP // QPAIR, HEAD_DIM, QPAIR * BLOCK), F32),
                        pltpu.VMEM((GROUP * BLOCK, QPAIR * BLOCK), F32),
                        pltpu.VMEM((GROUP * BLOCK, QPAIR * BLOCK), F32)],
        compiler_params=pltpu.CompilerParams(
            dimension_semantics=("arbitrary",),
            vmem_limit_bytes=V7X_VMEM_LIMIT_BYTES),
        name="moba_attn",
    )(rel_bias, jnp.asarray(idx_own), jnp.asarray(idx_adj), q, ks, v, kmean, sz)


def _out_kernel(a_ref, c_ref, wa_ref, wc_ref, x_ref, g_ref, o_ref, h_ref, *, nj, tn):
    j = pl.program_id(1)
    acc = (jnp.dot(a_ref[...], wa_ref[...], preferred_element_type=F32)
           + jnp.dot(c_ref[...], wc_ref[...], preferred_element_type=F32))
    h_ref[j] = x_ref[...] + acc

    @pl.when(j == nj - 1)
    def _():
        ss = jnp.zeros((h_ref.shape[1], 1), F32)
        for jj in range(nj):
            hj = h_ref[jj]
            ss = ss + jnp.sum(hj * hj, axis=-1, keepdims=True)
        inv = lax.rsqrt(ss / (nj * tn) + EPS)
        for jj in range(nj):
            cols = slice(jj * tn, (jj + 1) * tn)
            o_ref[:, cols] = h_ref[jj] * inv * g_ref[:, cols]


def _out_proj(attn, conv, w_out, x, g, tm=512, tn=512):
    s, d = x.shape
    ka = attn.shape[1]
    kc = conv.shape[1]
    assert ka == kc
    nj = d // tn
    return pl.pallas_call(
        functools.partial(_out_kernel, nj=nj, tn=tn),
        grid=(s // tm, nj),
        in_specs=[pl.BlockSpec((tm, ka), lambda i, j: (i, 0)),
                  pl.BlockSpec((tm, kc), lambda i, j: (i, 0)),
                  pl.BlockSpec((ka, tn), lambda i, j: (0, j)),
                  pl.BlockSpec((kc, tn), lambda i, j: (1, j)),
                  pl.BlockSpec((tm, tn), lambda i, j: (i, j)),
                  pl.BlockSpec((1, d), lambda i, j: (0, 0))],
        out_specs=pl.BlockSpec((tm, d), lambda i, j: (i, 0)),
        out_shape=jax.ShapeDtypeStruct((s, d), F32),
        scratch_shapes=[pltpu.VMEM((nj, tm, tn), F32)],
        compiler_params=pltpu.CompilerParams(
            dimension_semantics=("arbitrary", "arbitrary"),
            vmem_limit_bytes=V7X_VMEM_LIMIT_BYTES),
        name="out_proj_norm",
    )(attn, conv, w_out, w_out, x, g.reshape(1, d))


def kernel(x, norm_gain, w_in, conv_w, w_out, rel_bias, final_gain):
    b, s, d = x.shape
    depth = norm_gain.shape[0]
    assert b == 1 and depth == 1, "kernel is written for one sequence and one layer"
    attn_w = d // 2
    conv_width = d - attn_w
    assert rel_bias.shape == (NUM_BUCKETS, attn_w // HEAD_DIM)
    assert w_in.shape[2] == 4 * attn_w + 4 * conv_width

    x2 = x[0]
    w_in_b = w_in[0].astype(BF16)
    w_out_b = w_out[0].astype(BF16)

    u = _rmsnorm_bf16(x2, norm_gain[0])
    q, ks, v, sz, kmean = _proj_attn(u, w_in_b, attn_w)
    conv = _proj_conv(u, w_in_b, conv_w[0], 4 * attn_w, conv_width)
    attn = _moba_attention(q, ks, v, kmean.reshape(s // BLOCK, attn_w), sz, rel_bias)
    out = _out_proj(attn, conv, w_out_b, x2, final_gain)
    return out[None]
```

```python
import functools
import math

import numpy as np
import jax
import jax.numpy as jnp
from jax import lax
from jax.experimental import pallas as pl
from jax.experimental.pallas import tpu as pltpu

HEAD_DIM = 128
BLOCK = 256
TOPK = 3
NUM_BUCKETS = 32
MAX_DISTANCE = 128
EPS = 1e-6
NEG = -1e30
LOG2E = math.log2(math.e)

F32 = jnp.float32
BF16 = jnp.bfloat16

V7X_VMEM_LIMIT_BYTES = 56 * 1024 * 1024


def _silu(z):
    return z * (1.0 / (1.0 + jnp.exp(-z)))


def _rmsnorm_kernel(x_ref, g_ref, o_ref):
    x = x_ref[...]
    ms = jnp.mean(x * x, axis=-1, keepdims=True)
    o_ref[...] = (x * lax.rsqrt(ms + EPS) * g_ref[...]).astype(o_ref.dtype)


def _rmsnorm_bf16(x, g, tm=256):
    s, d = x.shape
    return pl.pallas_call(
        _rmsnorm_kernel,
        grid=(s // tm,),
        in_specs=[pl.BlockSpec((tm, d), lambda i: (i, 0)),
                  pl.BlockSpec((1, d), lambda i: (0, 0))],
        out_specs=pl.BlockSpec((tm, d), lambda i: (i, 0)),
        out_shape=jax.ShapeDtypeStruct((s, d), BF16),
        compiler_params=pltpu.CompilerParams(dimension_semantics=("arbitrary",)),
        name="rmsnorm_in",
    )(x, g.reshape(1, d))


def _proj_attn_kernel(u_ref, wq_ref, wk_ref, wv_ref, wz_ref,
                      q_ref, ks_ref, v_ref, sz_ref, km_ref, *, scale):
    u = u_ref[...]
    tm = u.shape[0]
    q = jnp.dot(u, wq_ref[...], preferred_element_type=F32)
    q_ref[...] = q.astype(BF16)
    k = jnp.dot(u, wk_ref[...], preferred_element_type=F32)
    ks_ref[...] = (k * scale).astype(BF16)
    km_ref[0] = jnp.mean(k.reshape(tm // BLOCK, BLOCK, k.shape[1]), axis=1)
    v = jnp.dot(u, wv_ref[...], preferred_element_type=F32)
    v_ref[...] = v.astype(BF16)
    z = jnp.dot(u, wz_ref[...], preferred_element_type=F32)
    sz_ref[...] = _silu(z)


def _proj_attn(u, w, attn_w, tm=1024, tn=256):
    s, d = u.shape
    nj = attn_w // tn
    w_spec = lambda part: pl.BlockSpec((d, tn), lambda j, i, part=part: (0, part * nj + j))
    tile = pl.BlockSpec((tm, tn), lambda j, i: (i, j))
    return pl.pallas_call(
        functools.partial(_proj_attn_kernel, scale=HEAD_DIM ** -0.5 * LOG2E),
        grid=(nj, s // tm),
        in_specs=[pl.BlockSpec((tm, d), lambda j, i: (i, 0)),
                  w_spec(0), w_spec(1), w_spec(2), w_spec(3)],
        out_specs=[tile, tile, tile, tile,
                   pl.BlockSpec((1, tm // BLOCK, tn), lambda j, i: (i, 0, j))],
        out_shape=[jax.ShapeDtypeStruct((s, attn_w), BF16),
                   jax.ShapeDtypeStruct((s, attn_w), BF16),
                   jax.ShapeDtypeStruct((s, attn_w), BF16),
                   jax.ShapeDtypeStruct((s, attn_w), F32),
                   jax.ShapeDtypeStruct((s // tm, tm // BLOCK, attn_w), F32)],
        compiler_params=pltpu.CompilerParams(
            dimension_semantics=("arbitrary", "arbitrary"),
            vmem_limit_bytes=V7X_VMEM_LIMIT_BYTES),
        name="proj_attn",
    )(u, w, w, w, w)


def _proj_conv_kernel(u_ref, wh_ref, wb_ref, wc_ref, wz_ref, cw_ref, o_ref, carry_ref):
    @pl.when(pl.program_id(1) == 0)
    def _():
        carry_ref[...] = jnp.zeros_like(carry_ref)

    u = u_ref[...]
    tm = u.shape[0]
    hc = jnp.dot(u, wh_ref[...], preferred_element_type=F32)
    c = jnp.dot(u, wc_ref[...], preferred_element_type=F32)
    p = c * hc
    rows = lax.broadcasted_iota(jnp.int32, p.shape, 0)
    prev = carry_ref[...]
    prev1 = prev[7:8, :]
    prev2 = prev[6:7, :]
    p1 = jnp.where(rows == 0, prev1, pltpu.roll(p, 1, 0))
    p2 = jnp.where(rows == 0, prev2, jnp.where(rows == 1, prev1, pltpu.roll(p, 2, 0)))
    carry_ref[...] = p[tm - 8:, :]
    cw = cw_ref[...]
    y = p2 * cw[0:1, :] + p1 * cw[1:2, :] + p * cw[2:3, :]
    b = jnp.dot(u, wb_ref[...], preferred_element_type=F32)
    z = jnp.dot(u, wz_ref[...], preferred_element_type=F32)
    o_ref[...] = (b * y * _silu(z)).astype(o_ref.dtype)


def _proj_conv(u, w, conv_w, col0, conv_width, tm=1024, tn=256):
    s, d = u.shape
    nj = conv_width // tn
    base = col0 // tn
    w_spec = lambda part: pl.BlockSpec((d, tn), lambda j, i, part=part: (0, base + part * nj + j))
    return pl.pallas_call(
        _proj_conv_kernel,
        grid=(nj, s // tm),
        in_specs=[pl.BlockSpec((tm, d), lambda j, i: (i, 0)),
                  w_spec(0), w_spec(1), w_spec(2), w_spec(3),
                  pl.BlockSpec((conv_w.shape[0], tn), lambda j, i: (0, j))],
        out_specs=pl.BlockSpec((tm, tn), lambda j, i: (i, j)),
        out_shape=jax.ShapeDtypeStruct((s, conv_width), BF16),
        scratch_shapes=[pltpu.VMEM((8, tn), F32)],
        compiler_params=pltpu.CompilerParams(
            dimension_semantics=("arbitrary", "arbitrary"),
            vmem_limit_bytes=V7X_VMEM_LIMIT_BYTES),
        name="proj_conv",
    )(u, w, w, w, w, conv_w)


def _rel_bucket_np(dist):
    n = np.maximum(dist, 0)
    max_exact = NUM_BUCKETS // 2
    nf = np.maximum(n, 1).astype(np.float32)
    ratio = np.log(nf / np.float32(max_exact)) / np.float32(math.log(MAX_DISTANCE / max_exact))
    large = max_exact + (ratio * np.float32(NUM_BUCKETS - max_exact)).astype(np.int32)
    large = np.minimum(large, NUM_BUCKETS - 1)
    return np.where(n < max_exact, n, large).astype(np.int32)


def _bucket_tables():
    key = np.arange(BLOCK)[:, None]
    qry = np.arange(BLOCK)[None, :]
    d_own = qry - key
    own = np.where(d_own >= 0, _rel_bucket_np(d_own), NUM_BUCKETS).astype(np.int32)
    adj = _rel_bucket_np(BLOCK + qry - key)
    return own, adj


GROUP = 4
QPAIR = 2


def _attn_kernel(relb_ref, idx_own_ref, idx_adj_ref, q_ref, ks_ref, v_ref, km_ref, sz_ref,
                 o_ref,
                 kaug_ref, vt_ref, bown_ref, badj_ref, qat_ref, pen_ref, m_ref, l_ref, acc_ref,
                 sa_ref, sb_ref, *, nb):
    h = pl.program_id(0)
    gk = GROUP * BLOCK
    pw = QPAIR * BLOCK
    dot = functools.partial(jnp.dot, preferred_element_type=F32)

    @pl.when(h == 0)
    def _():
        lane = lax.broadcasted_iota(jnp.int32, (BLOCK, HEAD_DIM), 1)
        for blk in range(nb):
            r0 = (blk % GROUP) * BLOCK
            kaug_ref[blk // GROUP, r0:r0 + BLOCK, HEAD_DIM:] = jnp.where(lane == blk, 1.0, 0.0).astype(BF16)

    far_bias = relb_ref[NUM_BUCKETS - 1, h]
    io = idx_own_ref[...]
    ia = idx_adj_ref[...]
    own = jnp.full((BLOCK, BLOCK), NEG, F32)
    adj = jnp.zeros((BLOCK, BLOCK), F32)
    for b in range(NUM_BUCKETS):
        t = (relb_ref[b, h] - far_bias) * LOG2E
        own = jnp.where(io == b, t, own)
        adj = jnp.where(ia == b, t, adj)
    bown_ref[...] = own
    badj_ref[...] = adj

    def stage(g, carry):
        rows = pl.ds(pl.multiple_of(g * gk, gk), gk)
        kaug_ref[g, :, :HEAD_DIM] = ks_ref[rows, :]
        vt_ref[g] = v_ref[rows, :].astype(F32).T.astype(BF16)
        return carry

    lax.fori_loop(0, nb // GROUP, stage, 0)

    km = km_ref[...]
    km_hi = km.astype(BF16)
    km_lo = (km - km_hi.astype(F32)).astype(BF16)

    def route(c, carry):
        rows = pl.ds(pl.multiple_of(c * gk, gk), gk)
        q_t = q_ref[rows, :].astype(F32).T.astype(BF16)
        gate = dot(km_hi, q_t) + dot(km_lo, q_t)
        blk_id = lax.broadcasted_iota(jnp.int32, gate.shape, 0)
        qb = c * GROUP + lax.shift_right_logical(
            lax.broadcasted_iota(jnp.int32, gate.shape, 1), int(math.log2(BLOCK)))
        gate = jnp.where(blk_id < qb, gate, -jnp.inf)
        sel = jnp.zeros(gate.shape, jnp.bool_)
        for r in range(min(TOPK, nb)):
            mx = jnp.max(gate, axis=0, keepdims=True)
            first = jnp.min(jnp.where(gate == mx, blk_id, nb), axis=0, keepdims=True)
            pick = (blk_id == first) & (qb > r)
            sel = sel | pick
            gate = jnp.where(pick, -jnp.inf, gate)
        sel_far = jnp.where(sel & (blk_id < qb - 1), 0.0, NEG)
        sel_rows = jnp.concatenate(
            [sel_far, jnp.zeros((HEAD_DIM - nb, gk), F32)], axis=0).astype(BF16)
        adj_sel = jnp.max(jnp.where(sel & (blk_id == qb - 1), 1.0, 0.0), axis=0, keepdims=True)
        pen = jnp.where(adj_sel > 0.0, 0.0, NEG)
        for t in range(GROUP // QPAIR):
            cols = slice(t * pw, (t + 1) * pw)
            qat_ref[c * (GROUP // QPAIR) + t, :HEAD_DIM, :] = q_t[:, cols]
            qat_ref[c * (GROUP // QPAIR) + t, HEAD_DIM:, :] = sel_rows[:, cols]
        for j in range(GROUP):
            cols = slice(j * BLOCK, (j + 1) * BLOCK)
            pen_ref[c * GROUP + j] = jnp.broadcast_to(pen[:, cols], (8, BLOCK))
        return carry

    lax.fori_loop(0, nb // GROUP, route, 0)

    def colmax(s):
        return jnp.max(s, axis=0, keepdims=True)

    def colsum(p):
        return jnp.sum(p, axis=0, keepdims=True)

    def pv(g, first, p):
        keys = slice(first * BLOCK, first * BLOCK + p.shape[0])
        return dot(vt_ref[g, :, keys], p.astype(BF16))

    def do_class(c, carry):
        g_prev = max(c - 1, 0)

        def far_logits(g, t):
            return dot(kaug_ref[g], qat_ref[c * (GROUP // QPAIR) + t])

        lanes_of = [slice((j % QPAIR) * BLOCK, (j % QPAIR + 1) * BLOCK) for j in range(GROUP)]
        logits = []
        for j in range(GROUP):
            qb = c * GROUP + j
            pair = c * (GROUP // QPAIR) + j // QPAIR
            q_t = qat_ref[pair, :HEAD_DIM, lanes_of[j]]
            s_own = dot(kaug_ref[c, j * BLOCK:(j + 1) * BLOCK, :HEAD_DIM], q_t) + bown_ref[...]
            if j == 0:
                k_adj = kaug_ref[g_prev, (GROUP - 1) * BLOCK:, :HEAD_DIM]
            else:
                k_adj = kaug_ref[c, (j - 1) * BLOCK:j * BLOCK, :HEAD_DIM]
            s_adj = dot(k_adj, q_t) + (badj_ref[...] + pen_ref[qb, 0:1, :])
            s_far = None
            if j >= 2:
                s_far = dot(kaug_ref[c, :(j - 1) * BLOCK, :], qat_ref[pair, :, lanes_of[j]])
            logits.append((s_own, s_adj, s_far))

        s_first = far_logits(0, 0)
        sa_ref[...] = s_first
        cmax_first = colmax(s_first)

        for j, (s_own, s_adj, s_far) in enumerate(logits):
            g_adj, t_adj = (g_prev, GROUP - 1) if j == 0 else (c, j - 1)
            m = jnp.maximum(colmax(s_own), colmax(s_adj))
            if s_far is not None:
                m = jnp.maximum(m, colmax(s_far))
            p_own = jnp.exp2(s_own - m)
            p_adj = jnp.exp2(s_adj - m)
            l = colsum(p_own) + colsum(p_adj)
            acc = pv(c, j, p_own) + pv(g_adj, t_adj, p_adj)
            if s_far is not None:
                p_far = jnp.exp2(s_far - m)
                l = l + colsum(p_far)
                acc = acc + pv(c, 0, p_far)
            m_ref[j // QPAIR, :, lanes_of[j]] = m
            l_ref[j // QPAIR, :, lanes_of[j]] = l
            acc_ref[j // QPAIR, :, lanes_of[j]] = acc


        def far_update(g, t, s_ref, cmax):
            m_old = m_ref[t]
            m_new = jnp.maximum(m_old, cmax)
            alpha = jnp.exp2(m_old - m_new)
            p = jnp.exp2(s_ref[...] - m_new)
            l_ref[t] = alpha * l_ref[t] + colsum(p)
            acc_ref[t] = alpha * acc_ref[t] + pv(g, 0, p)
            m_ref[t] = m_new

        def far(g, cmax_a):
            s_b = far_logits(g, 1)
            sb_ref[...] = s_b
            cmax_b = colmax(s_b)
            far_update(g, 0, sa_ref, cmax_a)
            s_a = far_logits(g + 1, 0)
            sa_ref[...] = s_a
            cmax_a = colmax(s_a)
            far_update(g, 1, sb_ref, cmax_b)
            return cmax_a

        cmax_a = cmax_first
        for g in range(c):
            cmax_a = far(g, cmax_a)

        for j in range(GROUP):
            lanes = slice((j % QPAIR) * BLOCK, (j % QPAIR + 1) * BLOCK)
            rows = pl.ds((c * GROUP + j) * BLOCK, BLOCK)
            out_t = acc_ref[j // QPAIR, :, lanes] * (1.0 / l_ref[j // QPAIR, :, lanes])
            o_ref[rows, :] = (out_t.T * sz_ref[rows, :]).astype(o_ref.dtype)
        return carry

    one = jnp.minimum(h + 1, 1)
    for c in range(nb // GROUP):
        lax.fori_loop(0, one, lambda _, carry, c=c: do_class(c, carry), 0)


def _moba_attention(q, ks, v, kmean, sz, rel_bias):
    s, attn_w = q.shape
    nh = attn_w // HEAD_DIM
    nb = s // BLOCK
    assert nb % GROUP == 0 and nb <= HEAD_DIM
    idx_own, idx_adj = _bucket_tables()
    head_cols = pl.BlockSpec((s, HEAD_DIM), lambda h: (0, h))
    const_tile = pl.BlockSpec((BLOCK, BLOCK), lambda h: (0, 0))
    return pl.pallas_call(
        functools.partial(_attn_kernel, nb=nb),
        grid=(nh,),
        in_specs=[pl.BlockSpec(memory_space=pltpu.SMEM),
                  const_tile, const_tile,
                  head_cols, head_cols, head_cols,
                  pl.BlockSpec((nb, HEAD_DIM), lambda h: (0, h)),
                  head_cols],
        out_specs=head_cols,
        out_shape=jax.ShapeDtypeStruct((s, attn_w), BF16),
        scratch_shapes=[pltpu.VMEM((nb // GROUP, GROUP * BLOCK, 2 * HEAD_DIM), BF16),
                        pltpu.VMEM((nb // GROUP, HEAD_DIM, GROUP * BLOCK), BF16),
                        pltpu.VMEM((BLOCK, BLOCK), F32),
                        pltpu.VMEM((BLOCK, BLOCK), F32),
                        pltpu.VMEM((nb // QPAIR, 2 * HEAD_DIM, QPAIR * BLOCK), BF16),
                        pltpu.VMEM((nb, 8, BLOCK), F32),
                        pltpu.VMEM((GROUP // QPAIR, 1, QPAIR * BLOCK), F32),
                        pltpu.VMEM((GROUP // QPAIR, 1, QPAIR * BLOCK), F32),
                        pltpu.VMEM((GROUP // QPAIR, HEAD_DIM, QPAIR * BLOCK), F32),
                        pltpu.VMEM((GROUP * BLOCK, QPAIR * BLOCK), F32),
                        pltpu.VMEM((GROUP * BLOCK, QPAIR * BLOCK), F32)],
        compiler_params=pltpu.CompilerParams(
            dimension_semantics=("arbitrary",),
            vmem_limit_bytes=V7X_VMEM_LIMIT_BYTES),
        name="moba_attn",
    )(rel_bias, jnp.asarray(idx_own), jnp.asarray(idx_adj), q, ks, v, kmean, sz)


def _out_kernel(a_ref, c_ref, wa_ref, wc_ref, x_ref, g_ref, o_ref, h_ref, *, nj, tn):
    j = pl.program_id(1)
    acc = (jnp.dot(a_ref[...], wa_ref[...], preferred_element_type=F32)
           + jnp.dot(c_ref[...], wc_ref[...], preferred_element_type=F32))
    h_ref[j] = x_ref[...] + acc

    @pl.when(j == nj - 1)
    def _():
        ss = jnp.zeros((h_ref.shape[1], 1), F32)
        for jj in range(nj):
            hj = h_ref[jj]
            ss = ss + jnp.sum(hj * hj, axis=-1, keepdims=True)
        inv = lax.rsqrt(ss / (nj * tn) + EPS)
        for jj in range(nj):
            cols = slice(jj * tn, (jj + 1) * tn)
            o_ref[:, cols] = h_ref[jj] * inv * g_ref[:, cols]


def _out_proj(attn, conv, w_out, x, g, tm=512, tn=512):
    s, d = x.shape
    ka = attn.shape[1]
    kc = conv.shape[1]
    assert ka == kc
    nj = d // tn
    return pl.pallas_call(
        functools.partial(_out_kernel, nj=nj, tn=tn),
        grid=(s // tm, nj),
        in_specs=[pl.BlockSpec((tm, ka), lambda i, j: (i, 0)),
                  pl.BlockSpec((tm, kc), lambda i, j: (i, 0)),
                  pl.BlockSpec((ka, tn), lambda i, j: (0, j)),
                  pl.BlockSpec((kc, tn), lambda i, j: (1, j)),
                  pl.BlockSpec((tm, tn), lambda i, j: (i, j)),
                  pl.BlockSpec((1, d), lambda i, j: (0, 0))],
        out_specs=pl.BlockSpec((tm, d), lambda i, j: (i, 0)),
        out_shape=jax.ShapeDtypeStruct((s, d), F32),
        scratch_shapes=[pltpu.VMEM((nj, tm, tn), F32)],
        compiler_params=pltpu.CompilerParams(
            dimension_semantics=("arbitrary", "arbitrary"),
            vmem_limit_bytes=V7X_VMEM_LIMIT_BYTES),
        name="out_proj_norm",
    )(attn, conv, w_out, w_out, x, g.reshape(1, d))


def kernel(x, norm_gain, w_in, conv_w, w_out, rel_bias, final_gain):
    b, s, d = x.shape
    depth = norm_gain.shape[0]
    assert b == 1 and depth == 1, "kernel is written for one sequence and one layer"
    attn_w = d // 2
    conv_width = d - attn_w
    assert rel_bias.shape == (NUM_BUCKETS, attn_w // HEAD_DIM)
    assert w_in.shape[2] == 4 * attn_w + 4 * conv_width

    x2 = x[0]
    w_in_b = w_in[0].astype(BF16)
    w_out_b = w_out[0].astype(BF16)

    u = _rmsnorm_bf16(x2, norm_gain[0])
    q, ks, v, sz, kmean = _proj_attn(u, w_in_b, attn_w)
    conv = _proj_conv(u, w_in_b, conv_w[0], 4 * attn_w, conv_width)
    attn = _moba_attention(q, ks, v, kmean.reshape(s // BLOCK, attn_w), sz, rel_bias)
    out = _out_proj(attn, conv, w_out_b, x2, final_gain)
    return out[None]
```

```python
import functools
import math

import numpy as np
import jax
import jax.numpy as jnp
from jax import lax
from jax.experimental import pallas as pl
from jax.experimental.pallas import tpu as pltpu

HEAD_DIM = 128
BLOCK = 256
TOPK = 3
NUM_BUCKETS = 32
MAX_DISTANCE = 128
EPS = 1e-6
NEG = -1e30
LOG2E = math.log2(math.e)

F32 = jnp.float32
BF16 = jnp.bfloat16

V7X_VMEM_LIMIT_BYTES = 56 * 1024 * 1024


def _silu(z):
    return z * (1.0 / (1.0 + jnp.exp(-z)))


def _rmsnorm_kernel(x_ref, g_ref, o_ref):
    x = x_ref[...]
    ms = jnp.mean(x * x, axis=-1, keepdims=True)
    o_ref[...] = (x * lax.rsqrt(ms + EPS) * g_ref[...]).astype(o_ref.dtype)


def _rmsnorm_bf16(x, g, tm=256):
    s, d = x.shape
    return pl.pallas_call(
        _rmsnorm_kernel,
        grid=(s // tm,),
        in_specs=[pl.BlockSpec((tm, d), lambda i: (i, 0)),
                  pl.BlockSpec((1, d), lambda i: (0, 0))],
        out_specs=pl.BlockSpec((tm, d), lambda i: (i, 0)),
        out_shape=jax.ShapeDtypeStruct((s, d), BF16),
        compiler_params=pltpu.CompilerParams(dimension_semantics=("arbitrary",)),
        name="rmsnorm_in",
    )(x, g.reshape(1, d))


def _proj_attn_kernel(u_ref, wq_ref, wk_ref, wv_ref, wz_ref,
                      q_ref, ks_ref, v_ref, sz_ref, km_ref, *, scale):
    u = u_ref[...]
    tm = u.shape[0]
    q = jnp.dot(u, wq_ref[...], preferred_element_type=F32)
    q_ref[...] = q.astype(BF16)
    k = jnp.dot(u, wk_ref[...], preferred_element_type=F32)
    ks_ref[...] = (k * scale).astype(BF16)
    km_ref[0] = jnp.mean(k.reshape(tm // BLOCK, BLOCK, k.shape[1]), axis=1)
    v = jnp.dot(u, wv_ref[...], preferred_element_type=F32)
    v_ref[...] = v.astype(BF16)
    z = jnp.dot(u, wz_ref[...], preferred_element_type=F32)
    sz_ref[...] = _silu(z)


def _proj_attn(u, w, attn_w, tm=1024, tn=256):
    s, d = u.shape
    nj = attn_w // tn
    w_spec = lambda part: pl.BlockSpec((d, tn), lambda j, i, part=part: (0, part * nj + j))
    tile = pl.BlockSpec((tm, tn), lambda j, i: (i, j))
    return pl.pallas_call(
        functools.partial(_proj_attn_kernel, scale=HEAD_DIM ** -0.5 * LOG2E),
        grid=(nj, s // tm),
        in_specs=[pl.BlockSpec((tm, d), lambda j, i: (i, 0)),
                  w_spec(0), w_spec(1), w_spec(2), w_spec(3)],
        out_specs=[tile, tile, tile, tile,
                   pl.BlockSpec((1, tm // BLOCK, tn), lambda j, i: (i, 0, j))],
        out_shape=[jax.ShapeDtypeStruct((s, attn_w), BF16),
                   jax.ShapeDtypeStruct((s, attn_w), BF16),
                   jax.ShapeDtypeStruct((s, attn_w), BF16),
                   jax.ShapeDtypeStruct((s, attn_w), F32),
                   jax.ShapeDtypeStruct((s // tm, tm // BLOCK, attn_w), F32)],
        compiler_params=pltpu.CompilerParams(
            dimension_semantics=("arbitrary", "arbitrary"),
            vmem_limit_bytes=V7X_VMEM_LIMIT_BYTES),
        name="proj_attn",
    )(u, w, w, w, w)


def _proj_conv_kernel(u_ref, wh_ref, wb_ref, wc_ref, wz_ref, cw_ref, o_ref, carry_ref):
    @pl.when(pl.program_id(1) == 0)
    def _():
        carry_ref[...] = jnp.zeros_like(carry_ref)

    u = u_ref[...]
    tm = u.shape[0]
    hc = jnp.dot(u, wh_ref[...], preferred_element_type=F32)
    c = jnp.dot(u, wc_ref[...], preferred_element_type=F32)
    p = c * hc
    rows = lax.broadcasted_iota(jnp.int32, p.shape, 0)
    prev = carry_ref[...]
    prev1 = prev[7:8, :]
    prev2 = prev[6:7, :]
    p1 = jnp.where(rows == 0, prev1, pltpu.roll(p, 1, 0))
    p2 = jnp.where(rows == 0, prev2, jnp.where(rows == 1, prev1, pltpu.roll(p, 2, 0)))
    carry_ref[...] = p[tm - 8:, :]
    cw = cw_ref[...]
    y = p2 * cw[0:1, :] + p1 * cw[1:2, :] + p * cw[2:3, :]
    b = jnp.dot(u, wb_ref[...], preferred_element_type=F32)
    z = jnp.dot(u, wz_ref[...], preferred_element_type=F32)
    o_ref[...] = (b * y * _silu(z)).astype(o_ref.dtype)


def _proj_conv(u, w, conv_w, col0, conv_width, tm=1024, tn=256):
    s, d = u.shape
    nj = conv_width // tn
    base = col0 // tn
    w_spec = lambda part: pl.BlockSpec((d, tn), lambda j, i, part=part: (0, base + part * nj + j))
    return pl.pallas_call(
        _proj_conv_kernel,
        grid=(nj, s // tm),
        in_specs=[pl.BlockSpec((tm, d), lambda j, i: (i, 0)),
                  w_spec(0), w_spec(1), w_spec(2), w_spec(3),
                  pl.BlockSpec((conv_w.shape[0], tn), lambda j, i: (0, j))],
        out_specs=pl.BlockSpec((tm, tn), lambda j, i: (i, j)),
        out_shape=jax.ShapeDtypeStruct((s, conv_width), BF16),
        scratch_shapes=[pltpu.VMEM((8, tn), F32)],
        compiler_params=pltpu.CompilerParams(
            dimension_semantics=("arbitrary", "arbitrary"),
            vmem_limit_bytes=V7X_VMEM_LIMIT_BYTES),
        name="proj_conv",
    )(u, w, w, w, w, conv_w)


def _rel_bucket_np(dist):
    n = np.maximum(dist, 0)
    max_exact = NUM_BUCKETS // 2
    nf = np.maximum(n, 1).astype(np.float32)
    ratio = np.log(nf / np.float32(max_exact)) / np.float32(math.log(MAX_DISTANCE / max_exact))
    large = max_exact + (ratio * np.float32(NUM_BUCKETS - max_exact)).astype(np.int32)
    large = np.minimum(large, NUM_BUCKETS - 1)
    return np.where(n < max_exact, n, large).astype(np.int32)


def _bucket_tables():
    key = np.arange(BLOCK)[:, None]
    qry = np.arange(BLOCK)[None, :]
    d_own = qry - key
    own = np.where(d_own >= 0, _rel_bucket_np(d_own), NUM_BUCKETS).astype(np.int32)
    adj = _rel_bucket_np(BLOCK + qry - key)
    return own, adj


GROUP = 4
QPAIR = 2


def _attn_kernel(relb_ref, idx_own_ref, idx_adj_ref, q_ref, ks_ref, v_ref, km_ref, sz_ref,
                 o_ref,
                 kaug_ref, vt_ref, bown_ref, badj_ref, qat_ref, pen_ref, m_ref, l_ref, acc_ref,
                 sa_ref, sb_ref, *, nb):
    h = pl.program_id(0)
    gk = GROUP * BLOCK
    pw = QPAIR * BLOCK
    dot = functools.partial(jnp.dot, preferred_element_type=F32)

    @pl.when(h == 0)
    def _():
        lane = lax.broadcasted_iota(jnp.int32, (BLOCK, HEAD_DIM), 1)
        for blk in range(nb):
            r0 = (blk % GROUP) * BLOCK
            kaug_ref[blk // GROUP, r0:r0 + BLOCK, HEAD_DIM:] = jnp.where(lane == blk, 1.0, 0.0).astype(BF16)

    far_bias = relb_ref[NUM_BUCKETS - 1, h]
    io = idx_own_ref[...]
    ia = idx_adj_ref[...]
    own = jnp.full((BLOCK, BLOCK), NEG, F32)
    adj = jnp.zeros((BLOCK, BLOCK), F32)
    for b in range(NUM_BUCKETS):
        t = (relb_ref[b, h] - far_bias) * LOG2E
        own = jnp.where(io == b, t, own)
        adj = jnp.where(ia == b, t, adj)
    bown_ref[...] = own
    badj_ref[...] = adj

    def stage(g, carry):
        rows = pl.ds(pl.multiple_of(g * gk, gk), gk)
        kaug_ref[g, :, :HEAD_DIM] = ks_ref[rows, :]
        vt_ref[g] = v_ref[rows, :].astype(F32).T.astype(BF16)
        return carry

    lax.fori_loop(0, nb // GROUP, stage, 0)

    km = km_ref[...]
    km_hi = km.astype(BF16)
    km_lo = (km - km_hi.astype(F32)).astype(BF16)

    def route(c, carry):
        rows = pl.ds(pl.multiple_of(c * gk, gk), gk)
        q_t = q_ref[rows, :].astype(F32).T.astype(BF16)
        gate = dot(km_hi, q_t) + dot(km_lo, q_t)
        blk_id = lax.broadcasted_iota(jnp.int32, gate.shape, 0)
        qb = c * GROUP + lax.shift_right_logical(
            lax.broadcasted_iota(jnp.int32, gate.shape, 1), int(math.log2(BLOCK)))
        gate = jnp.where(blk_id < qb, gate, -jnp.inf)
        sel = jnp.zeros(gate.shape, jnp.bool_)
        for r in range(min(TOPK, nb)):
            mx = jnp.max(gate, axis=0, keepdims=True)
            first = jnp.min(jnp.where(gate == mx, blk_id, nb), axis=0, keepdims=True)
            pick = (blk_id == first) & (qb > r)
            sel = sel | pick
            gate = jnp.where(pick, -jnp.inf, gate)
        sel_far = jnp.where(sel & (blk_id < qb - 1), 0.0, NEG)
        sel_rows = jnp.concatenate(
            [sel_far, jnp.zeros((HEAD_DIM - nb, gk), F32)], axis=0).astype(BF16)
        adj_sel = jnp.max(jnp.where(sel & (blk_id == qb - 1), 1.0, 0.0), axis=0, keepdims=True)
        pen = jnp.where(adj_sel > 0.0, 0.0, NEG)
        for t in range(GROUP // QPAIR):
            cols = slice(t * pw, (t + 1) * pw)
            qat_ref[c * (GROUP // QPAIR) + t, :HEAD_DIM, :] = q_t[:, cols]
            qat_ref[c * (GROUP // QPAIR) + t, HEAD_DIM:, :] = sel_rows[:, cols]
        for j in range(GROUP):
            cols = slice(j * BLOCK, (j + 1) * BLOCK)
            pen_ref[c * GROUP + j] = jnp.broadcast_to(pen[:, cols], (8, BLOCK))
        return carry

    lax.fori_loop(0, nb // GROUP, route, 0)

    def colmax(s):
        return jnp.max(s, axis=0, keepdims=True)

    def colsum(p):
        return jnp.sum(p, axis=0, keepdims=True)

    def pv(g, first, p):
        keys = slice(first * BLOCK, first * BLOCK + p.shape[0])
        return dot(vt_ref[g, :, keys], p.astype(BF16))

    def do_class(c, carry):
        def far_logits(g, t):
            return dot(kaug_ref[g], qat_ref[c * (GROUP // QPAIR) + t])

        def far_park(g, t, s_ref):
            s = far_logits(g, t)
            s_ref[...] = s
            return colmax(s)

        lanes_of = [slice((j % QPAIR) * BLOCK, (j % QPAIR + 1) * BLOCK) for j in range(GROUP)]
        near = []
        for j in range(GROUP):
            qb = c * GROUP + j
            pair = c * (GROUP // QPAIR) + j // QPAIR
            q_t = qat_ref[pair, :HEAD_DIM, lanes_of[j]]
            parts = [(dot(kaug_ref[c, j * BLOCK:(j + 1) * BLOCK, :HEAD_DIM], q_t) + bown_ref[...], c, j)]
            if qb > 0:
                g_adj, t_adj = (c, j - 1) if j > 0 else (c - 1, GROUP - 1)
                k_adj = kaug_ref[g_adj, t_adj * BLOCK:(t_adj + 1) * BLOCK, :HEAD_DIM]
                parts.append((dot(k_adj, q_t) + (badj_ref[...] + pen_ref[qb, 0:1, :]), g_adj, t_adj))
            if j >= 2:
                parts.append((dot(kaug_ref[c, :(j - 1) * BLOCK, :], qat_ref[pair, :, lanes_of[j]]), c, 0))
            near.append(parts)

        if c > 0:
            cmax_a = far_park(0, 0, sa_ref)

        for j, parts in enumerate(near):
            m = functools.reduce(jnp.maximum, [colmax(s) for s, _, _ in parts])
            probs = [(jnp.exp2(s - m), g, first) for s, g, first in parts]
            m_ref[j // QPAIR, :, lanes_of[j]] = m
            l_ref[j // QPAIR, :, lanes_of[j]] = functools.reduce(jnp.add, [colsum(p) for p, _, _ in probs])
            acc_ref[j // QPAIR, :, lanes_of[j]] = functools.reduce(
                jnp.add, [pv(g, first, p) for p, g, first in probs])

        def far_update(g, t, s_ref, cmax):
            m_old = m_ref[t]
            m_new = jnp.maximum(m_old, cmax)
            alpha = jnp.exp2(m_old - m_new)
            p = jnp.exp2(s_ref[...] - m_new)
            l_ref[t] = alpha * l_ref[t] + colsum(p)
            acc_ref[t] = alpha * acc_ref[t] + pv(g, 0, p)
            m_ref[t] = m_new

        for g in range(c):
            cmax_b = far_park(g, 1, sb_ref)
            far_update(g, 0, sa_ref, cmax_a)
            if g + 1 < c:
                cmax_a = far_park(g + 1, 0, sa_ref)
            far_update(g, 1, sb_ref, cmax_b)

        for j in range(GROUP):
            rows = pl.ds((c * GROUP + j) * BLOCK, BLOCK)
            out_t = acc_ref[j // QPAIR, :, lanes_of[j]] * (1.0 / l_ref[j // QPAIR, :, lanes_of[j]])
            o_ref[rows, :] = (out_t.T * sz_ref[rows, :]).astype(o_ref.dtype)
        return carry

    one = jnp.minimum(h + 1, 1)
    for c in range(nb // GROUP):
        lax.fori_loop(0, one, lambda _, carry, c=c: do_class(c, carry), 0)


def _moba_attention(q, ks, v, kmean, sz, rel_bias):
    s, attn_w = q.shape
    nh = attn_w // HEAD_DIM
    nb = s // BLOCK
    assert nb % GROUP == 0 and nb <= HEAD_DIM
    idx_own, idx_adj = _bucket_tables()
    head_cols = pl.BlockSpec((s, HEAD_DIM), lambda h: (0, h))
    const_tile = pl.BlockSpec((BLOCK, BLOCK), lambda h: (0, 0))
    return pl.pallas_call(
        functools.partial(_attn_kernel, nb=nb),
        grid=(nh,),
        in_specs=[pl.BlockSpec(memory_space=pltpu.SMEM),
                  const_tile, const_tile,
                  head_cols, head_cols, head_cols,
                  pl.BlockSpec((nb, HEAD_DIM), lambda h: (0, h)),
                  head_cols],
        out_specs=head_cols,
        out_shape=jax.ShapeDtypeStruct((s, attn_w), BF16),
        scratch_shapes=[pltpu.VMEM((nb // GROUP, GROUP * BLOCK, 2 * HEAD_DIM), BF16),
                        pltpu.VMEM((nb // GROUP, HEAD_DIM, GROUP * BLOCK), BF16),
                        pltpu.VMEM((BLOCK, BLOCK), F32),
                        pltpu.VMEM((BLOCK, BLOCK), F32),
                        pltpu.VMEM((nb // QPAIR, 2 * HEAD_DIM, QPAIR * BLOCK), BF16),
                        pltpu.VMEM((nb, 8, BLOCK), F32),
                        pltpu.VMEM((GROUP // QPAIR, 1, QPAIR * BLOCK), F32),
                        pltpu.VMEM((GROUP // QPAIR, 1, QPAIR * BLOCK), F32),
                        pltpu.VMEM((GROUP // QPAIR, HEAD_DIM, QPAIR * BLOCK), F32),
                        pltpu.VMEM((GROUP * BLOCK, QPAIR * BLOCK), F32),
                        pltpu.VMEM((GROUP * BLOCK, QPAIR * BLOCK), F32)],
        compiler_params=pltpu.CompilerParams(
            dimension_semantics=("arbitrary",),
            vmem_limit_bytes=V7X_VMEM_LIMIT_BYTES),
        name="moba_attn",
    )(rel_bias, jnp.asarray(idx_own), jnp.asarray(idx_adj), q, ks, v, kmean, sz)


def _out_kernel(a_ref, c_ref, wa_ref, wc_ref, x_ref, g_ref, o_ref, *, nj, tn):
    j = pl.program_id(1)
    acc = (jnp.dot(a_ref[...], wa_ref[...], preferred_element_type=F32)
           + jnp.dot(c_ref[...], wc_ref[...], preferred_element_type=F32))
    o_ref[:, pl.ds(pl.multiple_of(j * tn, tn), tn)] = x_ref[...] + acc

    @pl.when(j == nj - 1)
    def _():
        ss = jnp.zeros((o_ref.shape[0], 1), F32)
        for jj in range(nj):
            hj = o_ref[:, jj * tn:(jj + 1) * tn]
            ss = ss + jnp.sum(hj * hj, axis=-1, keepdims=True)
        inv = lax.rsqrt(ss / (nj * tn) + EPS)
        for jj in range(nj):
            cols = slice(jj * tn, (jj + 1) * tn)
            o_ref[:, cols] = o_ref[:, cols] * inv * g_ref[:, cols]


def _out_proj(attn, conv, w_out, x, g, tm=512, tn=1024):
    s, d = x.shape
    ka = attn.shape[1]
    kc = conv.shape[1]
    assert ka == kc
    nj = d // tn
    return pl.pallas_call(
        functools.partial(_out_kernel, nj=nj, tn=tn),
        grid=(s // tm, nj),
        in_specs=[pl.BlockSpec((tm, ka), lambda i, j: (i, 0)),
                  pl.BlockSpec((tm, kc), lambda i, j: (i, 0)),
                  pl.BlockSpec((ka, tn), lambda i, j: (0, j)),
                  pl.BlockSpec((kc, tn), lambda i, j: (1, j)),
                  pl.BlockSpec((tm, tn), lambda i, j: (i, j)),
                  pl.BlockSpec((1, d), lambda i, j: (0, 0))],
        out_specs=pl.BlockSpec((tm, d), lambda i, j: (i, 0)),
        out_shape=jax.ShapeDtypeStruct((s, d), F32),
        compiler_params=pltpu.CompilerParams(
            dimension_semantics=("arbitrary", "arbitrary"),
            vmem_limit_bytes=V7X_VMEM_LIMIT_BYTES),
        name="out_proj_norm",
    )(attn, conv, w_out, w_out, x, g.reshape(1, d))


def kernel(x, norm_gain, w_in, conv_w, w_out, rel_bias, final_gain):
    b, s, d = x.shape
    depth = norm_gain.shape[0]
    assert b == 1 and depth == 1, "kernel is written for one sequence and one layer"
    attn_w = d // 2
    conv_width = d - attn_w
    assert rel_bias.shape == (NUM_BUCKETS, attn_w // HEAD_DIM)
    assert w_in.shape[2] == 4 * attn_w + 4 * conv_width

    x2 = x[0]
    w_in_b = w_in[0].astype(BF16)
    w_out_b = w_out[0].astype(BF16)

    u = _rmsnorm_bf16(x2, norm_gain[0])
    q, ks, v, sz, kmean = _proj_attn(u, w_in_b, attn_w)
    conv = _proj_conv(u, w_in_b, conv_w[0], 4 * attn_w, conv_width)
    attn = _moba_attention(q, ks, v, kmean.reshape(s // BLOCK, attn_w), sz, rel_bias)
    out = _out_proj(attn, conv, w_out_b, x2, final_gain)
    return out[None]
```

```python
import functools
import math

import numpy as np
import jax
import jax.numpy as jnp
from jax import lax
from jax.experimental import pallas as pl
from jax.experimental.pallas import tpu as pltpu

HEAD_DIM = 128
BLOCK = 256
TOPK = 3
NUM_BUCKETS = 32
MAX_DISTANCE = 128
EPS = 1e-6
NEG = -1e30
LOG2E = math.log2(math.e)

F32 = jnp.float32
BF16 = jnp.bfloat16

V7X_VMEM_LIMIT_BYTES = 56 * 1024 * 1024


def _silu(z):
    return z * (1.0 / (1.0 + jnp.exp(-z)))


def _rmsnorm_kernel(x_ref, g_ref, o_ref):
    x = x_ref[...]
    ms = jnp.mean(x * x, axis=-1, keepdims=True)
    o_ref[...] = (x * lax.rsqrt(ms + EPS) * g_ref[...]).astype(o_ref.dtype)


def _rmsnorm_bf16(x, g, tm=256):
    s, d = x.shape
    return pl.pallas_call(
        _rmsnorm_kernel,
        grid=(s // tm,),
        in_specs=[pl.BlockSpec((tm, d), lambda i: (i, 0)),
                  pl.BlockSpec((1, d), lambda i: (0, 0))],
        out_specs=pl.BlockSpec((tm, d), lambda i: (i, 0)),
        out_shape=jax.ShapeDtypeStruct((s, d), BF16),
        compiler_params=pltpu.CompilerParams(dimension_semantics=("arbitrary",)),
        name="rmsnorm_in",
    )(x, g.reshape(1, d))


def _proj_attn_kernel(u_ref, wq_ref, wk_ref, wv_ref, wz_ref,
                      q_ref, ks_ref, v_ref, sz_ref, km_ref, *, scale):
    u = u_ref[...]
    tm = u.shape[0]
    q = jnp.dot(u, wq_ref[...], preferred_element_type=F32)
    q_ref[...] = q.astype(BF16)
    k = jnp.dot(u, wk_ref[...], preferred_element_type=F32)
    ks_ref[...] = (k * scale).astype(BF16)
    km_ref[0] = jnp.mean(k.reshape(tm // BLOCK, BLOCK, k.shape[1]), axis=1)
    v = jnp.dot(u, wv_ref[...], preferred_element_type=F32)
    v_ref[...] = v.astype(BF16)
    z = jnp.dot(u, wz_ref[...], preferred_element_type=F32)
    sz_ref[...] = _silu(z)


def _proj_attn(u, w, attn_w, tm=1024, tn=256):
    s, d = u.shape
    nj = attn_w // tn
    w_spec = lambda part: pl.BlockSpec((d, tn), lambda j, i, part=part: (0, part * nj + j))
    tile = pl.BlockSpec((tm, tn), lambda j, i: (i, j))
    return pl.pallas_call(
        functools.partial(_proj_attn_kernel, scale=HEAD_DIM ** -0.5 * LOG2E),
        grid=(nj, s // tm),
        in_specs=[pl.BlockSpec((tm, d), lambda j, i: (i, 0)),
                  w_spec(0), w_spec(1), w_spec(2), w_spec(3)],
        out_specs=[tile, tile, tile, tile,
                   pl.BlockSpec((1, tm // BLOCK, tn), lambda j, i: (i, 0, j))],
        out_shape=[jax.ShapeDtypeStruct((s, attn_w), BF16),
                   jax.ShapeDtypeStruct((s, attn_w), BF16),
                   jax.ShapeDtypeStruct((s, attn_w), BF16),
                   jax.ShapeDtypeStruct((s, attn_w), F32),
                   jax.ShapeDtypeStruct((s // tm, tm // BLOCK, attn_w), F32)],
        compiler_params=pltpu.CompilerParams(
            dimension_semantics=("arbitrary", "arbitrary"),
            vmem_limit_bytes=V7X_VMEM_LIMIT_BYTES),
        name="proj_attn",
    )(u, w, w, w, w)


def _proj_conv_kernel(u_ref, wh_ref, wb_ref, wc_ref, wz_ref, cw_ref, o_ref, carry_ref):
    @pl.when(pl.program_id(1) == 0)
    def _():
        carry_ref[...] = jnp.zeros_like(carry_ref)

    u = u_ref[...]
    tm = u.shape[0]
    hc = jnp.dot(u, wh_ref[...], preferred_element_type=F32)
    c = jnp.dot(u, wc_ref[...], preferred_element_type=F32)
    p = c * hc
    rows = lax.broadcasted_iota(jnp.int32, p.shape, 0)
    prev = carry_ref[...]
    prev1 = prev[7:8, :]
    prev2 = prev[6:7, :]
    p1 = jnp.where(rows == 0, prev1, pltpu.roll(p, 1, 0))
    p2 = jnp.where(rows == 0, prev2, jnp.where(rows == 1, prev1, pltpu.roll(p, 2, 0)))
    carry_ref[...] = p[tm - 8:, :]
    cw = cw_ref[...]
    y = p2 * cw[0:1, :] + p1 * cw[1:2, :] + p * cw[2:3, :]
    b = jnp.dot(u, wb_ref[...], preferred_element_type=F32)
    z = jnp.dot(u, wz_ref[...], preferred_element_type=F32)
    o_ref[...] = (b * y * _silu(z)).astype(o_ref.dtype)


def _proj_conv(u, w, conv_w, col0, conv_width, tm=1024, tn=256):
    s, d = u.shape
    nj = conv_width // tn
    base = col0 // tn
    w_spec = lambda part: pl.BlockSpec((d, tn), lambda j, i, part=part: (0, base + part * nj + j))
    return pl.pallas_call(
        _proj_conv_kernel,
        grid=(nj, s // tm),
        in_specs=[pl.BlockSpec((tm, d), lambda j, i: (i, 0)),
                  w_spec(0), w_spec(1), w_spec(2), w_spec(3),
                  pl.BlockSpec((conv_w.shape[0], tn), lambda j, i: (0, j))],
        out_specs=pl.BlockSpec((tm, tn), lambda j, i: (i, j)),
        out_shape=jax.ShapeDtypeStruct((s, conv_width), BF16),
        scratch_shapes=[pltpu.VMEM((8, tn), F32)],
        compiler_params=pltpu.CompilerParams(
            dimension_semantics=("arbitrary", "arbitrary"),
            vmem_limit_bytes=V7X_VMEM_LIMIT_BYTES),
        name="proj_conv",
    )(u, w, w, w, w, conv_w)


def _rel_bucket_np(dist):
    n = np.maximum(dist, 0)
    max_exact = NUM_BUCKETS // 2
    nf = np.maximum(n, 1).astype(np.float32)
    ratio = np.log(nf / np.float32(max_exact)) / np.float32(math.log(MAX_DISTANCE / max_exact))
    large = max_exact + (ratio * np.float32(NUM_BUCKETS - max_exact)).astype(np.int32)
    large = np.minimum(large, NUM_BUCKETS - 1)
    return np.where(n < max_exact, n, large).astype(np.int32)


def _bucket_tables():
    key = np.arange(BLOCK)[:, None]
    qry = np.arange(BLOCK)[None, :]
    d_own = qry - key
    own = np.where(d_own >= 0, _rel_bucket_np(d_own), NUM_BUCKETS).astype(np.int32)
    adj = _rel_bucket_np(BLOCK + qry - key)
    return own, adj


GROUP = 4
QPAIR = 2


def _attn_kernel(relb_ref, idx_own_ref, idx_adj_ref, q_ref, ks_ref, v_ref, km_ref, sz_ref,
                 o_ref,
                 kaug_ref, vt_ref, bown_ref, badj_ref, qat_ref, pen_ref, m_ref, l_ref, acc_ref,
                 sa_ref, sb_ref, *, nb):
    h = pl.program_id(0)
    gk = GROUP * BLOCK
    pw = QPAIR * BLOCK
    dot = functools.partial(jnp.dot, preferred_element_type=F32)

    @pl.when(h == 0)
    def _():
        lane = lax.broadcasted_iota(jnp.int32, (BLOCK, HEAD_DIM), 1)
        for blk in range(nb):
            r0 = (blk % GROUP) * BLOCK
            kaug_ref[blk // GROUP, r0:r0 + BLOCK, HEAD_DIM:] = jnp.where(lane == blk, 1.0, 0.0).astype(BF16)

    far_bias = relb_ref[NUM_BUCKETS - 1, h]
    io = idx_own_ref[...]
    ia = idx_adj_ref[...]
    own = jnp.full((BLOCK, BLOCK), NEG, F32)
    adj = jnp.zeros((BLOCK, BLOCK), F32)
    for b in range(NUM_BUCKETS):
        t = (relb_ref[b, h] - far_bias) * LOG2E
        own = jnp.where(io == b, t, own)
        adj = jnp.where(ia == b, t, adj)
    bown_ref[...] = own
    badj_ref[...] = adj

    def stage(g, carry):
        rows = pl.ds(pl.multiple_of(g * gk, gk), gk)
        kaug_ref[g, :, :HEAD_DIM] = ks_ref[rows, :]
        vt_ref[g] = v_ref[rows, :].astype(F32).T.astype(BF16)
        return carry

    lax.fori_loop(0, nb // GROUP, stage, 0)

    km = km_ref[...]
    km_hi = km.astype(BF16)
    km_lo = (km - km_hi.astype(F32)).astype(BF16)

    def route(c, carry):
        rows = pl.ds(pl.multiple_of(c * gk, gk), gk)
        q_t = q_ref[rows, :].astype(F32).T.astype(BF16)
        gate = dot(km_hi, q_t) + dot(km_lo, q_t)
        blk_id = lax.broadcasted_iota(jnp.int32, gate.shape, 0)
        qb = c * GROUP + lax.shift_right_logical(
            lax.broadcasted_iota(jnp.int32, gate.shape, 1), int(math.log2(BLOCK)))
        gate = jnp.where(blk_id < qb, gate, -jnp.inf)
        sel = jnp.zeros(gate.shape, jnp.bool_)
        for r in range(min(TOPK, nb)):
            mx = jnp.max(gate, axis=0, keepdims=True)
            first = jnp.min(jnp.where(gate == mx, blk_id, nb), axis=0, keepdims=True)
            pick = (blk_id == first) & (qb > r)
            sel = sel | pick
            gate = jnp.where(pick, -jnp.inf, gate)
        sel_far = jnp.where(sel & (blk_id < qb - 1), 0.0, NEG)
        sel_rows = jnp.concatenate(
            [sel_far, jnp.zeros((HEAD_DIM - nb, gk), F32)], axis=0).astype(BF16)
        adj_sel = jnp.max(jnp.where(sel & (blk_id == qb - 1), 1.0, 0.0), axis=0, keepdims=True)
        pen = jnp.where(adj_sel > 0.0, 0.0, NEG)
        for t in range(GROUP // QPAIR):
            cols = slice(t * pw, (t + 1) * pw)
            qat_ref[c * (GROUP // QPAIR) + t, :HEAD_DIM, :] = q_t[:, cols]
            qat_ref[c * (GROUP // QPAIR) + t, HEAD_DIM:, :] = sel_rows[:, cols]
        for j in range(GROUP):
            cols = slice(j * BLOCK, (j + 1) * BLOCK)
            pen_ref[c * GROUP + j] = jnp.broadcast_to(pen[:, cols], (8, BLOCK))
        return carry

    lax.fori_loop(0, nb // GROUP, route, 0)

    def colmax(s):
        return jnp.max(s, axis=0, keepdims=True)

    def colsum(p):
        return jnp.sum(p, axis=0, keepdims=True)

    def pv(g, first, p):
        keys = slice(first * BLOCK, first * BLOCK + p.shape[0])
        return dot(vt_ref[g, :, keys], p.astype(BF16))

    def do_class(c, carry):
        def far_logits(g, t):
            return dot(kaug_ref[g], qat_ref[c * (GROUP // QPAIR) + t])

        def far_park(g, t, s_ref):
            s = far_logits(g, t)
            s_ref[...] = s
            return colmax(s)

        lanes_of = [slice((j % QPAIR) * BLOCK, (j % QPAIR + 1) * BLOCK) for j in range(GROUP)]
        near = []
        for j in range(GROUP):
            qb = c * GROUP + j
            pair = c * (GROUP // QPAIR) + j // QPAIR
            q_t = qat_ref[pair, :HEAD_DIM, lanes_of[j]]
            parts = [(dot(kaug_ref[c, j * BLOCK:(j + 1) * BLOCK, :HEAD_DIM], q_t) + bown_ref[...], c, j)]
            if qb > 0:
                g_adj, t_adj = (c, j - 1) if j > 0 else (c - 1, GROUP - 1)
                k_adj = kaug_ref[g_adj, t_adj * BLOCK:(t_adj + 1) * BLOCK, :HEAD_DIM]
                parts.append((dot(k_adj, q_t) + (badj_ref[...] + pen_ref[qb, 0:1, :]), g_adj, t_adj))
            if j >= 2:
                parts.append((dot(kaug_ref[c, :(j - 1) * BLOCK, :], qat_ref[pair, :, lanes_of[j]]), c, 0))
            near.append(parts)

        if c > 0:
            cmax_a = far_park(0, 0, sa_ref)

        for j, parts in enumerate(near):
            m = functools.reduce(jnp.maximum, [colmax(s) for s, _, _ in parts])
            probs = [(jnp.exp2(s - m), g, first) for s, g, first in parts]
            m_ref[j // QPAIR, :, lanes_of[j]] = m
            l_ref[j // QPAIR, :, lanes_of[j]] = functools.reduce(jnp.add, [colsum(p) for p, _, _ in probs])
            acc_ref[j // QPAIR, :, lanes_of[j]] = functools.reduce(
                jnp.add, [pv(g, first, p) for p, g, first in probs])

        def far_update(g, t, s_ref, cmax):
            m_old = m_ref[t]
            m_new = jnp.maximum(m_old, cmax)
            alpha = jnp.exp2(m_old - m_new)
            l_new = alpha * l_ref[t]
            acc_new = alpha * acc_ref[t]
            for k in range(GROUP):
                p = jnp.exp2(s_ref[k * BLOCK:(k + 1) * BLOCK, :] - m_new)
                l_new = l_new + colsum(p)
                acc_new = acc_new + pv(g, k, p)
            l_ref[t] = l_new
            acc_ref[t] = acc_new
            m_ref[t] = m_new

        for g in range(c):
            cmax_b = far_park(g, 1, sb_ref)
            far_update(g, 0, sa_ref, cmax_a)
            if g + 1 < c:
                cmax_a = far_park(g + 1, 0, sa_ref)
            far_update(g, 1, sb_ref, cmax_b)

        for j in range(GROUP):
            rows = pl.ds((c * GROUP + j) * BLOCK, BLOCK)
            out_t = acc_ref[j // QPAIR, :, lanes_of[j]] * (1.0 / l_ref[j // QPAIR, :, lanes_of[j]])
            o_ref[rows, :] = (out_t.T * sz_ref[rows, :]).astype(o_ref.dtype)
        return carry

    one = jnp.minimum(h + 1, 1)
    for c in range(nb // GROUP):
        lax.fori_loop(0, one, lambda _, carry, c=c: do_class(c, carry), 0)


def _moba_attention(q, ks, v, kmean, sz, rel_bias):
    s, attn_w = q.shape
    nh = attn_w // HEAD_DIM
    nb = s // BLOCK
    assert nb % GROUP == 0 and nb <= HEAD_DIM
    idx_own, idx_adj = _bucket_tables()
    head_cols = pl.BlockSpec((s, HEAD_DIM), lambda h: (0, h))
    const_tile = pl.BlockSpec((BLOCK, BLOCK), lambda h: (0, 0))
    return pl.pallas_call(
        functools.partial(_attn_kernel, nb=nb),
        grid=(nh,),
        in_specs=[pl.BlockSpec(memory_space=pltpu.SMEM),
                  const_tile, const_tile,
                  head_cols, head_cols, head_cols,
                  pl.BlockSpec((nb, HEAD_DIM), lambda h: (0, h)),
                  head_cols],
        out_specs=head_cols,
        out_shape=jax.ShapeDtypeStruct((s, attn_w), BF16),
        scratch_shapes=[pltpu.VMEM((nb // GROUP, GROUP * BLOCK, 2 * HEAD_DIM), BF16),
                        pltpu.VMEM((nb // GROUP, HEAD_DIM, GROUP * BLOCK), BF16),
                        pltpu.VMEM((BLOCK, BLOCK), F32),
                        pltpu.VMEM((BLOCK, BLOCK), F32),
                        pltpu.VMEM((nb // QPAIR, 2 * HEAD_DIM, QPAIR * BLOCK), BF16),
                        pltpu.VMEM((nb, 8, BLOCK), F32),
                        pltpu.VMEM((GROUP // QPAIR, 1, QPAIR * BLOCK), F32),
                        pltpu.VMEM((GROUP // QPAIR, 1, QPAIR * BLOCK), F32),
                        pltpu.VMEM((GROUP // QPAIR, HEAD_DIM, QPAIR * BLOCK), F32),
                        pltpu.VMEM((GROUP * BLOCK, QPAIR * BLOCK), F32),
                        pltpu.VMEM((GROUP * BLOCK, QPAIR * BLOCK), F32)],
        compiler_params=pltpu.CompilerParams(
            dimension_semantics=("arbitrary",),
            vmem_limit_bytes=V7X_VMEM_LIMIT_BYTES),
        name="moba_attn",
    )(rel_bias, jnp.asarray(idx_own), jnp.asarray(idx_adj), q, ks, v, kmean, sz)


def _out_kernel(a_ref, c_ref, wa_ref, wc_ref, x_ref, g_ref, o_ref, *, nj, tn):
    j = pl.program_id(1)
    acc = (jnp.dot(a_ref[...], wa_ref[...], preferred_element_type=F32)
           + jnp.dot(c_ref[...], wc_ref[...], preferred_element_type=F32))
    o_ref[:, pl.ds(pl.multiple_of(j * tn, tn), tn)] = x_ref[...] + acc

    @pl.when(j == nj - 1)
    def _():
        ss = jnp.zeros((o_ref.shape[0], 1), F32)
        for jj in range(nj):
            hj = o_ref[:, jj * tn:(jj + 1) * tn]
            ss = ss + jnp.sum(hj * hj, axis=-1, keepdims=True)
        inv = lax.rsqrt(ss / (nj * tn) + EPS)
        for jj in range(nj):
            cols = slice(jj * tn, (jj + 1) * tn)
            o_ref[:, cols] = o_ref[:, cols] * inv * g_ref[:, cols]


def _out_proj(attn, conv, w_out, x, g, tm=512, tn=1024):
    s, d = x.shape
    ka = attn.shape[1]
    kc = conv.shape[1]
    assert ka == kc
    nj = d // tn
    return pl.pallas_call(
        functools.partial(_out_kernel, nj=nj, tn=tn),
        grid=(s // tm, nj),
        in_specs=[pl.BlockSpec((tm, ka), lambda i, j: (i, 0)),
                  pl.BlockSpec((tm, kc), lambda i, j: (i, 0)),
                  pl.BlockSpec((ka, tn), lambda i, j: (0, j)),
                  pl.BlockSpec((kc, tn), lambda i, j: (1, j)),
                  pl.BlockSpec((tm, tn), lambda i, j: (i, j)),
                  pl.BlockSpec((1, d), lambda i, j: (0, 0))],
        out_specs=pl.BlockSpec((tm, d), lambda i, j: (i, 0)),
        out_shape=jax.ShapeDtypeStruct((s, d), F32),
        compiler_params=pltpu.CompilerParams(
            dimension_semantics=("arbitrary", "arbitrary"),
            vmem_limit_bytes=V7X_VMEM_LIMIT_BYTES),
        name="out_proj_norm",
    )(attn, conv, w_out, w_out, x, g.reshape(1, d))


def kernel(x, norm_gain, w_in, conv_w, w_out, rel_bias, final_gain):
    b, s, d = x.shape
    depth = norm_gain.shape[0]
    assert b == 1 and depth == 1, "kernel is written for one sequence and one layer"
    attn_w = d // 2
    conv_width = d - attn_w
    assert rel_bias.shape == (NUM_BUCKETS, attn_w // HEAD_DIM)
    assert w_in.shape[2] == 4 * attn_w + 4 * conv_width

    x2 = x[0]
    w_in_b = w_in[0].astype(BF16)
    w_out_b = w_out[0].astype(BF16)

    u = _rmsnorm_bf16(x2, norm_gain[0])
    q, ks, v, sz, kmean = _proj_attn(u, w_in_b, attn_w)
    conv = _proj_conv(u, w_in_b, conv_w[0], 4 * attn_w, conv_width)
    attn = _moba_attention(q, ks, v, kmean.reshape(s // BLOCK, attn_w), sz, rel_bias)
    out = _out_proj(attn, conv, w_out_b, x2, final_gain)
    return out[None]
```

```python
import functools
import math

import numpy as np
import jax
import jax.numpy as jnp
from jax import lax
from jax.experimental import pallas as pl
from jax.experimental.pallas import tpu as pltpu

HEAD_DIM = 128
BLOCK = 256
TOPK = 3
NUM_BUCKETS = 32
MAX_DISTANCE = 128
EPS = 1e-6
NEG = -1e30
LOG2E = math.log2(math.e)

F32 = jnp.float32
BF16 = jnp.bfloat16

V7X_VMEM_LIMIT_BYTES = 56 * 1024 * 1024


def _silu(z):
    return z * (1.0 / (1.0 + jnp.exp(-z)))


def _rmsnorm_kernel(x_ref, g_ref, o_ref):
    x = x_ref[...]
    ms = jnp.mean(x * x, axis=-1, keepdims=True)
    o_ref[...] = (x * lax.rsqrt(ms + EPS) * g_ref[...]).astype(o_ref.dtype)


def _rmsnorm_bf16(x, g, tm=256):
    s, d = x.shape
    return pl.pallas_call(
        _rmsnorm_kernel,
        grid=(s // tm,),
        in_specs=[pl.BlockSpec((tm, d), lambda i: (i, 0)),
                  pl.BlockSpec((1, d), lambda i: (0, 0))],
        out_specs=pl.BlockSpec((tm, d), lambda i: (i, 0)),
        out_shape=jax.ShapeDtypeStruct((s, d), BF16),
        compiler_params=pltpu.CompilerParams(dimension_semantics=("arbitrary",)),
        name="rmsnorm_in",
    )(x, g.reshape(1, d))


def _cast_pair(wa_ref, wb_ref, wbf_ref, first):
    @pl.when(first)
    def _():
        wbf_ref[0] = wa_ref[...].astype(BF16)
        wbf_ref[1] = wb_ref[...].astype(BF16)


def _proj_qk_kernel(u_ref, wq_ref, wk_ref, q_ref, ks_ref, km_ref, wbf_ref, *, scale):
    _cast_pair(wq_ref, wk_ref, wbf_ref, pl.program_id(1) == 0)
    u = u_ref[...]
    tm = u.shape[0]
    q = jnp.dot(u, wbf_ref[0], preferred_element_type=F32)
    q_ref[...] = q.astype(BF16)
    k = jnp.dot(u, wbf_ref[1], preferred_element_type=F32)
    ks_ref[...] = (k * scale).astype(BF16)
    km_ref[0] = jnp.mean(k.reshape(tm // BLOCK, BLOCK, k.shape[1]), axis=1)


def _proj_vz_kernel(u_ref, wv_ref, wz_ref, v_ref, sz_ref, wbf_ref):
    _cast_pair(wv_ref, wz_ref, wbf_ref, pl.program_id(1) == 0)
    u = u_ref[...]
    v = jnp.dot(u, wbf_ref[0], preferred_element_type=F32)
    v_ref[...] = v.astype(BF16)
    z = jnp.dot(u, wbf_ref[1], preferred_element_type=F32)
    sz_ref[...] = _silu(z)


def _proj_pair(kernel_fn, name, u, w_in, part_a, part_b, width, out_specs, out_shape, tm, tn):
    s, d = u.shape
    nj = width // tn
    w_spec = lambda part: pl.BlockSpec((None, d, tn), lambda j, i, part=part: (0, 0, part * nj + j))
    return pl.pallas_call(
        kernel_fn,
        grid=(nj, s // tm),
        in_specs=[pl.BlockSpec((tm, d), lambda j, i: (i, 0)), w_spec(part_a), w_spec(part_b)],
        out_specs=out_specs,
        out_shape=out_shape,
        scratch_shapes=[pltpu.VMEM((2, d, tn), BF16)],
        compiler_params=pltpu.CompilerParams(
            dimension_semantics=("arbitrary", "arbitrary"),
            vmem_limit_bytes=V7X_VMEM_LIMIT_BYTES),
        name=name,
    )(u, w_in, w_in)


def _proj_attn(u, w_in, attn_w, tm=1024, tn=256):
    s, _ = u.shape
    tile = pl.BlockSpec((tm, tn), lambda j, i: (i, j))
    act = jax.ShapeDtypeStruct((s, attn_w), BF16)
    q, ks, kmean = _proj_pair(
        functools.partial(_proj_qk_kernel, scale=HEAD_DIM ** -0.5 * LOG2E), "proj_qk",
        u, w_in, 0, 1, attn_w,
        [tile, tile, pl.BlockSpec((1, tm // BLOCK, tn), lambda j, i: (i, 0, j))],
        [act, act, jax.ShapeDtypeStruct((s // tm, tm // BLOCK, attn_w), F32)], tm, tn)
    v, sz = _proj_pair(
        _proj_vz_kernel, "proj_vz", u, w_in, 2, 3, attn_w,
        [tile, tile], [act, jax.ShapeDtypeStruct((s, attn_w), F32)], tm, tn)
    return q, ks, v, sz, kmean


def _proj_conv_kernel(u_ref, wa_ref, wb_ref, cw_ref, o_ref, wbf_ref, y_ref, carry_ref):
    half = pl.program_id(1)
    i = pl.program_id(2)
    _cast_pair(wa_ref, wb_ref, wbf_ref, i == 0)
    u = u_ref[...]
    tm = u.shape[0]
    a = jnp.dot(u, wbf_ref[0], preferred_element_type=F32)
    b = jnp.dot(u, wbf_ref[1], preferred_element_type=F32)

    @pl.when(half == 0)
    def _():
        @pl.when(i == 0)
        def _():
            carry_ref[...] = jnp.zeros_like(carry_ref)

        p = b * a
        rows = lax.broadcasted_iota(jnp.int32, p.shape, 0)
        prev = carry_ref[...]
        prev1 = prev[7:8, :]
        prev2 = prev[6:7, :]
        p1 = jnp.where(rows == 0, prev1, pltpu.roll(p, 1, 0))
        p2 = jnp.where(rows == 0, prev2, jnp.where(rows == 1, prev1, pltpu.roll(p, 2, 0)))
        carry_ref[...] = p[tm - 8:, :]
        cw = cw_ref[...]
        y_ref[i] = p2 * cw[0:1, :] + p1 * cw[1:2, :] + p * cw[2:3, :]
        o_ref[...] = jnp.zeros_like(o_ref)

    @pl.when(half == 1)
    def _():
        o_ref[...] = (a * y_ref[i] * _silu(b)).astype(o_ref.dtype)


def _proj_conv(u, w_in, conv_w, first_part, conv_width, tm=1024, tn=256):
    s, d = u.shape
    nj = conv_width // tn
    w_spec = lambda k: pl.BlockSpec(
        (None, d, tn), lambda j, half, i, k=k: (0, 0, (first_part + half + 2 * k) * nj + j))
    return pl.pallas_call(
        _proj_conv_kernel,
        grid=(nj, 2, s // tm),
        in_specs=[pl.BlockSpec((tm, d), lambda j, half, i: (i, 0)),
                  w_spec(0), w_spec(1),
                  pl.BlockSpec((conv_w.shape[0], tn), lambda j, half, i: (0, j))],
        out_specs=pl.BlockSpec((tm, tn), lambda j, half, i: (i, j)),
        out_shape=jax.ShapeDtypeStruct((s, conv_width), BF16),
        scratch_shapes=[pltpu.VMEM((2, d, tn), BF16),
                        pltpu.VMEM((s // tm, tm, tn), F32),
                        pltpu.VMEM((8, tn), F32)],
        compiler_params=pltpu.CompilerParams(
            dimension_semantics=("arbitrary", "arbitrary", "arbitrary"),
            vmem_limit_bytes=V7X_VMEM_LIMIT_BYTES),
        name="proj_conv",
    )(u, w_in, w_in, conv_w)


def _rel_bucket_np(dist):
    n = np.maximum(dist, 0)
    max_exact = NUM_BUCKETS // 2
    nf = np.maximum(n, 1).astype(np.float32)
    ratio = np.log(nf / np.float32(max_exact)) / np.float32(math.log(MAX_DISTANCE / max_exact))
    large = max_exact + (ratio * np.float32(NUM_BUCKETS - max_exact)).astype(np.int32)
    large = np.minimum(large, NUM_BUCKETS - 1)
    return np.where(n < max_exact, n, large).astype(np.int32)


def _bucket_tables():
    key = np.arange(BLOCK)[:, None]
    qry = np.arange(BLOCK)[None, :]
    d_own = qry - key
    own = np.where(d_own >= 0, _rel_bucket_np(d_own), NUM_BUCKETS).astype(np.int32)
    adj = _rel_bucket_np(BLOCK + qry - key)
    return own, adj


GROUP = 4
QPAIR = 2


def _attn_kernel(relb_ref, idx_own_ref, idx_adj_ref, q_ref, ks_ref, v_ref, km_ref, sz_ref,
                 o_ref,
                 kaug_ref, vt_ref, bown_ref, badj_ref, qat_ref, pen_ref, m_ref, l_ref, acc_ref,
                 sa_ref, sb_ref, *, nb):
    h = pl.program_id(0)
    gk = GROUP * BLOCK
    pw = QPAIR * BLOCK
    dot = functools.partial(jnp.dot, preferred_element_type=F32)

    @pl.when(h == 0)
    def _():
        lane = lax.broadcasted_iota(jnp.int32, (BLOCK, HEAD_DIM), 1)
        for blk in range(nb):
            r0 = (blk % GROUP) * BLOCK
            kaug_ref[blk // GROUP, r0:r0 + BLOCK, HEAD_DIM:] = jnp.where(lane == blk, 1.0, 0.0).astype(BF16)

    far_bias = relb_ref[NUM_BUCKETS - 1, h]
    io = idx_own_ref[...]
    ia = idx_adj_ref[...]
    own = jnp.full((BLOCK, BLOCK), NEG, F32)
    adj = jnp.zeros((BLOCK, BLOCK), F32)
    for b in range(NUM_BUCKETS):
        t = (relb_ref[b, h] - far_bias) * LOG2E
        own = jnp.where(io == b, t, own)
        adj = jnp.where(ia == b, t, adj)
    bown_ref[...] = own
    badj_ref[...] = adj

    def stage(g, carry):
        rows = pl.ds(pl.multiple_of(g * gk, gk), gk)
        kaug_ref[g, :, :HEAD_DIM] = ks_ref[rows, :]
        vt_ref[g] = v_ref[rows, :].astype(F32).T.astype(BF16)
        return carry

    lax.fori_loop(0, nb // GROUP, stage, 0)

    km = km_ref[...]
    km_hi = km.astype(BF16)
    km_lo = (km - km_hi.astype(F32)).astype(BF16)

    def route(c, carry):
        rows = pl.ds(pl.multiple_of(c * gk, gk), gk)
        q_t = q_ref[rows, :].astype(F32).T.astype(BF16)
        gate = dot(km_hi, q_t) + dot(km_lo, q_t)
        blk_id = lax.broadcasted_iota(jnp.int32, gate.shape, 0)
        qb = c * GROUP + lax.shift_right_logical(
            lax.broadcasted_iota(jnp.int32, gate.shape, 1), int(math.log2(BLOCK)))
        gate = jnp.where(blk_id < qb, gate, -jnp.inf)
        sel = jnp.zeros(gate.shape, jnp.bool_)
        for r in range(min(TOPK, nb)):
            mx = jnp.max(gate, axis=0, keepdims=True)
            first = jnp.min(jnp.where(gate == mx, blk_id, nb), axis=0, keepdims=True)
            pick = (blk_id == first) & (qb > r)
            sel = sel | pick
            gate = jnp.where(pick, -jnp.inf, gate)
        sel_far = jnp.where(sel & (blk_id < qb - 1), 0.0, NEG)
        sel_rows = jnp.concatenate(
            [sel_far, jnp.zeros((HEAD_DIM - nb, gk), F32)], axis=0).astype(BF16)
        adj_sel = jnp.max(jnp.where(sel & (blk_id == qb - 1), 1.0, 0.0), axis=0, keepdims=True)
        pen = jnp.where(adj_sel > 0.0, 0.0, NEG)
        for t in range(GROUP // QPAIR):
            cols = slice(t * pw, (t + 1) * pw)
            qat_ref[c * (GROUP // QPAIR) + t, :HEAD_DIM, :] = q_t[:, cols]
            qat_ref[c * (GROUP // QPAIR) + t, HEAD_DIM:, :] = sel_rows[:, cols]
        for j in range(GROUP):
            cols = slice(j * BLOCK, (j + 1) * BLOCK)
            pen_ref[c * GROUP + j] = jnp.broadcast_to(pen[:, cols], (8, BLOCK))
        return carry

    lax.fori_loop(0, nb // GROUP, route, 0)

    def colmax(s):
        return jnp.max(s, axis=0, keepdims=True)

    def colsum(p):
        return jnp.sum(p, axis=0, keepdims=True)

    def pv(g, first, p):
        keys = slice(first * BLOCK, first * BLOCK + p.shape[0])
        return dot(vt_ref[g, :, keys], p.astype(BF16))

    def do_class(c, carry):
        def far_logits(g, t):
            return dot(kaug_ref[g], qat_ref[c * (GROUP // QPAIR) + t])

        def far_park(g, t, s_ref):
            s = far_logits(g, t)
            s_ref[...] = s
            return colmax(s)

        lanes_of = [slice((j % QPAIR) * BLOCK, (j % QPAIR + 1) * BLOCK) for j in range(GROUP)]
        near = []
        for j in range(GROUP):
            qb = c * GROUP + j
            pair = c * (GROUP // QPAIR) + j // QPAIR
            q_t = qat_ref[pair, :HEAD_DIM, lanes_of[j]]
            parts = [(dot(kaug_ref[c, j * BLOCK:(j + 1) * BLOCK, :HEAD_DIM], q_t) + bown_ref[...], c, j)]
            if qb > 0:
                g_adj, t_adj = (c, j - 1) if j > 0 else (c - 1, GROUP - 1)
                k_adj = kaug_ref[g_adj, t_adj * BLOCK:(t_adj + 1) * BLOCK, :HEAD_DIM]
                parts.append((dot(k_adj, q_t) + (badj_ref[...] + pen_ref[qb, 0:1, :]), g_adj, t_adj))
            if j >= 2:
                parts.append((dot(kaug_ref[c, :(j - 1) * BLOCK, :], qat_ref[pair, :, lanes_of[j]]), c, 0))
            near.append(parts)

        if c > 0:
            cmax_a = far_park(0, 0, sa_ref)

        for j, parts in enumerate(near):
            m = functools.reduce(jnp.maximum, [colmax(s) for s, _, _ in parts])
            probs = [(jnp.exp2(s - m), g, first) for s, g, first in parts]
            m_ref[j // QPAIR, :, lanes_of[j]] = m
            l_ref[j // QPAIR, :, lanes_of[j]] = functools.reduce(jnp.add, [colsum(p) for p, _, _ in probs])
            acc_ref[j // QPAIR, :, lanes_of[j]] = functools.reduce(
                jnp.add, [pv(g, first, p) for p, g, first in probs])

        def far_update(g, t, s_ref, cmax):
            m_old = m_ref[t]
            m_new = jnp.maximum(m_old, cmax)
            alpha = jnp.exp2(m_old - m_new)
            l_new = alpha * l_ref[t]
            acc_new = alpha * acc_ref[t]
            for k in range(GROUP):
                p = jnp.exp2(s_ref[k * BLOCK:(k + 1) * BLOCK, :] - m_new)
                l_new = l_new + colsum(p)
                acc_new = acc_new + pv(g, k, p)
            l_ref[t] = l_new
            acc_ref[t] = acc_new
            m_ref[t] = m_new

        for g in range(c):
            cmax_b = far_park(g, 1, sb_ref)
            far_update(g, 0, sa_ref, cmax_a)
            if g + 1 < c:
                cmax_a = far_park(g + 1, 0, sa_ref)
            far_update(g, 1, sb_ref, cmax_b)

        for j in range(GROUP):
            rows = pl.ds((c * GROUP + j) * BLOCK, BLOCK)
            out_t = acc_ref[j // QPAIR, :, lanes_of[j]] * (1.0 / l_ref[j // QPAIR, :, lanes_of[j]])
            o_ref[rows, :] = (out_t.T * sz_ref[rows, :]).astype(o_ref.dtype)
        return carry

    one = jnp.minimum(h + 1, 1)
    for c in range(nb // GROUP):
        lax.fori_loop(0, one, lambda _, carry, c=c: do_class(c, carry), 0)


def _moba_attention(q, ks, v, kmean, sz, rel_bias):
    s, attn_w = q.shape
    nh = attn_w // HEAD_DIM
    nb = s // BLOCK
    assert nb % GROUP == 0 and nb <= HEAD_DIM
    idx_own, idx_adj = _bucket_tables()
    head_cols = pl.BlockSpec((s, HEAD_DIM), lambda h: (0, h))
    const_tile = pl.BlockSpec((BLOCK, BLOCK), lambda h: (0, 0))
    return pl.pallas_call(
        functools.partial(_attn_kernel, nb=nb),
        grid=(nh,),
        in_specs=[pl.BlockSpec(memory_space=pltpu.SMEM),
                  const_tile, const_tile,
                  head_cols, head_cols, head_cols,
                  pl.BlockSpec((nb, HEAD_DIM), lambda h: (0, h)),
                  head_cols],
        out_specs=head_cols,
        out_shape=jax.ShapeDtypeStruct((s, attn_w), BF16),
        scratch_shapes=[pltpu.VMEM((nb // GROUP, GROUP * BLOCK, 2 * HEAD_DIM), BF16),
                        pltpu.VMEM((nb // GROUP, HEAD_DIM, GROUP * BLOCK), BF16),
                        pltpu.VMEM((BLOCK, BLOCK), F32),
                        pltpu.VMEM((BLOCK, BLOCK), F32),
                        pltpu.VMEM((nb // QPAIR, 2 * HEAD_DIM, QPAIR * BLOCK), BF16),
                        pltpu.VMEM((nb, 8, BLOCK), F32),
                        pltpu.VMEM((GROUP // QPAIR, 1, QPAIR * BLOCK), F32),
                        pltpu.VMEM((GROUP // QPAIR, 1, QPAIR * BLOCK), F32),
                        pltpu.VMEM((GROUP // QPAIR, HEAD_DIM, QPAIR * BLOCK), F32),
                        pltpu.VMEM((GROUP * BLOCK, QPAIR * BLOCK), F32),
                        pltpu.VMEM((GROUP * BLOCK, QPAIR * BLOCK), F32)],
        compiler_params=pltpu.CompilerParams(
            dimension_semantics=("arbitrary",),
            vmem_limit_bytes=V7X_VMEM_LIMIT_BYTES),
        name="moba_attn",
    )(rel_bias, jnp.asarray(idx_own), jnp.asarray(idx_adj), q, ks, v, kmean, sz)


def _out_kernel(a_ref, c_ref, wa_ref, wc_ref, x_ref, g_ref, o_ref, *, nj, tn):
    j = pl.program_id(1)
    acc = (jnp.dot(a_ref[...], wa_ref[...], preferred_element_type=F32)
           + jnp.dot(c_ref[...], wc_ref[...], preferred_element_type=F32))
    o_ref[:, pl.ds(pl.multiple_of(j * tn, tn), tn)] = x_ref[...] + acc

    @pl.when(j == nj - 1)
    def _():
        ss = jnp.zeros((o_ref.shape[0], 1), F32)
        for jj in range(nj):
            hj = o_ref[:, jj * tn:(jj + 1) * tn]
            ss = ss + jnp.sum(hj * hj, axis=-1, keepdims=True)
        inv = lax.rsqrt(ss / (nj * tn) + EPS)
        for jj in range(nj):
            cols = slice(jj * tn, (jj + 1) * tn)
            o_ref[:, cols] = o_ref[:, cols] * inv * g_ref[:, cols]


def _out_proj(attn, conv, w_out, x, g, tm=512, tn=1024):
    s, d = x.shape
    ka = attn.shape[1]
    kc = conv.shape[1]
    assert ka == kc
    nj = d // tn
    return pl.pallas_call(
        functools.partial(_out_kernel, nj=nj, tn=tn),
        grid=(s // tm, nj),
        in_specs=[pl.BlockSpec((tm, ka), lambda i, j: (i, 0)),
                  pl.BlockSpec((tm, kc), lambda i, j: (i, 0)),
                  pl.BlockSpec((ka, tn), lambda i, j: (0, j)),
                  pl.BlockSpec((kc, tn), lambda i, j: (1, j)),
                  pl.BlockSpec((tm, tn), lambda i, j: (i, j)),
                  pl.BlockSpec((1, d), lambda i, j: (0, 0))],
        out_specs=pl.BlockSpec((tm, d), lambda i, j: (i, 0)),
        out_shape=jax.ShapeDtypeStruct((s, d), F32),
        compiler_params=pltpu.CompilerParams(
            dimension_semantics=("arbitrary", "arbitrary"),
            vmem_limit_bytes=V7X_VMEM_LIMIT_BYTES),
        name="out_proj_norm",
    )(attn, conv, w_out, w_out, x, g.reshape(1, d))


def kernel(x, norm_gain, w_in, conv_w, w_out, rel_bias, final_gain):
    b, s, d = x.shape
    depth = norm_gain.shape[0]
    assert b == 1 and depth == 1, "kernel is written for one sequence and one layer"
    attn_w = d // 2
    conv_width = d - attn_w
    assert rel_bias.shape == (NUM_BUCKETS, attn_w // HEAD_DIM)
    assert w_in.shape[2] == 4 * attn_w + 4 * conv_width

    x2 = x[0]
    assert attn_w == conv_width
    w_out_b = w_out[0].astype(BF16)

    u = _rmsnorm_bf16(x2, norm_gain[0])
    q, ks, v, sz, kmean = _proj_attn(u, w_in, attn_w)
    conv = _proj_conv(u, w_in, conv_w[0], 4, conv_width)
    attn = _moba_attention(q, ks, v, kmean.reshape(s // BLOCK, attn_w), sz, rel_bias)
    out = _out_proj(attn, conv, w_out_b, x2, final_gain)
    return out[None]
```

```python
import functools
import math

import numpy as np
import jax
import jax.numpy as jnp
from jax import lax
from jax.experimental import pallas as pl
from jax.experimental.pallas import tpu as pltpu

HEAD_DIM = 128
BLOCK = 256
TOPK = 3
NUM_BUCKETS = 32
MAX_DISTANCE = 128
EPS = 1e-6
NEG = -1e30
LOG2E = math.log2(math.e)

F32 = jnp.float32
BF16 = jnp.bfloat16

V7X_VMEM_LIMIT_BYTES = 56 * 1024 * 1024


def _silu(z):
    return z * (1.0 / (1.0 + jnp.exp(-z)))


def _rmsnorm_kernel(x_ref, g_ref, o_ref):
    x = x_ref[...]
    ms = jnp.mean(x * x, axis=-1, keepdims=True)
    o_ref[...] = (x * lax.rsqrt(ms + EPS) * g_ref[...]).astype(o_ref.dtype)


def _rmsnorm_bf16(x, g, tm=256):
    s, d = x.shape
    return pl.pallas_call(
        _rmsnorm_kernel,
        grid=(s // tm,),
        in_specs=[pl.BlockSpec((tm, d), lambda i: (i, 0)),
                  pl.BlockSpec((1, d), lambda i: (0, 0))],
        out_specs=pl.BlockSpec((tm, d), lambda i: (i, 0)),
        out_shape=jax.ShapeDtypeStruct((s, d), BF16),
        compiler_params=pltpu.CompilerParams(dimension_semantics=("arbitrary",)),
        name="rmsnorm_in",
    )(x, g.reshape(1, d))


def _proj_attn_kernel(u_ref, wq_ref, wk_ref, wv_ref, wz_ref,
                      q_ref, ks_ref, v_ref, sz_ref, km_ref, *, scale):
    u = u_ref[...]
    tm = u.shape[0]
    q = jnp.dot(u, wq_ref[...], preferred_element_type=F32)
    q_ref[...] = q.astype(BF16)
    k = jnp.dot(u, wk_ref[...], preferred_element_type=F32)
    ks_ref[...] = (k * scale).astype(BF16)
    km_ref[0] = jnp.mean(k.reshape(tm // BLOCK, BLOCK, k.shape[1]), axis=1)
    v = jnp.dot(u, wv_ref[...], preferred_element_type=F32)
    v_ref[...] = v.astype(BF16)
    z = jnp.dot(u, wz_ref[...], preferred_element_type=F32)
    sz_ref[...] = _silu(z)


def _proj_attn(u, w, attn_w, tm=1024, tn=256):
    s, d = u.shape
    nj = attn_w // tn
    w_spec = lambda part: pl.BlockSpec((d, tn), lambda j, i, part=part: (0, part * nj + j))
    tile = pl.BlockSpec((tm, tn), lambda j, i: (i, j))
    return pl.pallas_call(
        functools.partial(_proj_attn_kernel, scale=HEAD_DIM ** -0.5 * LOG2E),
        grid=(nj, s // tm),
        in_specs=[pl.BlockSpec((tm, d), lambda j, i: (i, 0)),
                  w_spec(0), w_spec(1), w_spec(2), w_spec(3)],
        out_specs=[tile, tile, tile, tile,
                   pl.BlockSpec((1, tm // BLOCK, tn), lambda j, i: (i, 0, j))],
        out_shape=[jax.ShapeDtypeStruct((s, attn_w), BF16),
                   jax.ShapeDtypeStruct((s, attn_w), BF16),
                   jax.ShapeDtypeStruct((s, attn_w), BF16),
                   jax.ShapeDtypeStruct((s, attn_w), F32),
                   jax.ShapeDtypeStruct((s // tm, tm // BLOCK, attn_w), F32)],
        compiler_params=pltpu.CompilerParams(
            dimension_semantics=("arbitrary", "arbitrary"),
            vmem_limit_bytes=V7X_VMEM_LIMIT_BYTES),
        name="proj_attn",
    )(u, w, w, w, w)


def _proj_conv_kernel(u_ref, wh_ref, wb_ref, wc_ref, wz_ref, cw_ref, o_ref, carry_ref):
    @pl.when(pl.program_id(1) == 0)
    def _():
        carry_ref[...] = jnp.zeros_like(carry_ref)

    u = u_ref[...]
    tm = u.shape[0]
    hc = jnp.dot(u, wh_ref[...], preferred_element_type=F32)
    c = jnp.dot(u, wc_ref[...], preferred_element_type=F32)
    p = c * hc
    rows = lax.broadcasted_iota(jnp.int32, p.shape, 0)
    prev = carry_ref[...]
    prev1 = prev[7:8, :]
    prev2 = prev[6:7, :]
    p1 = jnp.where(rows == 0, prev1, pltpu.roll(p, 1, 0))
    p2 = jnp.where(rows == 0, prev2, jnp.where(rows == 1, prev1, pltpu.roll(p, 2, 0)))
    carry_ref[...] = p[tm - 8:, :]
    cw = cw_ref[...]
    y = p2 * cw[0:1, :] + p1 * cw[1:2, :] + p * cw[2:3, :]
    b = jnp.dot(u, wb_ref[...], preferred_element_type=F32)
    z = jnp.dot(u, wz_ref[...], preferred_element_type=F32)
    o_ref[...] = (b * y * _silu(z)).astype(o_ref.dtype)


def _proj_conv(u, w, conv_w, col0, conv_width, tm=1024, tn=256):
    s, d = u.shape
    nj = conv_width // tn
    base = col0 // tn
    w_spec = lambda part: pl.BlockSpec((d, tn), lambda j, i, part=part: (0, base + part * nj + j))
    return pl.pallas_call(
        _proj_conv_kernel,
        grid=(nj, s // tm),
        in_specs=[pl.BlockSpec((tm, d), lambda j, i: (i, 0)),
                  w_spec(0), w_spec(1), w_spec(2), w_spec(3),
                  pl.BlockSpec((conv_w.shape[0], tn), lambda j, i: (0, j))],
        out_specs=pl.BlockSpec((tm, tn), lambda j, i: (i, j)),
        out_shape=jax.ShapeDtypeStruct((s, conv_width), BF16),
        scratch_shapes=[pltpu.VMEM((8, tn), F32)],
        compiler_params=pltpu.CompilerParams(
            dimension_semantics=("arbitrary", "arbitrary"),
            vmem_limit_bytes=V7X_VMEM_LIMIT_BYTES),
        name="proj_conv",
    )(u, w, w, w, w, conv_w)


def _rel_bucket_np(dist):
    n = np.maximum(dist, 0)
    max_exact = NUM_BUCKETS // 2
    nf = np.maximum(n, 1).astype(np.float32)
    ratio = np.log(nf / np.float32(max_exact)) / np.float32(math.log(MAX_DISTANCE / max_exact))
    large = max_exact + (ratio * np.float32(NUM_BUCKETS - max_exact)).astype(np.int32)
    large = np.minimum(large, NUM_BUCKETS - 1)
    return np.where(n < max_exact, n, large).astype(np.int32)


def _bucket_tables():
    key = np.arange(BLOCK)[:, None]
    qry = np.arange(BLOCK)[None, :]
    d_own = qry - key
    own = np.where(d_own >= 0, _rel_bucket_np(d_own), NUM_BUCKETS).astype(np.int32)
    adj = _rel_bucket_np(BLOCK + qry - key)
    return own, adj


GROUP = 4
QPAIR = 2


def _attn_kernel(relb_ref, idx_own_ref, idx_adj_ref, q_ref, ks_ref, v_ref, km_ref, sz_ref,
                 o_ref,
                 kaug_ref, vt_ref, bown_ref, badj_ref, qat_ref, pen_ref, m_ref, l_ref, acc_ref,
                 sa_ref, sb_ref, p1_ref, alpha1_ref, *, nb):
    h = pl.program_id(0)
    gk = GROUP * BLOCK
    pw = QPAIR * BLOCK
    dot = functools.partial(jnp.dot, preferred_element_type=F32)

    @pl.when(h == 0)
    def _():
        lane = lax.broadcasted_iota(jnp.int32, (BLOCK, HEAD_DIM), 1)
        for blk in range(nb):
            r0 = (blk % GROUP) * BLOCK
            kaug_ref[blk // GROUP, r0:r0 + BLOCK, HEAD_DIM:] = jnp.where(lane == blk, 1.0, 0.0).astype(BF16)

    far_bias = relb_ref[NUM_BUCKETS - 1, h]
    io = idx_own_ref[...]
    ia = idx_adj_ref[...]
    own = jnp.full((BLOCK, BLOCK), NEG, F32)
    adj = jnp.zeros((BLOCK, BLOCK), F32)
    for b in range(NUM_BUCKETS):
        t = (relb_ref[b, h] - far_bias) * LOG2E
        own = jnp.where(io == b, t, own)
        adj = jnp.where(ia == b, t, adj)
    bown_ref[...] = own
    badj_ref[...] = adj

    def stage(g, carry):
        rows = pl.ds(pl.multiple_of(g * gk, gk), gk)
        kaug_ref[g, :, :HEAD_DIM] = ks_ref[rows, :]
        vt_ref[g] = v_ref[rows, :].astype(F32).T.astype(BF16)
        return carry

    lax.fori_loop(0, nb // GROUP, stage, 0)

    km = km_ref[...]
    km_hi = km.astype(BF16)
    km_lo = (km - km_hi.astype(F32)).astype(BF16)

    def route(c, carry):
        rows = pl.ds(pl.multiple_of(c * gk, gk), gk)
        q_t = q_ref[rows, :].astype(F32).T.astype(BF16)
        gate = dot(km_hi, q_t) + dot(km_lo, q_t)
        blk_id = lax.broadcasted_iota(jnp.int32, gate.shape, 0)
        qb = c * GROUP + lax.shift_right_logical(
            lax.broadcasted_iota(jnp.int32, gate.shape, 1), int(math.log2(BLOCK)))
        gate = jnp.where(blk_id < qb, gate, -jnp.inf)
        sel = jnp.zeros(gate.shape, jnp.bool_)
        for r in range(min(TOPK, nb)):
            mx = jnp.max(gate, axis=0, keepdims=True)
            first = jnp.min(jnp.where(gate == mx, blk_id, nb), axis=0, keepdims=True)
            pick = (blk_id == first) & (qb > r)
            sel = sel | pick
            gate = jnp.where(pick, -jnp.inf, gate)
        sel_far = jnp.where(sel & (blk_id < qb - 1), 0.0, NEG)
        sel_rows = jnp.concatenate(
            [sel_far, jnp.zeros((HEAD_DIM - nb, gk), F32)], axis=0).astype(BF16)
        adj_sel = jnp.max(jnp.where(sel & (blk_id == qb - 1), 1.0, 0.0), axis=0, keepdims=True)
        pen = jnp.where(adj_sel > 0.0, 0.0, NEG)
        for t in range(GROUP // QPAIR):
            cols = slice(t * pw, (t + 1) * pw)
            qat_ref[c * (GROUP // QPAIR) + t, :HEAD_DIM, :] = q_t[:, cols]
            qat_ref[c * (GROUP // QPAIR) + t, HEAD_DIM:, :] = sel_rows[:, cols]
        for j in range(GROUP):
            cols = slice(j * BLOCK, (j + 1) * BLOCK)
            pen_ref[c * GROUP + j] = jnp.broadcast_to(pen[:, cols], (8, BLOCK))
        return carry

    lax.fori_loop(0, nb // GROUP, route, 0)

    def colmax(s):
        return jnp.max(s, axis=0, keepdims=True)

    def colsum(p):
        return jnp.sum(p, axis=0, keepdims=True)

    def pv(g, first, p):
        keys = slice(first * BLOCK, first * BLOCK + p.shape[0])
        return dot(vt_ref[g, :, keys], p.astype(BF16))

    def do_class(c, carry):
        npair = GROUP // QPAIR
        g_prev = jnp.maximum(c - 1, 0)

        def far_park(g, t, s_ref):
            s = dot(kaug_ref[g], qat_ref[c * npair + t])
            s_ref[...] = s
            return colmax(s)

        lanes_of = [slice((j % QPAIR) * BLOCK, (j % QPAIR + 1) * BLOCK) for j in range(GROUP)]
        near = []
        for j in range(GROUP):
            qb = c * GROUP + j
            pair = c * npair + j // QPAIR
            q_t = qat_ref[pair, :HEAD_DIM, lanes_of[j]]
            parts = [(dot(kaug_ref[c, j * BLOCK:(j + 1) * BLOCK, :HEAD_DIM], q_t) + bown_ref[...], c, j)]
            g_adj, t_adj = (c, j - 1) if j > 0 else (g_prev, GROUP - 1)
            k_adj = kaug_ref[g_adj, t_adj * BLOCK:(t_adj + 1) * BLOCK, :HEAD_DIM]
            parts.append((dot(k_adj, q_t) + (badj_ref[...] + pen_ref[qb, 0:1, :]), g_adj, t_adj))
            if j >= 2:
                parts.append((dot(kaug_ref[c, :(j - 1) * BLOCK, :], qat_ref[pair, :, lanes_of[j]]), c, 0))
            near.append(parts)

        cmax_first = far_park(0, 0, sa_ref)
        p1_ref[...] = jnp.zeros_like(p1_ref)
        alpha1_ref[...] = jnp.ones_like(alpha1_ref)

        for j, parts in enumerate(near):
            m = functools.reduce(jnp.maximum, [colmax(s) for s, _, _ in parts])
            probs = [(jnp.exp2(s - m), g, first) for s, g, first in parts]
            m_ref[j // QPAIR, :, lanes_of[j]] = m
            l_ref[j // QPAIR, :, lanes_of[j]] = functools.reduce(jnp.add, [colsum(p) for p, _, _ in probs])
            acc_ref[j // QPAIR, :, lanes_of[j]] = functools.reduce(
                jnp.add, [pv(g, first, p) for p, g, first in probs])

        def rescale(t, cmax):
            m_old = m_ref[t]
            m_new = jnp.maximum(m_old, cmax)
            m_ref[t] = m_new
            return m_new, jnp.exp2(m_old - m_new)

        def drain_pair1(g):
            acc_ref[1] = alpha1_ref[0:1, :] * acc_ref[1] + dot(vt_ref[g], p1_ref[...])

        def park_block(g, t, s_ref, k):
            rows = slice(k * BLOCK, (k + 1) * BLOCK)
            s = dot(kaug_ref[g, rows, :], qat_ref[c * npair + t])
            s_ref[rows, :] = s
            return colmax(s)

        def far(g, cmax_a):
            g_old = jnp.maximum(g - 1, 0)
            m_new, alpha = rescale(0, cmax_a)
            l_new = alpha * l_ref[0]
            acc_new = alpha * acc_ref[0]
            cmax_b = None
            for k in range(GROUP):
                rows = slice(k * BLOCK, (k + 1) * BLOCK)
                cm = park_block(g, 1, sb_ref, k)
                cmax_b = cm if cmax_b is None else jnp.maximum(cmax_b, cm)
                p = jnp.exp2(sa_ref[rows, :] - m_new)
                l_new = l_new + colsum(p)
                acc_new = acc_new + pv(g, k, p)
            l_ref[0] = l_new
            acc_ref[0] = acc_new
            m_new, alpha = rescale(1, cmax_b)
            l_new = alpha * l_ref[1]
            acc_new = alpha1_ref[0:1, :] * acc_ref[1]
            cmax_nxt = None
            lag = 2
            for k in range(GROUP + lag):
                if k < GROUP:
                    rows = slice(k * BLOCK, (k + 1) * BLOCK)
                    acc_new = acc_new + dot(vt_ref[g_old, :, rows], p1_ref[rows, :])
                    cm = park_block(g + 1, 0, sa_ref, k)
                    cmax_nxt = cm if cmax_nxt is None else jnp.maximum(cmax_nxt, cm)
                if k >= lag:
                    rows = slice((k - lag) * BLOCK, (k - lag + 1) * BLOCK)
                    p = jnp.exp2(sb_ref[rows, :] - m_new)
                    l_new = l_new + colsum(p)
                    p1_ref[rows, :] = p.astype(BF16)
            acc_ref[1] = acc_new
            l_ref[1] = l_new
            alpha1_ref[...] = jnp.broadcast_to(alpha, alpha1_ref.shape)
            return cmax_nxt

        lax.fori_loop(0, c, far, cmax_first)
        drain_pair1(g_prev)

        for j in range(GROUP):
            rows = pl.ds(pl.multiple_of((c * GROUP + j) * BLOCK, BLOCK), BLOCK)
            out_t = acc_ref[j // QPAIR, :, lanes_of[j]] * (1.0 / l_ref[j // QPAIR, :, lanes_of[j]])
            o_ref[rows, :] = (out_t.T * sz_ref[rows, :]).astype(o_ref.dtype)
        return carry

    lax.fori_loop(0, nb // GROUP, do_class, 0)


def _moba_attention(q, ks, v, kmean, sz, rel_bias):
    s, attn_w = q.shape
    nh = attn_w // HEAD_DIM
    nb = s // BLOCK
    assert nb % GROUP == 0 and nb <= HEAD_DIM
    idx_own, idx_adj = _bucket_tables()
    head_cols = pl.BlockSpec((s, HEAD_DIM), lambda h: (0, h))
    const_tile = pl.BlockSpec((BLOCK, BLOCK), lambda h: (0, 0))
    return pl.pallas_call(
        functools.partial(_attn_kernel, nb=nb),
        grid=(nh,),
        in_specs=[pl.BlockSpec(memory_space=pltpu.SMEM),
                  const_tile, const_tile,
                  head_cols, head_cols, head_cols,
                  pl.BlockSpec((nb, HEAD_DIM), lambda h: (0, h)),
                  head_cols],
        out_specs=head_cols,
        out_shape=jax.ShapeDtypeStruct((s, attn_w), BF16),
        scratch_shapes=[pltpu.VMEM((nb // GROUP, GROUP * BLOCK, 2 * HEAD_DIM), BF16),
                        pltpu.VMEM((nb // GROUP, HEAD_DIM, GROUP * BLOCK), BF16),
                        pltpu.VMEM((BLOCK, BLOCK), F32),
                        pltpu.VMEM((BLOCK, BLOCK), F32),
                        pltpu.VMEM((nb // QPAIR, 2 * HEAD_DIM, QPAIR * BLOCK), BF16),
                        pltpu.VMEM((nb, 8, BLOCK), F32),
                        pltpu.VMEM((GROUP // QPAIR, 1, QPAIR * BLOCK), F32),
                        pltpu.VMEM((GROUP // QPAIR, 1, QPAIR * BLOCK), F32),
                        pltpu.VMEM((GROUP // QPAIR, HEAD_DIM, QPAIR * BLOCK), F32),
                        pltpu.VMEM((GROUP * BLOCK, QPAIR * BLOCK), F32),
                        pltpu.VMEM((GROUP * BLOCK, QPAIR * BLOCK), F32),
                        pltpu.VMEM((GROUP * BLOCK, QPAIR * BLOCK), BF16),
                        pltpu.VMEM((8, QPAIR * BLOCK), F32)],
        compiler_params=pltpu.CompilerParams(
            dimension_semantics=("arbitrary",),
            vmem_limit_bytes=V7X_VMEM_LIMIT_BYTES),
        name="moba_attn",
    )(rel_bias, jnp.asarray(idx_own), jnp.asarray(idx_adj), q, ks, v, kmean, sz)


def _out_kernel(a_ref, c_ref, wa_ref, wc_ref, x_ref, g_ref, o_ref, *, nj, tn):
    j = pl.program_id(1)
    acc = (jnp.dot(a_ref[...], wa_ref[...], preferred_element_type=F32)
           + jnp.dot(c_ref[...], wc_ref[...], preferred_element_type=F32))
    o_ref[:, pl.ds(pl.multiple_of(j * tn, tn), tn)] = x_ref[...] + acc

    @pl.when(j == nj - 1)
    def _():
        ss = jnp.zeros((o_ref.shape[0], 1), F32)
        for jj in range(nj):
            hj = o_ref[:, jj * tn:(jj + 1) * tn]
            ss = ss + jnp.sum(hj * hj, axis=-1, keepdims=True)
        inv = lax.rsqrt(ss / (nj * tn) + EPS)
        for jj in range(nj):
            cols = slice(jj * tn, (jj + 1) * tn)
            o_ref[:, cols] = o_ref[:, cols] * inv * g_ref[:, cols]


def _out_proj(attn, conv, w_out, x, g, tm=512, tn=1024):
    s, d = x.shape
    ka = attn.shape[1]
    kc = conv.shape[1]
    assert ka == kc
    nj = d // tn
    return pl.pallas_call(
        functools.partial(_out_kernel, nj=nj, tn=tn),
        grid=(s // tm, nj),
        in_specs=[pl.BlockSpec((tm, ka), lambda i, j: (i, 0)),
                  pl.BlockSpec((tm, kc), lambda i, j: (i, 0)),
                  pl.BlockSpec((ka, tn), lambda i, j: (0, j)),
                  pl.BlockSpec((kc, tn), lambda i, j: (1, j)),
                  pl.BlockSpec((tm, tn), lambda i, j: (i, j)),
                  pl.BlockSpec((1, d), lambda i, j: (0, 0))],
        out_specs=pl.BlockSpec((tm, d), lambda i, j: (i, 0)),
        out_shape=jax.ShapeDtypeStruct((s, d), F32),
        compiler_params=pltpu.CompilerParams(
            dimension_semantics=("arbitrary", "arbitrary"),
            vmem_limit_bytes=V7X_VMEM_LIMIT_BYTES),
        name="out_proj_norm",
    )(attn, conv, w_out, w_out, x, g.reshape(1, d))


def kernel(x, norm_gain, w_in, conv_w, w_out, rel_bias, final_gain):
    b, s, d = x.shape
    depth = norm_gain.shape[0]
    assert b == 1 and depth == 1, "kernel is written for one sequence and one layer"
    attn_w = d // 2
    conv_width = d - attn_w
    assert rel_bias.shape == (NUM_BUCKETS, attn_w // HEAD_DIM)
    assert w_in.shape[2] == 4 * attn_w + 4 * conv_width

    x2 = x[0]
    w_in_b = w_in[0].astype(BF16)
    w_out_b = w_out[0].astype(BF16)

    u = _rmsnorm_bf16(x2, norm_gain[0])
    q, ks, v, sz, kmean = _proj_attn(u, w_in_b, attn_w)
    conv = _proj_conv(u, w_in_b, conv_w[0], 4 * attn_w, conv_width)
    attn = _moba_attention(q, ks, v, kmean.reshape(s // BLOCK, attn_w), sz, rel_bias)
    out = _out_proj(attn, conv, w_out_b, x2, final_gain)
    return out[None]
```

```python
import functools
import math

import numpy as np
import jax
import jax.numpy as jnp
from jax import lax
from jax.experimental import pallas as pl
from jax.experimental.pallas import tpu as pltpu

HEAD_DIM = 128
BLOCK = 256
TOPK = 3
NUM_BUCKETS = 32
MAX_DISTANCE = 128
EPS = 1e-6
NEG = -1e30
LOG2E = math.log2(math.e)

F32 = jnp.float32
BF16 = jnp.bfloat16

V7X_VMEM_LIMIT_BYTES = 56 * 1024 * 1024


def _silu(z):
    return z * (1.0 / (1.0 + jnp.exp(-z)))


def _rmsnorm_kernel(x_ref, g_ref, o_ref):
    x = x_ref[...]
    ms = jnp.mean(x * x, axis=-1, keepdims=True)
    o_ref[...] = (x * lax.rsqrt(ms + EPS) * g_ref[...]).astype(o_ref.dtype)


def _rmsnorm_bf16(x, g, tm=256):
    s, d = x.shape
    return pl.pallas_call(
        _rmsnorm_kernel,
        grid=(s // tm,),
        in_specs=[pl.BlockSpec((tm, d), lambda i: (i, 0)),
                  pl.BlockSpec((1, d), lambda i: (0, 0))],
        out_specs=pl.BlockSpec((tm, d), lambda i: (i, 0)),
        out_shape=jax.ShapeDtypeStruct((s, d), BF16),
        compiler_params=pltpu.CompilerParams(dimension_semantics=("arbitrary",)),
        name="rmsnorm_in",
    )(x, g.reshape(1, d))


def _proj_attn_kernel(u_ref, wq_ref, wk_ref, wv_ref, wz_ref, side_ref,
                      q_ref, ks_ref, v_ref, sz_ref, km_ref, side_out_ref, *, scale):
    side_out_ref[...] = side_ref[...].astype(side_out_ref.dtype)
    u = u_ref[...]
    tm = u.shape[0]
    q = jnp.dot(u, wq_ref[...], preferred_element_type=F32)
    q_ref[...] = q.astype(BF16)
    k = jnp.dot(u, wk_ref[...], preferred_element_type=F32)
    ks_ref[...] = (k * scale).astype(BF16)
    km_ref[0] = jnp.mean(k.reshape(tm // BLOCK, BLOCK, k.shape[1]), axis=1)
    v = jnp.dot(u, wv_ref[...], preferred_element_type=F32)
    v_ref[...] = v.astype(BF16)
    z = jnp.dot(u, wz_ref[...], preferred_element_type=F32)
    sz_ref[...] = _silu(z)


def _side_cast_specs(side, col_block, cols, n_steps, step_of):
    rows = side.shape[0]
    chunk = rows // n_steps
    assert chunk * n_steps == rows and chunk % 16 == 0 and side.shape[1] % cols == 0
    return (pl.BlockSpec((chunk, cols), lambda j, i: (step_of(j, i), col_block)),
            pl.BlockSpec((chunk, cols), lambda j, i: (step_of(j, i), 0)),
            jax.ShapeDtypeStruct((rows, cols), BF16))


def _proj_attn(u, w, attn_w, side, side_col_block, side_cols, tm=1024, tn=256):
    s, d = u.shape
    nj = attn_w // tn
    ni = s // tm
    side_in, side_out, side_shape = _side_cast_specs(
        side, side_col_block, side_cols, nj * ni, lambda j, i: j * ni + i)
    w_spec = lambda part: pl.BlockSpec((d, tn), lambda j, i, part=part: (0, part * nj + j))
    tile = pl.BlockSpec((tm, tn), lambda j, i: (i, j))
    return pl.pallas_call(
        functools.partial(_proj_attn_kernel, scale=HEAD_DIM ** -0.5 * LOG2E),
        grid=(nj, s // tm),
        in_specs=[pl.BlockSpec((tm, d), lambda j, i: (i, 0)),
                  w_spec(0), w_spec(1), w_spec(2), w_spec(3), side_in],
        out_specs=[tile, tile, tile, tile,
                   pl.BlockSpec((1, tm // BLOCK, tn), lambda j, i: (i, 0, j)), side_out],
        out_shape=[jax.ShapeDtypeStruct((s, attn_w), BF16),
                   jax.ShapeDtypeStruct((s, attn_w), BF16),
                   jax.ShapeDtypeStruct((s, attn_w), BF16),
                   jax.ShapeDtypeStruct((s, attn_w), F32),
                   jax.ShapeDtypeStruct((s // tm, tm // BLOCK, attn_w), F32), side_shape],
        compiler_params=pltpu.CompilerParams(
            dimension_semantics=("arbitrary", "arbitrary"),
            vmem_limit_bytes=V7X_VMEM_LIMIT_BYTES),
        name="proj_attn",
    )(u, w, w, w, w, side)


def _proj_conv_kernel(u_ref, wh_ref, wb_ref, wc_ref, wz_ref, cw_ref, side_ref,
                      o_ref, side_out_ref, carry_ref):
    side_out_ref[...] = side_ref[...].astype(side_out_ref.dtype)
    @pl.when(pl.program_id(1) == 0)
    def _():
        carry_ref[...] = jnp.zeros_like(carry_ref)

    u = u_ref[...]
    tm = u.shape[0]
    hc = jnp.dot(u, wh_ref[...], preferred_element_type=F32)
    c = jnp.dot(u, wc_ref[...], preferred_element_type=F32)
    p = c * hc
    rows = lax.broadcasted_iota(jnp.int32, p.shape, 0)
    prev = carry_ref[...]
    prev1 = prev[7:8, :]
    prev2 = prev[6:7, :]
    p1 = jnp.where(rows == 0, prev1, pltpu.roll(p, 1, 0))
    p2 = jnp.where(rows == 0, prev2, jnp.where(rows == 1, prev1, pltpu.roll(p, 2, 0)))
    carry_ref[...] = p[tm - 8:, :]
    cw = cw_ref[...]
    y = p2 * cw[0:1, :] + p1 * cw[1:2, :] + p * cw[2:3, :]
    b = jnp.dot(u, wb_ref[...], preferred_element_type=F32)
    z = jnp.dot(u, wz_ref[...], preferred_element_type=F32)
    o_ref[...] = (b * y * _silu(z)).astype(o_ref.dtype)


def _proj_conv(u, w, conv_w, col0, conv_width, side, tm=1024, tn=256):
    s, d = u.shape
    nj = conv_width // tn
    ni = s // tm
    base = col0 // tn
    side_in, side_out, side_shape = _side_cast_specs(
        side, 0, side.shape[1], nj * ni, lambda j, i: j * ni + i)
    w_spec = lambda part: pl.BlockSpec((d, tn), lambda j, i, part=part: (0, base + part * nj + j))
    return pl.pallas_call(
        _proj_conv_kernel,
        grid=(nj, s // tm),
        in_specs=[pl.BlockSpec((tm, d), lambda j, i: (i, 0)),
                  w_spec(0), w_spec(1), w_spec(2), w_spec(3),
                  pl.BlockSpec((conv_w.shape[0], tn), lambda j, i: (0, j)), side_in],
        out_specs=[pl.BlockSpec((tm, tn), lambda j, i: (i, j)), side_out],
        out_shape=[jax.ShapeDtypeStruct((s, conv_width), BF16), side_shape],
        scratch_shapes=[pltpu.VMEM((8, tn), F32)],
        compiler_params=pltpu.CompilerParams(
            dimension_semantics=("arbitrary", "arbitrary"),
            vmem_limit_bytes=V7X_VMEM_LIMIT_BYTES),
        name="proj_conv",
    )(u, w, w, w, w, conv_w, side)


def _rel_bucket_np(dist):
    n = np.maximum(dist, 0)
    max_exact = NUM_BUCKETS // 2
    nf = np.maximum(n, 1).astype(np.float32)
    ratio = np.log(nf / np.float32(max_exact)) / np.float32(math.log(MAX_DISTANCE / max_exact))
    large = max_exact + (ratio * np.float32(NUM_BUCKETS - max_exact)).astype(np.int32)
    large = np.minimum(large, NUM_BUCKETS - 1)
    return np.where(n < max_exact, n, large).astype(np.int32)


def _bucket_tables():
    key = np.arange(BLOCK)[:, None]
    qry = np.arange(BLOCK)[None, :]
    d_own = qry - key
    own = np.where(d_own >= 0, _rel_bucket_np(d_own), NUM_BUCKETS).astype(np.int32)
    adj = _rel_bucket_np(BLOCK + qry - key)
    return own, adj


GROUP = 4
QPAIR = 2


def _attn_kernel(relb_ref, idx_own_ref, idx_adj_ref, q_ref, ks_ref, v_ref, km_ref, sz_ref,
                 o_ref,
                 kaug_ref, vt_ref, bown_ref, badj_ref, qat_ref, pen_ref, m_ref, l_ref, acc_ref,
                 sa_ref, sb_ref, *, nb):
    h = pl.program_id(0)
    gk = GROUP * BLOCK
    pw = QPAIR * BLOCK
    dot = functools.partial(jnp.dot, preferred_element_type=F32)

    @pl.when(h == 0)
    def _():
        lane = lax.broadcasted_iota(jnp.int32, (BLOCK, HEAD_DIM), 1)
        for blk in range(nb):
            r0 = (blk % GROUP) * BLOCK
            kaug_ref[blk // GROUP, r0:r0 + BLOCK, HEAD_DIM:] = jnp.where(lane == blk, 1.0, 0.0).astype(BF16)

    far_bias = relb_ref[NUM_BUCKETS - 1, h]
    io = idx_own_ref[...]
    ia = idx_adj_ref[...]
    own = jnp.full((BLOCK, BLOCK), NEG, F32)
    adj = jnp.zeros((BLOCK, BLOCK), F32)
    for b in range(NUM_BUCKETS):
        t = (relb_ref[b, h] - far_bias) * LOG2E
        own = jnp.where(io == b, t, own)
        adj = jnp.where(ia == b, t, adj)
    bown_ref[...] = own
    badj_ref[...] = adj

    def stage(g, carry):
        rows = pl.ds(pl.multiple_of(g * gk, gk), gk)
        kaug_ref[g, :, :HEAD_DIM] = ks_ref[rows, :]
        vt_ref[g] = v_ref[rows, :].astype(F32).T.astype(BF16)
        return carry

    lax.fori_loop(0, nb // GROUP, stage, 0)

    km = km_ref[...]
    km_hi = km.astype(BF16)
    km_lo = (km - km_hi.astype(F32)).astype(BF16)

    def route(c, carry):
        rows = pl.ds(pl.multiple_of(c * gk, gk), gk)
        q_t = q_ref[rows, :].astype(F32).T.astype(BF16)
        gate = dot(km_hi, q_t) + dot(km_lo, q_t)
        blk_id = lax.broadcasted_iota(jnp.int32, gate.shape, 0)
        qb = c * GROUP + lax.shift_right_logical(
            lax.broadcasted_iota(jnp.int32, gate.shape, 1), int(math.log2(BLOCK)))
        gate = jnp.where(blk_id < qb, gate, -jnp.inf)
        sel = jnp.zeros(gate.shape, jnp.bool_)
        for r in range(min(TOPK, nb)):
            mx = jnp.max(gate, axis=0, keepdims=True)
            first = jnp.min(jnp.where(gate == mx, blk_id, nb), axis=0, keepdims=True)
            pick = (blk_id == first) & (qb > r)
            sel = sel | pick
            gate = jnp.where(pick, -jnp.inf, gate)
        sel_far = jnp.where(sel & (blk_id < qb - 1), 0.0, NEG)
        sel_rows = jnp.concatenate(
            [sel_far, jnp.zeros((HEAD_DIM - nb, gk), F32)], axis=0).astype(BF16)
        adj_sel = jnp.max(jnp.where(sel & (blk_id == qb - 1), 1.0, 0.0), axis=0, keepdims=True)
        pen = jnp.where(adj_sel > 0.0, 0.0, NEG)
        for t in range(GROUP // QPAIR):
            cols = slice(t * pw, (t + 1) * pw)
            qat_ref[c * (GROUP // QPAIR) + t, :HEAD_DIM, :] = q_t[:, cols]
            qat_ref[c * (GROUP // QPAIR) + t, HEAD_DIM:, :] = sel_rows[:, cols]
        for j in range(GROUP):
            cols = slice(j * BLOCK, (j + 1) * BLOCK)
            pen_ref[c * GROUP + j] = jnp.broadcast_to(pen[:, cols], (8, BLOCK))
        return carry

    lax.fori_loop(0, nb // GROUP, route, 0)

    def colmax(s):
        return jnp.max(s, axis=0, keepdims=True)

    def colsum(p):
        return jnp.sum(p, axis=0, keepdims=True)

    def pv(g, first, p):
        keys = slice(first * BLOCK, first * BLOCK + p.shape[0])
        return dot(vt_ref[g, :, keys], p.astype(BF16))

    def do_class(c, carry):
        def far_logits(g, t):
            return dot(kaug_ref[g], qat_ref[c * (GROUP // QPAIR) + t])

        def far_park(g, t, s_ref):
            s = far_logits(g, t)
            s_ref[...] = s
            return colmax(s)

        lanes_of = [slice((j % QPAIR) * BLOCK, (j % QPAIR + 1) * BLOCK) for j in range(GROUP)]
        near = []
        for j in range(GROUP):
            qb = c * GROUP + j
            pair = c * (GROUP // QPAIR) + j // QPAIR
            q_t = qat_ref[pair, :HEAD_DIM, lanes_of[j]]
            parts = [(dot(kaug_ref[c, j * BLOCK:(j + 1) * BLOCK, :HEAD_DIM], q_t) + bown_ref[...], c, j)]
            if qb > 0:
                g_adj, t_adj = (c, j - 1) if j > 0 else (c - 1, GROUP - 1)
                k_adj = kaug_ref[g_adj, t_adj * BLOCK:(t_adj + 1) * BLOCK, :HEAD_DIM]
                parts.append((dot(k_adj, q_t) + (badj_ref[...] + pen_ref[qb, 0:1, :]), g_adj, t_adj))
            if j >= 2:
                parts.append((dot(kaug_ref[c, :(j - 1) * BLOCK, :], qat_ref[pair, :, lanes_of[j]]), c, 0))
            near.append(parts)

        if c > 0:
            cmax_a = far_park(0, 0, sa_ref)

        for j, parts in enumerate(near):
            m = functools.reduce(jnp.maximum, [colmax(s) for s, _, _ in parts])
            probs = [(jnp.exp2(s - m), g, first) for s, g, first in parts]
            m_ref[j // QPAIR, :, lanes_of[j]] = m
            l_ref[j // QPAIR, :, lanes_of[j]] = functools.reduce(jnp.add, [colsum(p) for p, _, _ in probs])
            acc_ref[j // QPAIR, :, lanes_of[j]] = functools.reduce(
                jnp.add, [pv(g, first, p) for p, g, first in probs])

        def far_update(g, t, s_ref, cmax):
            m_old = m_ref[t]
            m_new = jnp.maximum(m_old, cmax)
            alpha = jnp.exp2(m_old - m_new)
            l_new = alpha * l_ref[t]
            acc_new = alpha * acc_ref[t]
            for k in range(GROUP):
                p = jnp.exp2(s_ref[k * BLOCK:(k + 1) * BLOCK, :] - m_new)
                l_new = l_new + colsum(p)
                acc_new = acc_new + pv(g, k, p)
            l_ref[t] = l_new
            acc_ref[t] = acc_new
            m_ref[t] = m_new

        for g in range(c):
            cmax_b = far_park(g, 1, sb_ref)
            far_update(g, 0, sa_ref, cmax_a)
            if g + 1 < c:
                cmax_a = far_park(g + 1, 0, sa_ref)
            far_update(g, 1, sb_ref, cmax_b)

        for j in range(GROUP):
            rows = pl.ds((c * GROUP + j) * BLOCK, BLOCK)
            out_t = acc_ref[j // QPAIR, :, lanes_of[j]] * (1.0 / l_ref[j // QPAIR, :, lanes_of[j]])
            o_ref[rows, :] = (out_t.T * sz_ref[rows, :]).astype(o_ref.dtype)
        return carry

    one = jnp.minimum(h + 1, 1)
    for c in range(nb // GROUP):
        lax.fori_loop(0, one, lambda _, carry, c=c: do_class(c, carry), 0)


def _moba_attention(q, ks, v, kmean, sz, rel_bias):
    s, attn_w = q.shape
    nh = attn_w // HEAD_DIM
    nb = s // BLOCK
    assert nb % GROUP == 0 and nb <= HEAD_DIM
    idx_own, idx_adj = _bucket_tables()
    head_cols = pl.BlockSpec((s, HEAD_DIM), lambda h: (0, h))
    const_tile = pl.BlockSpec((BLOCK, BLOCK), lambda h: (0, 0))
    return pl.pallas_call(
        functools.partial(_attn_kernel, nb=nb),
        grid=(nh,),
        in_specs=[pl.BlockSpec(memory_space=pltpu.SMEM),
                  const_tile, const_tile,
                  head_cols, head_cols, head_cols,
                  pl.BlockSpec((nb, HEAD_DIM), lambda h: (0, h)),
                  head_cols],
        out_specs=head_cols,
        out_shape=jax.ShapeDtypeStruct((s, attn_w), BF16),
        scratch_shapes=[pltpu.VMEM((nb // GROUP, GROUP * BLOCK, 2 * HEAD_DIM), BF16),
                        pltpu.VMEM((nb // GROUP, HEAD_DIM, GROUP * BLOCK), BF16),
                        pltpu.VMEM((BLOCK, BLOCK), F32),
                        pltpu.VMEM((BLOCK, BLOCK), F32),
                        pltpu.VMEM((nb // QPAIR, 2 * HEAD_DIM, QPAIR * BLOCK), BF16),
                        pltpu.VMEM((nb, 8, BLOCK), F32),
                        pltpu.VMEM((GROUP // QPAIR, 1, QPAIR * BLOCK), F32),
                        pltpu.VMEM((GROUP // QPAIR, 1, QPAIR * BLOCK), F32),
                        pltpu.VMEM((GROUP // QPAIR, HEAD_DIM, QPAIR * BLOCK), F32),
                        pltpu.VMEM((GROUP * BLOCK, QPAIR * BLOCK), F32),
                        pltpu.VMEM((GROUP * BLOCK, QPAIR * BLOCK), F32)],
        compiler_params=pltpu.CompilerParams(
            dimension_semantics=("arbitrary",),
            vmem_limit_bytes=V7X_VMEM_LIMIT_BYTES),
        name="moba_attn",
    )(rel_bias, jnp.asarray(idx_own), jnp.asarray(idx_adj), q, ks, v, kmean, sz)


def _out_kernel(a_ref, c_ref, wa_ref, wc_ref, x_ref, g_ref, o_ref, *, nj, tn):
    j = pl.program_id(1)
    acc = (jnp.dot(a_ref[...], wa_ref[...], preferred_element_type=F32)
           + jnp.dot(c_ref[...], wc_ref[...], preferred_element_type=F32))
    o_ref[:, pl.ds(pl.multiple_of(j * tn, tn), tn)] = x_ref[...] + acc

    @pl.when(j == nj - 1)
    def _():
        ss = jnp.zeros((o_ref.shape[0], 1), F32)
        for jj in range(nj):
            hj = o_ref[:, jj * tn:(jj + 1) * tn]
            ss = ss + jnp.sum(hj * hj, axis=-1, keepdims=True)
        inv = lax.rsqrt(ss / (nj * tn) + EPS)
        for jj in range(nj):
            cols = slice(jj * tn, (jj + 1) * tn)
            o_ref[:, cols] = o_ref[:, cols] * inv * g_ref[:, cols]


def _out_proj(attn, conv, w_out, x, g, tm=512, tn=1024):
    s, d = x.shape
    ka = attn.shape[1]
    kc = conv.shape[1]
    assert ka == kc
    nj = d // tn
    return pl.pallas_call(
        functools.partial(_out_kernel, nj=nj, tn=tn),
        grid=(s // tm, nj),
        in_specs=[pl.BlockSpec((tm, ka), lambda i, j: (i, 0)),
                  pl.BlockSpec((tm, kc), lambda i, j: (i, 0)),
                  pl.BlockSpec((ka, tn), lambda i, j: (0, j)),
                  pl.BlockSpec((kc, tn), lambda i, j: (1, j)),
                  pl.BlockSpec((tm, tn), lambda i, j: (i, j)),
                  pl.BlockSpec((1, d), lambda i, j: (0, 0))],
        out_specs=pl.BlockSpec((tm, d), lambda i, j: (i, 0)),
        out_shape=jax.ShapeDtypeStruct((s, d), F32),
        compiler_params=pltpu.CompilerParams(
            dimension_semantics=("arbitrary", "arbitrary"),
            vmem_limit_bytes=V7X_VMEM_LIMIT_BYTES),
        name="out_proj_norm",
    )(attn, conv, w_out, w_out, x, g.reshape(1, d))


def kernel(x, norm_gain, w_in, conv_w, w_out, rel_bias, final_gain):
    b, s, d = x.shape
    depth = norm_gain.shape[0]
    assert b == 1 and depth == 1, "kernel is written for one sequence and one layer"
    attn_w = d // 2
    conv_width = d - attn_w
    assert rel_bias.shape == (NUM_BUCKETS, attn_w // HEAD_DIM)
    assert w_in.shape[2] == 4 * attn_w + 4 * conv_width

    x2 = x[0]
    w_attn_b = w_in[0, :, :4 * attn_w].astype(BF16)

    u = _rmsnorm_bf16(x2, norm_gain[0])
    assert attn_w == conv_width
    q, ks, v, sz, kmean, w_conv_b = _proj_attn(u, w_attn_b, attn_w, w_in[0], 1, 4 * conv_width)
    conv, w_out_b = _proj_conv(u, w_conv_b, conv_w[0], 0, conv_width, w_out[0])
    attn = _moba_attention(q, ks, v, kmean.reshape(s // BLOCK, attn_w), sz, rel_bias)
    out = _out_proj(attn, conv, w_out_b, x2, final_gain)
    return out[None]
```

```python
import functools
import math

import numpy as np
import jax
import jax.numpy as jnp
from jax import lax
from jax.experimental import pallas as pl
from jax.experimental.pallas import tpu as pltpu

HEAD_DIM = 128
BLOCK = 256
TOPK = 3
NUM_BUCKETS = 32
MAX_DISTANCE = 128
EPS = 1e-6
NEG = -1e30
LOG2E = math.log2(math.e)

F32 = jnp.float32
BF16 = jnp.bfloat16

V7X_VMEM_LIMIT_BYTES = 56 * 1024 * 1024


def _silu(z):
    return z * (1.0 / (1.0 + jnp.exp(-z)))


def _rmsnorm_kernel(x_ref, g_ref, o_ref):
    x = x_ref[...]
    ms = jnp.mean(x * x, axis=-1, keepdims=True)
    o_ref[...] = (x * lax.rsqrt(ms + EPS) * g_ref[...]).astype(o_ref.dtype)


def _rmsnorm_bf16(x, g, tm=512):
    s, d = x.shape
    return pl.pallas_call(
        _rmsnorm_kernel,
        grid=(s // tm,),
        in_specs=[pl.BlockSpec((tm, d), lambda i: (i, 0)),
                  pl.BlockSpec((1, d), lambda i: (0, 0))],
        out_specs=pl.BlockSpec((tm, d), lambda i: (i, 0)),
        out_shape=jax.ShapeDtypeStruct((s, d), BF16),
        compiler_params=pltpu.CompilerParams(dimension_semantics=("arbitrary",)),
        name="rmsnorm_in",
    )(x, g.reshape(1, d))


def _proj_attn_kernel(u_ref, wq_ref, wk_ref, wv_ref, wz_ref, side_ref,
                      q_ref, ks_ref, v_ref, sz_ref, km_ref, side_out_ref, *, scale):
    side_out_ref[...] = side_ref[...].astype(side_out_ref.dtype)
    u = u_ref[...]
    tm = u.shape[0]
    q = jnp.dot(u, wq_ref[...], preferred_element_type=F32)
    q_ref[...] = q.astype(BF16)
    k = jnp.dot(u, wk_ref[...], preferred_element_type=F32)
    ks_ref[...] = (k * scale).astype(BF16)
    km_ref[0] = jnp.mean(k.reshape(tm // BLOCK, BLOCK, k.shape[1]), axis=1)
    v = jnp.dot(u, wv_ref[...], preferred_element_type=F32)
    v_ref[...] = v.astype(BF16)
    z = jnp.dot(u, wz_ref[...], preferred_element_type=F32)
    sz_ref[...] = _silu(z)


def _side_cast_specs(side, col_block, cols, n_steps, step_of):
    rows = side.shape[0]
    chunk = rows // n_steps
    assert chunk * n_steps == rows and chunk % 16 == 0 and side.shape[1] % cols == 0
    return (pl.BlockSpec((chunk, cols), lambda j, i: (step_of(j, i), col_block)),
            pl.BlockSpec((chunk, cols), lambda j, i: (step_of(j, i), 0)),
            jax.ShapeDtypeStruct((rows, cols), BF16))


def _proj_attn(u, w, attn_w, side, side_col_block, side_cols, tm=1024, tn=256):
    s, d = u.shape
    nj = attn_w // tn
    ni = s // tm
    side_in, side_out, side_shape = _side_cast_specs(
        side, side_col_block, side_cols, nj * ni, lambda j, i: j * ni + i)
    w_spec = lambda part: pl.BlockSpec((d, tn), lambda j, i, part=part: (0, part * nj + j))
    tile = pl.BlockSpec((tm, tn), lambda j, i: (i, j))
    return pl.pallas_call(
        functools.partial(_proj_attn_kernel, scale=HEAD_DIM ** -0.5 * LOG2E),
        grid=(nj, s // tm),
        in_specs=[pl.BlockSpec((tm, d), lambda j, i: (i, 0)),
                  w_spec(0), w_spec(1), w_spec(2), w_spec(3), side_in],
        out_specs=[tile, tile, tile, tile,
                   pl.BlockSpec((1, tm // BLOCK, tn), lambda j, i: (i, 0, j)), side_out],
        out_shape=[jax.ShapeDtypeStruct((s, attn_w), BF16),
                   jax.ShapeDtypeStruct((s, attn_w), BF16),
                   jax.ShapeDtypeStruct((s, attn_w), BF16),
                   jax.ShapeDtypeStruct((s, attn_w), F32),
                   jax.ShapeDtypeStruct((s // tm, tm // BLOCK, attn_w), F32), side_shape],
        compiler_params=pltpu.CompilerParams(
            dimension_semantics=("arbitrary", "arbitrary"),
            vmem_limit_bytes=V7X_VMEM_LIMIT_BYTES),
        name="proj_attn",
    )(u, w, w, w, w, side)


def _proj_conv_kernel(u_ref, wh_ref, wb_ref, wc_ref, wz_ref, cw_ref, side_ref,
                      o_ref, side_out_ref, carry_ref):
    side_out_ref[...] = side_ref[...].astype(side_out_ref.dtype)
    @pl.when(pl.program_id(1) == 0)
    def _():
        carry_ref[...] = jnp.zeros_like(carry_ref)

    u = u_ref[...]
    tm = u.shape[0]
    hc = jnp.dot(u, wh_ref[...], preferred_element_type=F32)
    c = jnp.dot(u, wc_ref[...], preferred_element_type=F32)
    p = c * hc
    rows = lax.broadcasted_iota(jnp.int32, p.shape, 0)
    prev = carry_ref[...]
    prev1 = prev[7:8, :]
    prev2 = prev[6:7, :]
    p1 = jnp.where(rows == 0, prev1, pltpu.roll(p, 1, 0))
    p2 = jnp.where(rows == 0, prev2, jnp.where(rows == 1, prev1, pltpu.roll(p, 2, 0)))
    carry_ref[...] = p[tm - 8:, :]
    cw = cw_ref[...]
    y = p2 * cw[0:1, :] + p1 * cw[1:2, :] + p * cw[2:3, :]
    b = jnp.dot(u, wb_ref[...], preferred_element_type=F32)
    z = jnp.dot(u, wz_ref[...], preferred_element_type=F32)
    o_ref[...] = (b * y * _silu(z)).astype(o_ref.dtype)


def _proj_conv(u, w, conv_w, col0, conv_width, side, tm=1024, tn=256):
    s, d = u.shape
    nj = conv_width // tn
    ni = s // tm
    base = col0 // tn
    side_in, side_out, side_shape = _side_cast_specs(
        side, 0, side.shape[1], nj * ni, lambda j, i: j * ni + i)
    w_spec = lambda part: pl.BlockSpec((d, tn), lambda j, i, part=part: (0, base + part * nj + j))
    return pl.pallas_call(
        _proj_conv_kernel,
        grid=(nj, s // tm),
        in_specs=[pl.BlockSpec((tm, d), lambda j, i: (i, 0)),
                  w_spec(0), w_spec(1), w_spec(2), w_spec(3),
                  pl.BlockSpec((conv_w.shape[0], tn), lambda j, i: (0, j)), side_in],
        out_specs=[pl.BlockSpec((tm, tn), lambda j, i: (i, j)), side_out],
        out_shape=[jax.ShapeDtypeStruct((s, conv_width), BF16), side_shape],
        scratch_shapes=[pltpu.VMEM((8, tn), F32)],
        compiler_params=pltpu.CompilerParams(
            dimension_semantics=("arbitrary", "arbitrary"),
            vmem_limit_bytes=V7X_VMEM_LIMIT_BYTES),
        name="proj_conv",
    )(u, w, w, w, w, conv_w, side)


def _rel_bucket_np(dist):
    n = np.maximum(dist, 0)
    max_exact = NUM_BUCKETS // 2
    nf = np.maximum(n, 1).astype(np.float32)
    ratio = np.log(nf / np.float32(max_exact)) / np.float32(math.log(MAX_DISTANCE / max_exact))
    large = max_exact + (ratio * np.float32(NUM_BUCKETS - max_exact)).astype(np.int32)
    large = np.minimum(large, NUM_BUCKETS - 1)
    return np.where(n < max_exact, n, large).astype(np.int32)


BIAS_ROW = 4 * BLOCK


def _bucket_row():
    dist = np.arange(BIAS_ROW)
    row = np.where(dist < 2 * BLOCK, _rel_bucket_np(dist), NUM_BUCKETS).astype(np.int32)
    return np.broadcast_to(row, (8, BIAS_ROW))


GROUP = 4
QPAIR = 2


def _attn_kernel(relb_ref, bucket_ref, q_ref, ks_ref, v_ref, km_ref, sz_ref,
                 o_ref,
                 kaug_ref, vt_ref, bown_ref, badj_ref, qat_ref, pen_ref, m_ref, l_ref, acc_ref,
                 sa_ref, sb_ref, *, nb):
    h = pl.program_id(0)
    gk = GROUP * BLOCK
    pw = QPAIR * BLOCK
    dot = functools.partial(jnp.dot, preferred_element_type=F32)

    @pl.when(h == 0)
    def _():
        lane = lax.broadcasted_iota(jnp.int32, (BLOCK, HEAD_DIM), 1)
        for blk in range(nb):
            r0 = (blk % GROUP) * BLOCK
            kaug_ref[blk // GROUP, r0:r0 + BLOCK, HEAD_DIM:] = jnp.where(lane == blk, 1.0, 0.0).astype(BF16)

    far_bias = relb_ref[NUM_BUCKETS - 1, h]
    buckets = bucket_ref[...]
    by_dist = jnp.full(buckets.shape, NEG, F32)
    for b in range(NUM_BUCKETS):
        by_dist = jnp.where(buckets == b, (relb_ref[b, h] - far_bias) * LOG2E, by_dist)
    toeplitz = pltpu.roll(jnp.broadcast_to(by_dist[0:1, :], (BLOCK, BIAS_ROW)), 0, 1,
                          stride=1, stride_axis=0)
    bown_ref[...] = toeplitz[:, :BLOCK]
    badj_ref[...] = toeplitz[:, BLOCK:2 * BLOCK]

    def stage(g, carry):
        rows = pl.ds(pl.multiple_of(g * gk, gk), gk)
        kaug_ref[g, :, :HEAD_DIM] = ks_ref[rows, :]
        vt_ref[g] = v_ref[rows, :].astype(F32).T.astype(BF16)
        return carry

    lax.fori_loop(0, nb // GROUP, stage, 0)

    km = km_ref[...]
    km_hi = km.astype(BF16)
    km_lo = (km - km_hi.astype(F32)).astype(BF16)

    def route(c, carry):
        rows = pl.ds(pl.multiple_of(c * gk, gk), gk)
        q_t = q_ref[rows, :].astype(F32).T.astype(BF16)
        gate = dot(km_hi, q_t) + dot(km_lo, q_t)
        blk_id = lax.broadcasted_iota(jnp.int32, gate.shape, 0)
        qb = c * GROUP + lax.shift_right_logical(
            lax.broadcasted_iota(jnp.int32, gate.shape, 1), int(math.log2(BLOCK)))
        gate = jnp.where(blk_id < qb, gate, -jnp.inf)
        sel = jnp.zeros(gate.shape, jnp.bool_)
        for r in range(min(TOPK, nb)):
            mx = jnp.max(gate, axis=0, keepdims=True)
            first = jnp.min(jnp.where(gate == mx, blk_id, nb), axis=0, keepdims=True)
            pick = (blk_id == first) & (qb > r)
            sel = sel | pick
            gate = jnp.where(pick, -jnp.inf, gate)
        sel_far = jnp.where(sel & (blk_id < qb - 1), 0.0, NEG)
        sel_rows = jnp.concatenate(
            [sel_far, jnp.zeros((HEAD_DIM - nb, gk), F32)], axis=0).astype(BF16)
        adj_sel = jnp.max(jnp.where(sel & (blk_id == qb - 1), 1.0, 0.0), axis=0, keepdims=True)
        pen = jnp.where(adj_sel > 0.0, 0.0, NEG)
        for t in range(GROUP // QPAIR):
            cols = slice(t * pw, (t + 1) * pw)
            qat_ref[c * (GROUP // QPAIR) + t, :HEAD_DIM, :] = q_t[:, cols]
            qat_ref[c * (GROUP // QPAIR) + t, HEAD_DIM:, :] = sel_rows[:, cols]
        for j in range(GROUP):
            cols = slice(j * BLOCK, (j + 1) * BLOCK)
            pen_ref[c * GROUP + j] = jnp.broadcast_to(pen[:, cols], (8, BLOCK))
        return carry

    lax.fori_loop(0, nb // GROUP, route, 0)

    def colmax(s):
        return jnp.max(s, axis=0, keepdims=True)

    def colsum(p):
        return jnp.sum(p, axis=0, keepdims=True)

    def pv(g, first, p):
        keys = slice(first * BLOCK, first * BLOCK + p.shape[0])
        return dot(vt_ref[g, :, keys], p.astype(BF16))

    def do_class(c, carry):
        def far_logits(g, t):
            return dot(kaug_ref[g], qat_ref[c * (GROUP // QPAIR) + t])

        def far_park(g, t, s_ref):
            s = far_logits(g, t)
            s_ref[...] = s
            return colmax(s)

        lanes_of = [slice((j % QPAIR) * BLOCK, (j % QPAIR + 1) * BLOCK) for j in range(GROUP)]
        near = []
        for j in range(GROUP):
            qb = c * GROUP + j
            pair = c * (GROUP // QPAIR) + j // QPAIR
            q_t = qat_ref[pair, :HEAD_DIM, lanes_of[j]]
            parts = [(dot(kaug_ref[c, j * BLOCK:(j + 1) * BLOCK, :HEAD_DIM], q_t) + bown_ref[...], c, j)]
            if qb > 0:
                g_adj, t_adj = (c, j - 1) if j > 0 else (c - 1, GROUP - 1)
                k_adj = kaug_ref[g_adj, t_adj * BLOCK:(t_adj + 1) * BLOCK, :HEAD_DIM]
                parts.append((dot(k_adj, q_t) + (badj_ref[...] + pen_ref[qb, 0:1, :]), g_adj, t_adj))
            if j >= 2:
                parts.append((dot(kaug_ref[c, :(j - 1) * BLOCK, :], qat_ref[pair, :, lanes_of[j]]), c, 0))
            near.append(parts)

        if c > 0:
            cmax_a = far_park(0, 0, sa_ref)

        for j, parts in enumerate(near):
            m = functools.reduce(jnp.maximum, [colmax(s) for s, _, _ in parts])
            probs = [(jnp.exp2(s - m), g, first) for s, g, first in parts]
            m_ref[j // QPAIR, :, lanes_of[j]] = m
            l_ref[j // QPAIR, :, lanes_of[j]] = functools.reduce(jnp.add, [colsum(p) for p, _, _ in probs])
            acc_ref[j // QPAIR, :, lanes_of[j]] = functools.reduce(
                jnp.add, [pv(g, first, p) for p, g, first in probs])

        def far_update(g, t, s_ref, cmax):
            m_old = m_ref[t]
            m_new = jnp.maximum(m_old, cmax)
            alpha = jnp.exp2(m_old - m_new)
            l_new = alpha * l_ref[t]
            acc_new = alpha * acc_ref[t]
            for k in range(GROUP):
                p = jnp.exp2(s_ref[k * BLOCK:(k + 1) * BLOCK, :] - m_new)
                l_new = l_new + colsum(p)
                acc_new = acc_new + pv(g, k, p)
            l_ref[t] = l_new
            acc_ref[t] = acc_new
            m_ref[t] = m_new

        for g in range(c):
            cmax_b = far_park(g, 1, sb_ref)
            far_update(g, 0, sa_ref, cmax_a)
            if g + 1 < c:
                cmax_a = far_park(g + 1, 0, sa_ref)
            far_update(g, 1, sb_ref, cmax_b)

        for j in range(GROUP):
            rows = pl.ds((c * GROUP + j) * BLOCK, BLOCK)
            out_t = acc_ref[j // QPAIR, :, lanes_of[j]] * (1.0 / l_ref[j // QPAIR, :, lanes_of[j]])
            o_ref[rows, :] = (out_t.T * sz_ref[rows, :]).astype(o_ref.dtype)
        return carry

    one = jnp.minimum(h + 1, 1)
    for c in range(nb // GROUP):
        lax.fori_loop(0, one, lambda _, carry, c=c: do_class(c, carry), 0)


def _moba_attention(q, ks, v, kmean, sz, rel_bias):
    s, attn_w = q.shape
    nh = attn_w // HEAD_DIM
    nb = s // BLOCK
    assert nb % GROUP == 0 and nb <= HEAD_DIM
    head_cols = pl.BlockSpec((s, HEAD_DIM), lambda h: (0, h))
    return pl.pallas_call(
        functools.partial(_attn_kernel, nb=nb),
        grid=(nh,),
        in_specs=[pl.BlockSpec(memory_space=pltpu.SMEM),
                  pl.BlockSpec((8, BIAS_ROW), lambda h: (0, 0)),
                  head_cols, head_cols, head_cols,
                  pl.BlockSpec((nb, HEAD_DIM), lambda h: (0, h)),
                  head_cols],
        out_specs=head_cols,
        out_shape=jax.ShapeDtypeStruct((s, attn_w), BF16),
        scratch_shapes=[pltpu.VMEM((nb // GROUP, GROUP * BLOCK, 2 * HEAD_DIM), BF16),
                        pltpu.VMEM((nb // GROUP, HEAD_DIM, GROUP * BLOCK), BF16),
                        pltpu.VMEM((BLOCK, BLOCK), F32),
                        pltpu.VMEM((BLOCK, BLOCK), F32),
                        pltpu.VMEM((nb // QPAIR, 2 * HEAD_DIM, QPAIR * BLOCK), BF16),
                        pltpu.VMEM((nb, 8, BLOCK), F32),
                        pltpu.VMEM((GROUP // QPAIR, 1, QPAIR * BLOCK), F32),
                        pltpu.VMEM((GROUP // QPAIR, 1, QPAIR * BLOCK), F32),
                        pltpu.VMEM((GROUP // QPAIR, HEAD_DIM, QPAIR * BLOCK), F32),
                        pltpu.VMEM((GROUP * BLOCK, QPAIR * BLOCK), F32),
                        pltpu.VMEM((GROUP * BLOCK, QPAIR * BLOCK), F32)],
        compiler_params=pltpu.CompilerParams(
            dimension_semantics=("arbitrary",),
            vmem_limit_bytes=V7X_VMEM_LIMIT_BYTES),
        name="moba_attn",
    )(rel_bias, jnp.asarray(_bucket_row()), q, ks, v, kmean, sz)


def _out_kernel(a_ref, c_ref, wa_ref, wc_ref, x_ref, g_ref, o_ref, ss_ref, *, nj, tn):
    j = pl.program_id(1)
    acc = (jnp.dot(a_ref[...], wa_ref[...], preferred_element_type=F32)
           + jnp.dot(c_ref[...], wc_ref[...], preferred_element_type=F32))
    hj = x_ref[...] + acc
    o_ref[:, pl.ds(pl.multiple_of(j * tn, tn), tn)] = hj
    ss_new = jnp.sum(hj * hj, axis=-1, keepdims=True)

    @pl.when(j == 0)
    def _():
        ss_ref[...] = ss_new

    @pl.when(j > 0)
    def _():
        ss_ref[...] = ss_ref[...] + ss_new

    @pl.when(j == nj - 1)
    def _():
        inv = lax.rsqrt(ss_ref[...] / (nj * tn) + EPS)
        for jj in range(nj):
            cols = slice(jj * tn, (jj + 1) * tn)
            o_ref[:, cols] = o_ref[:, cols] * inv * g_ref[:, cols]


def _out_proj(attn, conv, w_out, x, g, tm=512, tn=1024):
    s, d = x.shape
    ka = attn.shape[1]
    kc = conv.shape[1]
    assert ka == kc
    nj = d // tn
    return pl.pallas_call(
        functools.partial(_out_kernel, nj=nj, tn=tn),
        grid=(s // tm, nj),
        in_specs=[pl.BlockSpec((tm, ka), lambda i, j: (i, 0)),
                  pl.BlockSpec((tm, kc), lambda i, j: (i, 0)),
                  pl.BlockSpec((ka, tn), lambda i, j: (0, j)),
                  pl.BlockSpec((kc, tn), lambda i, j: (1, j)),
                  pl.BlockSpec((tm, tn), lambda i, j: (i, j)),
                  pl.BlockSpec((1, d), lambda i, j: (0, 0))],
        out_specs=pl.BlockSpec((tm, d), lambda i, j: (i, 0)),
        out_shape=jax.ShapeDtypeStruct((s, d), F32),
        scratch_shapes=[pltpu.VMEM((tm, 1), F32)],
        compiler_params=pltpu.CompilerParams(
            dimension_semantics=("arbitrary", "arbitrary"),
            vmem_limit_bytes=V7X_VMEM_LIMIT_BYTES),
        name="out_proj_norm",
    )(attn, conv, w_out, w_out, x, g.reshape(1, d))


def kernel(x, norm_gain, w_in, conv_w, w_out, rel_bias, final_gain):
    b, s, d = x.shape
    depth = norm_gain.shape[0]
    assert b == 1 and depth == 1, "kernel is written for one sequence and one layer"
    attn_w = d // 2
    conv_width = d - attn_w
    assert rel_bias.shape == (NUM_BUCKETS, attn_w // HEAD_DIM)
    assert w_in.shape[2] == 4 * attn_w + 4 * conv_width

    x2 = x[0]
    w_attn_b = w_in[0, :, :4 * attn_w].astype(BF16)

    u = _rmsnorm_bf16(x2, norm_gain[0])
    assert attn_w == conv_width
    q, ks, v, sz, kmean, w_conv_b = _proj_attn(u, w_attn_b, attn_w, w_in[0], 1, 4 * conv_width)
    conv, w_out_b = _proj_conv(u, w_conv_b, conv_w[0], 0, conv_width, w_out[0])
    attn = _moba_attention(q, ks, v, kmean.reshape(s // BLOCK, attn_w), sz, rel_bias)
    out = _out_proj(attn, conv, w_out_b, x2, final_gain)
    return out[None]
```

```python
import functools
import math

import numpy as np
import jax
import jax.numpy as jnp
from jax import lax
from jax.experimental import pallas as pl
from jax.experimental.pallas import tpu as pltpu

HEAD_DIM = 128
BLOCK = 256
TOPK = 3
NUM_BUCKETS = 32
MAX_DISTANCE = 128
EPS = 1e-6
NEG = -1e30
LOG2E = math.log2(math.e)

F32 = jnp.float32
BF16 = jnp.bfloat16

V7X_VMEM_LIMIT_BYTES = 56 * 1024 * 1024


def _silu(z):
    return z * (1.0 / (1.0 + jnp.exp(-z)))


def _rmsnorm_kernel(x_ref, g_ref, o_ref):
    x = x_ref[...]
    ms = jnp.mean(x * x, axis=-1, keepdims=True)
    o_ref[...] = (x * lax.rsqrt(ms + EPS) * g_ref[...]).astype(o_ref.dtype)


def _rmsnorm_bf16(x, g, tm=512):
    s, d = x.shape
    return pl.pallas_call(
        _rmsnorm_kernel,
        grid=(s // tm,),
        in_specs=[pl.BlockSpec((tm, d), lambda i: (i, 0)),
                  pl.BlockSpec((1, d), lambda i: (0, 0))],
        out_specs=pl.BlockSpec((tm, d), lambda i: (i, 0)),
        out_shape=jax.ShapeDtypeStruct((s, d), BF16),
        compiler_params=pltpu.CompilerParams(dimension_semantics=("arbitrary",)),
        name="rmsnorm_in",
    )(x, g.reshape(1, d))


def _proj_attn_kernel(u_ref, wq_ref, wk_ref, wv_ref, wz_ref, side_ref,
                      q_ref, ks_ref, v_ref, sz_ref, km_ref, side_out_ref, *, scale):
    side_out_ref[...] = side_ref[...].astype(side_out_ref.dtype)
    u = u_ref[...]
    tm = u.shape[0]
    q = jnp.dot(u, wq_ref[...], preferred_element_type=F32)
    q_ref[...] = q.astype(BF16)
    k = jnp.dot(u, wk_ref[...], preferred_element_type=F32)
    ks_ref[...] = (k * scale).astype(BF16)
    km_ref[0] = jnp.mean(k.reshape(tm // BLOCK, BLOCK, k.shape[1]), axis=1)
    v = jnp.dot(u, wv_ref[...], preferred_element_type=F32)
    v_ref[...] = v.astype(BF16)
    z = jnp.dot(u, wz_ref[...], preferred_element_type=F32)
    sz_ref[...] = _silu(z)


def _side_cast_specs(side, col_block, cols, n_steps, step_of):
    rows = side.shape[0]
    chunk = rows // n_steps
    assert chunk * n_steps == rows and chunk % 16 == 0 and side.shape[1] % cols == 0
    return (pl.BlockSpec((chunk, cols), lambda j, i: (step_of(j, i), col_block)),
            pl.BlockSpec((chunk, cols), lambda j, i: (step_of(j, i), 0)),
            jax.ShapeDtypeStruct((rows, cols), BF16))


def _proj_attn(u, w, attn_w, side, side_col_block, side_cols, tm=1024, tn=256):
    s, d = u.shape
    nj = attn_w // tn
    ni = s // tm
    side_in, side_out, side_shape = _side_cast_specs(
        side, side_col_block, side_cols, nj * ni, lambda j, i: j * ni + i)
    w_spec = lambda part: pl.BlockSpec((d, tn), lambda j, i, part=part: (0, part * nj + j))
    tile = pl.BlockSpec((tm, tn), lambda j, i: (i, j))
    return pl.pallas_call(
        functools.partial(_proj_attn_kernel, scale=HEAD_DIM ** -0.5 * LOG2E),
        grid=(nj, s // tm),
        in_specs=[pl.BlockSpec((tm, d), lambda j, i: (i, 0)),
                  w_spec(0), w_spec(1), w_spec(2), w_spec(3), side_in],
        out_specs=[tile, tile, tile, tile,
                   pl.BlockSpec((1, tm // BLOCK, tn), lambda j, i: (i, 0, j)), side_out],
        out_shape=[jax.ShapeDtypeStruct((s, attn_w), BF16),
                   jax.ShapeDtypeStruct((s, attn_w), BF16),
                   jax.ShapeDtypeStruct((s, attn_w), BF16),
                   jax.ShapeDtypeStruct((s, attn_w), F32),
                   jax.ShapeDtypeStruct((s // tm, tm // BLOCK, attn_w), F32), side_shape],
        compiler_params=pltpu.CompilerParams(
            dimension_semantics=("arbitrary", "arbitrary"),
            vmem_limit_bytes=V7X_VMEM_LIMIT_BYTES),
        name="proj_attn",
    )(u, w, w, w, w, side)


def _proj_conv_kernel(u_ref, wh_ref, wb_ref, wc_ref, wz_ref, cw_ref, side_ref,
                      o_ref, side_out_ref, carry_ref):
    side_out_ref[...] = side_ref[...].astype(side_out_ref.dtype)
    @pl.when(pl.program_id(1) == 0)
    def _():
        carry_ref[...] = jnp.zeros_like(carry_ref)

    u = u_ref[...]
    tm = u.shape[0]
    hc = jnp.dot(u, wh_ref[...], preferred_element_type=F32)
    c = jnp.dot(u, wc_ref[...], preferred_element_type=F32)
    p = c * hc
    rows = lax.broadcasted_iota(jnp.int32, p.shape, 0)
    prev = carry_ref[...]
    prev1 = prev[7:8, :]
    prev2 = prev[6:7, :]
    p1 = jnp.where(rows == 0, prev1, pltpu.roll(p, 1, 0))
    p2 = jnp.where(rows == 0, prev2, jnp.where(rows == 1, prev1, pltpu.roll(p, 2, 0)))
    carry_ref[...] = p[tm - 8:, :]
    cw = cw_ref[...]
    y = p2 * cw[0:1, :] + p1 * cw[1:2, :] + p * cw[2:3, :]
    b = jnp.dot(u, wb_ref[...], preferred_element_type=F32)
    z = jnp.dot(u, wz_ref[...], preferred_element_type=F32)
    o_ref[...] = (b * y * _silu(z)).astype(o_ref.dtype)


def _proj_conv(u, w, conv_w, col0, conv_width, side, tm=1024, tn=256):
    s, d = u.shape
    nj = conv_width // tn
    ni = s // tm
    base = col0 // tn
    side_in, side_out, side_shape = _side_cast_specs(
        side, 0, side.shape[1], nj * ni, lambda j, i: j * ni + i)
    w_spec = lambda part: pl.BlockSpec((d, tn), lambda j, i, part=part: (0, base + part * nj + j))
    return pl.pallas_call(
        _proj_conv_kernel,
        grid=(nj, s // tm),
        in_specs=[pl.BlockSpec((tm, d), lambda j, i: (i, 0)),
                  w_spec(0), w_spec(1), w_spec(2), w_spec(3),
                  pl.BlockSpec((conv_w.shape[0], tn), lambda j, i: (0, j)), side_in],
        out_specs=[pl.BlockSpec((tm, tn), lambda j, i: (i, j)), side_out],
        out_shape=[jax.ShapeDtypeStruct((s, conv_width), BF16), side_shape],
        scratch_shapes=[pltpu.VMEM((8, tn), F32)],
        compiler_params=pltpu.CompilerParams(
            dimension_semantics=("arbitrary", "arbitrary"),
            vmem_limit_bytes=V7X_VMEM_LIMIT_BYTES),
        name="proj_conv",
    )(u, w, w, w, w, conv_w, side)


def _rel_bucket_np(dist):
    n = np.maximum(dist, 0)
    max_exact = NUM_BUCKETS // 2
    nf = np.maximum(n, 1).astype(np.float32)
    ratio = np.log(nf / np.float32(max_exact)) / np.float32(math.log(MAX_DISTANCE / max_exact))
    large = max_exact + (ratio * np.float32(NUM_BUCKETS - max_exact)).astype(np.int32)
    large = np.minimum(large, NUM_BUCKETS - 1)
    return np.where(n < max_exact, n, large).astype(np.int32)


def _bucket_tables():
    key = np.arange(BLOCK)[:, None]
    qry = np.arange(BLOCK)[None, :]
    d_own = qry - key
    own = np.where(d_own >= 0, _rel_bucket_np(d_own), NUM_BUCKETS).astype(np.int32)
    adj = _rel_bucket_np(BLOCK + qry - key)
    return own, adj


GROUP = 4
QPAIR = 2


def _attn_kernel(relb_ref, idx_own_ref, idx_adj_ref, q_ref, ks_ref, v_ref, km_ref, sz_ref,
                 o_ref,
                 kaug_ref, vt_ref, bown_ref, badj_ref, qat_ref, pen_ref, m_ref, l_ref, acc_ref,
                 sa_ref, sb_ref, *, nb):
    h = pl.program_id(0)
    gk = GROUP * BLOCK
    pw = QPAIR * BLOCK
    dot = functools.partial(jnp.dot, preferred_element_type=F32)

    @pl.when(h == 0)
    def _():
        lane = lax.broadcasted_iota(jnp.int32, (BLOCK, HEAD_DIM), 1)
        for blk in range(nb):
            r0 = (blk % GROUP) * BLOCK
            kaug_ref[blk // GROUP, r0:r0 + BLOCK, HEAD_DIM:] = jnp.where(lane == blk, 1.0, 0.0).astype(BF16)

    far_bias = relb_ref[NUM_BUCKETS - 1, h]
    io = idx_own_ref[...]
    ia = idx_adj_ref[...]
    own = jnp.full((BLOCK, BLOCK), NEG, F32)
    adj = jnp.zeros((BLOCK, BLOCK), F32)
    for b in range(NUM_BUCKETS):
        t = (relb_ref[b, h] - far_bias) * LOG2E
        own = jnp.where(io == b, t, own)
        adj = jnp.where(ia == b, t, adj)
    bown_ref[...] = own
    badj_ref[...] = adj

    def stage(g, carry):
        rows = pl.ds(pl.multiple_of(g * gk, gk), gk)
        kaug_ref[g, :, :HEAD_DIM] = ks_ref[rows, :]
        vt_ref[g] = v_ref[rows, :].astype(F32).T.astype(BF16)
        return carry

    lax.fori_loop(0, nb // GROUP, stage, 0)

    km = km_ref[...]
    km_hi = km.astype(BF16)
    km_lo = (km - km_hi.astype(F32)).astype(BF16)

    def route(c, carry):
        rows = pl.ds(pl.multiple_of(c * gk, gk), gk)
        q_t = q_ref[rows, :].astype(F32).T.astype(BF16)
        gate = dot(km_hi, q_t) + dot(km_lo, q_t)
        blk_id = lax.broadcasted_iota(jnp.int32, gate.shape, 0)
        qb = c * GROUP + lax.shift_right_logical(
            lax.broadcasted_iota(jnp.int32, gate.shape, 1), int(math.log2(BLOCK)))
        gate = jnp.where(blk_id < qb, gate, -jnp.inf)
        sel = jnp.zeros(gate.shape, jnp.bool_)
        for r in range(min(TOPK, nb)):
            mx = jnp.max(gate, axis=0, keepdims=True)
            first = jnp.min(jnp.where(gate == mx, blk_id, nb), axis=0, keepdims=True)
            pick = (blk_id == first) & (qb > r)
            sel = sel | pick
            gate = jnp.where(pick, -jnp.inf, gate)
        sel_far = jnp.where(sel & (blk_id < qb - 1), 0.0, NEG)
        sel_rows = jnp.concatenate(
            [sel_far, jnp.zeros((HEAD_DIM - nb, gk), F32)], axis=0).astype(BF16)
        adj_sel = jnp.max(jnp.where(sel & (blk_id == qb - 1), 1.0, 0.0), axis=0, keepdims=True)
        pen = jnp.where(adj_sel > 0.0, 0.0, NEG)
        for t in range(GROUP // QPAIR):
            cols = slice(t * pw, (t + 1) * pw)
            qat_ref[c * (GROUP // QPAIR) + t, :HEAD_DIM, :] = q_t[:, cols]
            qat_ref[c * (GROUP // QPAIR) + t, HEAD_DIM:, :] = sel_rows[:, cols]
        for j in range(GROUP):
            cols = slice(j * BLOCK, (j + 1) * BLOCK)
            pen_ref[c * GROUP + j] = jnp.broadcast_to(pen[:, cols], (8, BLOCK))
        return carry

    lax.fori_loop(0, nb // GROUP, route, 0)

    def colmax(s):
        return jnp.max(s, axis=0, keepdims=True)

    def colsum(p):
        return jnp.sum(p, axis=0, keepdims=True)

    def pv(g, first, p):
        keys = slice(first * BLOCK, first * BLOCK + p.shape[0])
        return dot(vt_ref[g, :, keys], p.astype(BF16))

    def do_class(c, carry):
        def far_logits(g, t):
            return dot(kaug_ref[g], qat_ref[c * (GROUP // QPAIR) + t])

        def far_park(g, t, s_ref):
            s = far_logits(g, t)
            s_ref[...] = s
            return colmax(s)

        lanes_of = [slice((j % QPAIR) * BLOCK, (j % QPAIR + 1) * BLOCK) for j in range(GROUP)]
        near = []
        for j in range(GROUP):
            qb = c * GROUP + j
            pair = c * (GROUP // QPAIR) + j // QPAIR
            q_t = qat_ref[pair, :HEAD_DIM, lanes_of[j]]
            parts = [(dot(kaug_ref[c, j * BLOCK:(j + 1) * BLOCK, :HEAD_DIM], q_t) + bown_ref[...], c, j)]
            if qb > 0:
                g_adj, t_adj = (c, j - 1) if j > 0 else (c - 1, GROUP - 1)
                k_adj = kaug_ref[g_adj, t_adj * BLOCK:(t_adj + 1) * BLOCK, :HEAD_DIM]
                parts.append((dot(k_adj, q_t) + (badj_ref[...] + pen_ref[qb, 0:1, :]), g_adj, t_adj))
            if j >= 2:
                parts.append((dot(kaug_ref[c, :(j - 1) * BLOCK, :], qat_ref[pair, :, lanes_of[j]]), c, 0))
            near.append(parts)

        if c > 0:
            cmax_a = far_park(0, 0, sa_ref)

        for j, parts in enumerate(near):
            m = functools.reduce(jnp.maximum, [colmax(s) for s, _, _ in parts])
            probs = [(jnp.exp2(s - m), g, first) for s, g, first in parts]
            m_ref[j // QPAIR, :, lanes_of[j]] = m
            l_ref[j // QPAIR, :, lanes_of[j]] = functools.reduce(jnp.add, [colsum(p) for p, _, _ in probs])
            acc_ref[j // QPAIR, :, lanes_of[j]] = functools.reduce(
                jnp.add, [pv(g, first, p) for p, g, first in probs])

        def far_update(g, t, s_ref, cmax):
            m_old = m_ref[t]
            m_new = jnp.maximum(m_old, cmax)
            alpha = jnp.exp2(m_old - m_new)
            l_new = alpha * l_ref[t]
            acc_new = alpha * acc_ref[t]
            for k in range(GROUP):
                p = jnp.exp2(s_ref[k * BLOCK:(k + 1) * BLOCK, :] - m_new)
                l_new = l_new + colsum(p)
                acc_new = acc_new + pv(g, k, p)
            l_ref[t] = l_new
            acc_ref[t] = acc_new
            m_ref[t] = m_new

        for g in range(c):
            cmax_b = far_park(g, 1, sb_ref)
            far_update(g, 0, sa_ref, cmax_a)
            if g + 1 < c:
                cmax_a = far_park(g + 1, 0, sa_ref)
            far_update(g, 1, sb_ref, cmax_b)

        for j in range(GROUP):
            rows = pl.ds((c * GROUP + j) * BLOCK, BLOCK)
            out_t = acc_ref[j // QPAIR, :, lanes_of[j]] * (1.0 / l_ref[j // QPAIR, :, lanes_of[j]])
            o_ref[rows, :] = (out_t.T * sz_ref[rows, :]).astype(o_ref.dtype)
        return carry

    one = jnp.minimum(h + 1, 1)
    for c in range(nb // GROUP):
        lax.fori_loop(0, one, lambda _, carry, c=c: do_class(c, carry), 0)


def _moba_attention(q, ks, v, kmean, sz, rel_bias):
    s, attn_w = q.shape
    nh = attn_w // HEAD_DIM
    nb = s // BLOCK
    assert nb % GROUP == 0 and nb <= HEAD_DIM
    idx_own, idx_adj = _bucket_tables()
    head_cols = pl.BlockSpec((s, HEAD_DIM), lambda h: (0, h))
    const_tile = pl.BlockSpec((BLOCK, BLOCK), lambda h: (0, 0))
    return pl.pallas_call(
        functools.partial(_attn_kernel, nb=nb),
        grid=(nh,),
        in_specs=[pl.BlockSpec(memory_space=pltpu.SMEM),
                  const_tile, const_tile,
                  head_cols, head_cols, head_cols,
                  pl.BlockSpec((nb, HEAD_DIM), lambda h: (0, h)),
                  head_cols],
        out_specs=head_cols,
        out_shape=jax.ShapeDtypeStruct((s, attn_w), BF16),
        scratch_shapes=[pltpu.VMEM((nb // GROUP, GROUP * BLOCK, 2 * HEAD_DIM), BF16),
                        pltpu.VMEM((nb // GROUP, HEAD_DIM, GROUP * BLOCK), BF16),
                        pltpu.VMEM((BLOCK, BLOCK), F32),
                        pltpu.VMEM((BLOCK, BLOCK), F32),
                        pltpu.VMEM((nb // QPAIR, 2 * HEAD_DIM, QPAIR * BLOCK), BF16),
                        pltpu.VMEM((nb, 8, BLOCK), F32),
                        pltpu.VMEM((GROUP // QPAIR, 1, QPAIR * BLOCK), F32),
                        pltpu.VMEM((GROUP // QPAIR, 1, QPAIR * BLOCK), F32),
                        pltpu.VMEM((GROUP // QPAIR, HEAD_DIM, QPAIR * BLOCK), F32),
                        pltpu.VMEM((GROUP * BLOCK, QPAIR * BLOCK), F32),
                        pltpu.VMEM((GROUP * BLOCK, QPAIR * BLOCK), F32)],
        compiler_params=pltpu.CompilerParams(
            dimension_semantics=("arbitrary",),
            vmem_limit_bytes=V7X_VMEM_LIMIT_BYTES),
        name="moba_attn",
    )(rel_bias, jnp.asarray(idx_own), jnp.asarray(idx_adj), q, ks, v, kmean, sz)


def _out_kernel(a_ref, c_ref, wa_ref, wc_ref, x_ref, g_ref, o_ref, *, nj, tn):
    j = pl.program_id(1)
    acc = (jnp.dot(a_ref[...], wa_ref[...], preferred_element_type=F32)
           + jnp.dot(c_ref[...], wc_ref[...], preferred_element_type=F32))
    o_ref[:, pl.ds(pl.multiple_of(j * tn, tn), tn)] = x_ref[...] + acc

    @pl.when(j == nj - 1)
    def _():
        ss = jnp.zeros((o_ref.shape[0], 1), F32)
        for jj in range(nj):
            hj = o_ref[:, jj * tn:(jj + 1) * tn]
            ss = ss + jnp.sum(hj * hj, axis=-1, keepdims=True)
        inv = lax.rsqrt(ss / (nj * tn) + EPS)
        for jj in range(nj):
            cols = slice(jj * tn, (jj + 1) * tn)
            o_ref[:, cols] = o_ref[:, cols] * inv * g_ref[:, cols]


def _out_proj(attn, conv, w_out, x, g, tm=512, tn=1024):
    s, d = x.shape
    ka = attn.shape[1]
    kc = conv.shape[1]
    assert ka == kc
    nj = d // tn
    return pl.pallas_call(
        functools.partial(_out_kernel, nj=nj, tn=tn),
        grid=(s // tm, nj),
        in_specs=[pl.BlockSpec((tm, ka), lambda i, j: (i, 0)),
                  pl.BlockSpec((tm, kc), lambda i, j: (i, 0)),
                  pl.BlockSpec((ka, tn), lambda i, j: (0, j)),
                  pl.BlockSpec((kc, tn), lambda i, j: (1, j)),
                  pl.BlockSpec((tm, tn), lambda i, j: (i, j)),
                  pl.BlockSpec((1, d), lambda i, j: (0, 0))],
        out_specs=pl.BlockSpec((tm, d), lambda i, j: (i, 0)),
        out_shape=jax.ShapeDtypeStruct((s, d), F32),
        compiler_params=pltpu.CompilerParams(
            dimension_semantics=("arbitrary", "arbitrary"),
            vmem_limit_bytes=V7X_VMEM_LIMIT_BYTES),
        name="out_proj_norm",
    )(attn, conv, w_out, w_out, x, g.reshape(1, d))


def kernel(x, norm_gain, w_in, conv_w, w_out, rel_bias, final_gain):
    b, s, d = x.shape
    depth = norm_gain.shape[0]
    assert b == 1 and depth == 1, "kernel is written for one sequence and one layer"
    attn_w = d // 2
    conv_width = d - attn_w
    assert rel_bias.shape == (NUM_BUCKETS, attn_w // HEAD_DIM)
    assert w_in.shape[2] == 4 * attn_w + 4 * conv_width

    x2 = x[0]
    w_attn_b = w_in[0, :, :4 * attn_w].astype(BF16)

    u = _rmsnorm_bf16(x2, norm_gain[0])
    assert attn_w == conv_width
    q, ks, v, sz, kmean, w_conv_b = _proj_attn(u, w_attn_b, attn_w, w_in[0], 1, 4 * conv_width)
    conv, w_out_b = _proj_conv(u, w_conv_b, conv_w[0], 0, conv_width, w_out[0])
    attn = _moba_attention(q, ks, v, kmean.reshape(s // BLOCK, attn_w), sz, rel_bias)
    out = _out_proj(attn, conv, w_out_b, x2, final_gain)
    return out[None]
```

```python
import functools
import math

import numpy as np
import jax
import jax.numpy as jnp
from jax import lax
from jax.experimental import pallas as pl
from jax.experimental.pallas import tpu as pltpu

HEAD_DIM = 128
BLOCK = 256
TOPK = 3
NUM_BUCKETS = 32
MAX_DISTANCE = 128
EPS = 1e-6
NEG = -1e30
LOG2E = math.log2(math.e)

F32 = jnp.float32
BF16 = jnp.bfloat16

V7X_VMEM_LIMIT_BYTES = 56 * 1024 * 1024
V7X_OUT_PROJ_VMEM_LIMIT_BYTES = 60 * 1024 * 1024


def _silu(z):
    return z * (1.0 / (1.0 + jnp.exp(-z)))


def _rmsnorm_kernel(x_ref, g_ref, o_ref):
    x = x_ref[...]
    ms = jnp.mean(x * x, axis=-1, keepdims=True)
    o_ref[...] = (x * lax.rsqrt(ms + EPS) * g_ref[...]).astype(o_ref.dtype)


def _rmsnorm_bf16(x, g, tm=256):
    s, d = x.shape
    return pl.pallas_call(
        _rmsnorm_kernel,
        grid=(s // tm,),
        in_specs=[pl.BlockSpec((tm, d), lambda i: (i, 0)),
                  pl.BlockSpec((1, d), lambda i: (0, 0))],
        out_specs=pl.BlockSpec((tm, d), lambda i: (i, 0)),
        out_shape=jax.ShapeDtypeStruct((s, d), BF16),
        compiler_params=pltpu.CompilerParams(dimension_semantics=("arbitrary",)),
        name="rmsnorm_in",
    )(x, g.reshape(1, d))


def _proj_attn_kernel(u_ref, wq_ref, wk_ref, wv_ref, wz_ref, side_ref,
                      q_ref, ks_ref, v_ref, sz_ref, km_ref, side_out_ref, *, scale):
    side_out_ref[...] = side_ref[...].astype(side_out_ref.dtype)
    u = u_ref[...]
    tm = u.shape[0]
    q = jnp.dot(u, wq_ref[...], preferred_element_type=F32)
    q_ref[...] = q.astype(BF16)
    k = jnp.dot(u, wk_ref[...], preferred_element_type=F32)
    ks_ref[...] = (k * scale).astype(BF16)
    km_ref[0] = jnp.mean(k.reshape(tm // BLOCK, BLOCK, k.shape[1]), axis=1)
    v = jnp.dot(u, wv_ref[...], preferred_element_type=F32)
    v_ref[...] = v.astype(BF16)
    z = jnp.dot(u, wz_ref[...], preferred_element_type=F32)
    sz_ref[...] = _silu(z)


def _side_cast_specs(side, col_block, cols, n_steps, step_of):
    rows = side.shape[0]
    chunk = rows // n_steps
    assert chunk * n_steps == rows and chunk % 16 == 0 and side.shape[1] % cols == 0
    return (pl.BlockSpec((chunk, cols), lambda j, i: (step_of(j, i), col_block)),
            pl.BlockSpec((chunk, cols), lambda j, i: (step_of(j, i), 0)),
            jax.ShapeDtypeStruct((rows, cols), BF16))


def _proj_attn(u, w, attn_w, side, side_col_block, side_cols, tm=1024, tn=256):
    s, d = u.shape
    nj = attn_w // tn
    ni = s // tm
    side_in, side_out, side_shape = _side_cast_specs(
        side, side_col_block, side_cols, nj * ni, lambda j, i: j * ni + i)
    w_spec = lambda part: pl.BlockSpec((d, tn), lambda j, i, part=part: (0, part * nj + j))
    tile = pl.BlockSpec((tm, tn), lambda j, i: (i, j))
    return pl.pallas_call(
        functools.partial(_proj_attn_kernel, scale=HEAD_DIM ** -0.5 * LOG2E),
        grid=(nj, s // tm),
        in_specs=[pl.BlockSpec((tm, d), lambda j, i: (i, 0)),
                  w_spec(0), w_spec(1), w_spec(2), w_spec(3), side_in],
        out_specs=[tile, tile, tile, tile,
                   pl.BlockSpec((1, tm // BLOCK, tn), lambda j, i: (i, 0, j)), side_out],
        out_shape=[jax.ShapeDtypeStruct((s, attn_w), BF16),
                   jax.ShapeDtypeStruct((s, attn_w), BF16),
                   jax.ShapeDtypeStruct((s, attn_w), BF16),
                   jax.ShapeDtypeStruct((s, attn_w), F32),
                   jax.ShapeDtypeStruct((s // tm, tm // BLOCK, attn_w), F32), side_shape],
        compiler_params=pltpu.CompilerParams(
            dimension_semantics=("arbitrary", "arbitrary"),
            vmem_limit_bytes=V7X_VMEM_LIMIT_BYTES),
        name="proj_attn",
    )(u, w, w, w, w, side)


def _proj_conv_kernel(u_ref, wh_ref, wb_ref, wc_ref, wz_ref, cw_ref, side_ref,
                      o_ref, side_out_ref, carry_ref):
    side_out_ref[...] = side_ref[...].astype(side_out_ref.dtype)
    @pl.when(pl.program_id(1) == 0)
    def _():
        carry_ref[...] = jnp.zeros_like(carry_ref)

    u = u_ref[...]
    tm = u.shape[0]
    hc = jnp.dot(u, wh_ref[...], preferred_element_type=F32)
    c = jnp.dot(u, wc_ref[...], preferred_element_type=F32)
    p = c * hc
    rows = lax.broadcasted_iota(jnp.int32, p.shape, 0)
    prev = carry_ref[...]
    prev1 = prev[7:8, :]
    prev2 = prev[6:7, :]
    p1 = jnp.where(rows == 0, prev1, pltpu.roll(p, 1, 0))
    p2 = jnp.where(rows == 0, prev2, jnp.where(rows == 1, prev1, pltpu.roll(p, 2, 0)))
    carry_ref[...] = p[tm - 8:, :]
    cw = cw_ref[...]
    y = p2 * cw[0:1, :] + p1 * cw[1:2, :] + p * cw[2:3, :]
    b = jnp.dot(u, wb_ref[...], preferred_element_type=F32)
    z = jnp.dot(u, wz_ref[...], preferred_element_type=F32)
    o_ref[...] = (b * y * _silu(z)).astype(o_ref.dtype)


def _proj_conv(u, w, conv_w, col0, conv_width, side, tm=1024, tn=256):
    s, d = u.shape
    nj = conv_width // tn
    ni = s // tm
    base = col0 // tn
    side_in, side_out, side_shape = _side_cast_specs(
        side, 0, side.shape[1], nj * ni, lambda j, i: j * ni + i)
    w_spec = lambda part: pl.BlockSpec((d, tn), lambda j, i, part=part: (0, base + part * nj + j))
    return pl.pallas_call(
        _proj_conv_kernel,
        grid=(nj, s // tm),
        in_specs=[pl.BlockSpec((tm, d), lambda j, i: (i, 0)),
                  w_spec(0), w_spec(1), w_spec(2), w_spec(3),
                  pl.BlockSpec((conv_w.shape[0], tn), lambda j, i: (0, j)), side_in],
        out_specs=[pl.BlockSpec((tm, tn), lambda j, i: (i, j)), side_out],
        out_shape=[jax.ShapeDtypeStruct((s, conv_width), BF16), side_shape],
        scratch_shapes=[pltpu.VMEM((8, tn), F32)],
        compiler_params=pltpu.CompilerParams(
            dimension_semantics=("arbitrary", "arbitrary"),
            vmem_limit_bytes=V7X_VMEM_LIMIT_BYTES),
        name="proj_conv",
    )(u, w, w, w, w, conv_w, side)


def _rel_bucket_np(dist):
    n = np.maximum(dist, 0)
    max_exact = NUM_BUCKETS // 2
    nf = np.maximum(n, 1).astype(np.float32)
    ratio = np.log(nf / np.float32(max_exact)) / np.float32(math.log(MAX_DISTANCE / max_exact))
    large = max_exact + (ratio * np.float32(NUM_BUCKETS - max_exact)).astype(np.int32)
    large = np.minimum(large, NUM_BUCKETS - 1)
    return np.where(n < max_exact, n, large).astype(np.int32)


def _bucket_tables():
    key = np.arange(BLOCK)[:, None]
    qry = np.arange(BLOCK)[None, :]
    d_own = qry - key
    own = np.where(d_own >= 0, _rel_bucket_np(d_own), NUM_BUCKETS).astype(np.int32)
    adj = _rel_bucket_np(BLOCK + qry - key)
    return own, adj


GROUP = 4
QPAIR = 2


def _attn_kernel(relb_ref, idx_own_ref, idx_adj_ref, q_ref, ks_ref, v_ref, km_ref, sz_ref,
                 o_ref,
                 kaug_ref, vt_ref, bown_ref, badj_ref, qat_ref, pen_ref, m_ref, l_ref, acc_ref,
                 sa_ref, sb_ref, *, nb):
    h = pl.program_id(0)
    gk = GROUP * BLOCK
    pw = QPAIR * BLOCK
    dot = functools.partial(jnp.dot, preferred_element_type=F32)

    @pl.when(h == 0)
    def _():
        lane = lax.broadcasted_iota(jnp.int32, (BLOCK, HEAD_DIM), 1)
        for blk in range(nb):
            r0 = (blk % GROUP) * BLOCK
            kaug_ref[blk // GROUP, r0:r0 + BLOCK, HEAD_DIM:] = jnp.where(lane == blk, 1.0, 0.0).astype(BF16)

    far_bias = relb_ref[NUM_BUCKETS - 1, h]
    io = idx_own_ref[...]
    ia = idx_adj_ref[...]
    own = jnp.full((BLOCK, BLOCK), NEG, F32)
    adj = jnp.zeros((BLOCK, BLOCK), F32)
    for b in range(NUM_BUCKETS):
        t = (relb_ref[b, h] - far_bias) * LOG2E
        own = jnp.where(io == b, t, own)
        adj = jnp.where(ia == b, t, adj)
    bown_ref[...] = own
    badj_ref[...] = adj

    def stage(g, carry):
        rows = pl.ds(pl.multiple_of(g * gk, gk), gk)
        kaug_ref[g, :, :HEAD_DIM] = ks_ref[rows, :]
        vt_ref[g] = v_ref[rows, :].astype(F32).T.astype(BF16)
        return carry

    lax.fori_loop(0, nb // GROUP, stage, 0)

    km = km_ref[...]
    km_hi = km.astype(BF16)
    km_lo = (km - km_hi.astype(F32)).astype(BF16)

    def route(c, carry):
        rows = pl.ds(pl.multiple_of(c * gk, gk), gk)
        q_t = q_ref[rows, :].astype(F32).T.astype(BF16)
        gate = dot(km_hi, q_t) + dot(km_lo, q_t)
        blk_id = lax.broadcasted_iota(jnp.int32, gate.shape, 0)
        qb = c * GROUP + lax.shift_right_logical(
            lax.broadcasted_iota(jnp.int32, gate.shape, 1), int(math.log2(BLOCK)))
        gate = jnp.where(blk_id < qb, gate, -jnp.inf)
        sel = jnp.zeros(gate.shape, jnp.bool_)
        for r in range(min(TOPK, nb)):
            mx = jnp.max(gate, axis=0, keepdims=True)
            first = jnp.min(jnp.where(gate == mx, blk_id, nb), axis=0, keepdims=True)
            pick = (blk_id == first) & (qb > r)
            sel = sel | pick
            gate = jnp.where(pick, -jnp.inf, gate)
        sel_far = jnp.where(sel & (blk_id < qb - 1), 0.0, NEG)
        sel_rows = jnp.concatenate(
            [sel_far, jnp.zeros((HEAD_DIM - nb, gk), F32)], axis=0).astype(BF16)
        adj_sel = jnp.max(jnp.where(sel & (blk_id == qb - 1), 1.0, 0.0), axis=0, keepdims=True)
        pen = jnp.where(adj_sel > 0.0, 0.0, NEG)
        for t in range(GROUP // QPAIR):
            cols = slice(t * pw, (t + 1) * pw)
            qat_ref[c * (GROUP // QPAIR) + t, :HEAD_DIM, :] = q_t[:, cols]
            qat_ref[c * (GROUP // QPAIR) + t, HEAD_DIM:, :] = sel_rows[:, cols]
        for j in range(GROUP):
            cols = slice(j * BLOCK, (j + 1) * BLOCK)
            pen_ref[c * GROUP + j] = jnp.broadcast_to(pen[:, cols], (8, BLOCK))
        return carry

    lax.fori_loop(0, nb // GROUP, route, 0)

    def colmax(s):
        return jnp.max(s, axis=0, keepdims=True)

    def colsum(p):
        return jnp.sum(p, axis=0, keepdims=True)

    def pv(g, first, p):
        keys = slice(first * BLOCK, first * BLOCK + p.shape[0])
        return dot(vt_ref[g, :, keys], p.astype(BF16))

    def do_class(c, carry):
        def far_logits(g, t):
            return dot(kaug_ref[g], qat_ref[c * (GROUP // QPAIR) + t])

        def far_park(g, t, s_ref):
            s = far_logits(g, t)
            s_ref[...] = s
            return colmax(s)

        lanes_of = [slice((j % QPAIR) * BLOCK, (j % QPAIR + 1) * BLOCK) for j in range(GROUP)]
        near = []
        for j in range(GROUP):
            qb = c * GROUP + j
            pair = c * (GROUP // QPAIR) + j // QPAIR
            q_t = qat_ref[pair, :HEAD_DIM, lanes_of[j]]
            parts = [(dot(kaug_ref[c, j * BLOCK:(j + 1) * BLOCK, :HEAD_DIM], q_t) + bown_ref[...], c, j)]
            if qb > 0:
                g_adj, t_adj = (c, j - 1) if j > 0 else (c - 1, GROUP - 1)
                k_adj = kaug_ref[g_adj, t_adj * BLOCK:(t_adj + 1) * BLOCK, :HEAD_DIM]
                parts.append((dot(k_adj, q_t) + (badj_ref[...] + pen_ref[qb, 0:1, :]), g_adj, t_adj))
            if j >= 2:
                parts.append((dot(kaug_ref[c, :(j - 1) * BLOCK, :], qat_ref[pair, :, lanes_of[j]]), c, 0))
            near.append(parts)

        if c > 0:
            cmax_a = far_park(0, 0, sa_ref)

        for j, parts in enumerate(near):
            m = functools.reduce(jnp.maximum, [colmax(s) for s, _, _ in parts])
            probs = [(jnp.exp2(s - m), g, first) for s, g, first in parts]
            m_ref[j // QPAIR, :, lanes_of[j]] = m
            l_ref[j // QPAIR, :, lanes_of[j]] = functools.reduce(jnp.add, [colsum(p) for p, _, _ in probs])
            acc_ref[j // QPAIR, :, lanes_of[j]] = functools.reduce(
                jnp.add, [pv(g, first, p) for p, g, first in probs])

        def far_update(g, t, s_ref, cmax):
            m_old = m_ref[t]
            m_new = jnp.maximum(m_old, cmax)
            alpha = jnp.exp2(m_old - m_new)
            l_new = alpha * l_ref[t]
            acc_new = alpha * acc_ref[t]
            for k in range(GROUP):
                p = jnp.exp2(s_ref[k * BLOCK:(k + 1) * BLOCK, :] - m_new)
                l_new = l_new + colsum(p)
                acc_new = acc_new + pv(g, k, p)
            l_ref[t] = l_new
            acc_ref[t] = acc_new
            m_ref[t] = m_new

        for g in range(c):
            cmax_b = far_park(g, 1, sb_ref)
            far_update(g, 0, sa_ref, cmax_a)
            if g + 1 < c:
                cmax_a = far_park(g + 1, 0, sa_ref)
            far_update(g, 1, sb_ref, cmax_b)

        for j in range(GROUP):
            rows = pl.ds((c * GROUP + j) * BLOCK, BLOCK)
            out_t = acc_ref[j // QPAIR, :, lanes_of[j]] * (1.0 / l_ref[j // QPAIR, :, lanes_of[j]])
            o_ref[rows, :] = (out_t.T * sz_ref[rows, :]).astype(o_ref.dtype)
        return carry

    one = jnp.minimum(h + 1, 1)
    for c in range(nb // GROUP):
        lax.fori_loop(0, one, lambda _, carry, c=c: do_class(c, carry), 0)


def _moba_attention(q, ks, v, kmean, sz, rel_bias):
    s, attn_w = q.shape
    nh = attn_w // HEAD_DIM
    nb = s // BLOCK
    assert nb % GROUP == 0 and nb <= HEAD_DIM
    idx_own, idx_adj = _bucket_tables()
    head_cols = pl.BlockSpec((s, HEAD_DIM), lambda h: (0, h))
    const_tile = pl.BlockSpec((BLOCK, BLOCK), lambda h: (0, 0))
    return pl.pallas_call(
        functools.partial(_attn_kernel, nb=nb),
        grid=(nh,),
        in_specs=[pl.BlockSpec(memory_space=pltpu.SMEM),
                  const_tile, const_tile,
                  head_cols, head_cols, head_cols,
                  pl.BlockSpec((nb, HEAD_DIM), lambda h: (0, h)),
                  head_cols],
        out_specs=head_cols,
        out_shape=jax.ShapeDtypeStruct((s, attn_w), BF16),
        scratch_shapes=[pltpu.VMEM((nb // GROUP, GROUP * BLOCK, 2 * HEAD_DIM), BF16),
                        pltpu.VMEM((nb // GROUP, HEAD_DIM, GROUP * BLOCK), BF16),
                        pltpu.VMEM((BLOCK, BLOCK), F32),
                        pltpu.VMEM((BLOCK, BLOCK), F32),
                        pltpu.VMEM((nb // QPAIR, 2 * HEAD_DIM, QPAIR * BLOCK), BF16),
                        pltpu.VMEM((nb, 8, BLOCK), F32),
                        pltpu.VMEM((GROUP // QPAIR, 1, QPAIR * BLOCK), F32),
                        pltpu.VMEM((GROUP // QPAIR, 1, QPAIR * BLOCK), F32),
                        pltpu.VMEM((GROUP // QPAIR, HEAD_DIM, QPAIR * BLOCK), F32),
                        pltpu.VMEM((GROUP * BLOCK, QPAIR * BLOCK), F32),
                        pltpu.VMEM((GROUP * BLOCK, QPAIR * BLOCK), F32)],
        compiler_params=pltpu.CompilerParams(
            dimension_semantics=("arbitrary",),
            vmem_limit_bytes=V7X_VMEM_LIMIT_BYTES),
        name="moba_attn",
    )(rel_bias, jnp.asarray(idx_own), jnp.asarray(idx_adj), q, ks, v, kmean, sz)


def _out_kernel(a_ref, c_ref, wa_ref, wc_ref, x_ref, g_ref, o_ref):
    h = (x_ref[...]
         + jnp.dot(a_ref[...], wa_ref[...], preferred_element_type=F32)
         + jnp.dot(c_ref[...], wc_ref[...], preferred_element_type=F32))
    ms = jnp.mean(h * h, axis=-1, keepdims=True)
    o_ref[...] = h * lax.rsqrt(ms + EPS) * g_ref[...]


def _out_proj(attn, conv, w_out, x, g, tm=256):
    s, d = x.shape
    ka = attn.shape[1]
    kc = conv.shape[1]
    assert ka == kc
    resident = pl.Buffered(1)
    return pl.pallas_call(
        _out_kernel,
        grid=(s // tm,),
        in_specs=[pl.BlockSpec((tm, ka), lambda i: (i, 0)),
                  pl.BlockSpec((tm, kc), lambda i: (i, 0)),
                  pl.BlockSpec((ka, d), lambda i: (0, 0), pipeline_mode=resident),
                  pl.BlockSpec((kc, d), lambda i: (1, 0), pipeline_mode=resident),
                  pl.BlockSpec((tm, d), lambda i: (i, 0)),
                  pl.BlockSpec((1, d), lambda i: (0, 0))],
        out_specs=pl.BlockSpec((tm, d), lambda i: (i, 0)),
        out_shape=jax.ShapeDtypeStruct((s, d), F32),
        compiler_params=pltpu.CompilerParams(
            dimension_semantics=("arbitrary",),
            vmem_limit_bytes=V7X_OUT_PROJ_VMEM_LIMIT_BYTES),
        name="out_proj_norm",
    )(attn, conv, w_out, w_out, x, g.reshape(1, d))


def kernel(x, norm_gain, w_in, conv_w, w_out, rel_bias, final_gain):
    b, s, d = x.shape
    depth = norm_gain.shape[0]
    assert b == 1 and depth == 1, "kernel is written for one sequence and one layer"
    attn_w = d // 2
    conv_width = d - attn_w
    assert rel_bias.shape == (NUM_BUCKETS, attn_w // HEAD_DIM)
    assert w_in.shape[2] == 4 * attn_w + 4 * conv_width

    x2 = x[0]
    w_attn_b = w_in[0, :, :4 * attn_w].astype(BF16)

    u = _rmsnorm_bf16(x2, norm_gain[0])
    assert attn_w == conv_width
    q, ks, v, sz, kmean, w_conv_b = _proj_attn(u, w_attn_b, attn_w, w_in[0], 1, 4 * conv_width)
    conv, w_out_b = _proj_conv(u, w_conv_b, conv_w[0], 0, conv_width, w_out[0])
    attn = _moba_attention(q, ks, v, kmean.reshape(s // BLOCK, attn_w), sz, rel_bias)
    out = _out_proj(attn, conv, w_out_b, x2, final_gain)
    return out[None]
```

```python
import functools
import math

import numpy as np
import jax
import jax.numpy as jnp
from jax import lax
from jax.experimental import pallas as pl
from jax.experimental.pallas import tpu as pltpu

HEAD_DIM = 128
BLOCK = 256
TOPK = 3
NUM_BUCKETS = 32
MAX_DISTANCE = 128
EPS = 1e-6
NEG = -1e30
LOG2E = math.log2(math.e)

F32 = jnp.float32
BF16 = jnp.bfloat16

V7X_VMEM_LIMIT_BYTES = 56 * 1024 * 1024
V7X_OUT_PROJ_VMEM_LIMIT_BYTES = 60 * 1024 * 1024


def _silu(z):
    return z * (1.0 / (1.0 + jnp.exp(-z)))


def _rmsnorm_kernel(x_ref, g_ref, o_ref):
    x = x_ref[...]
    ms = jnp.mean(x * x, axis=-1, keepdims=True)
    o_ref[...] = (x * lax.rsqrt(ms + EPS) * g_ref[...]).astype(o_ref.dtype)


def _rmsnorm_bf16(x, g, tm=256):
    s, d = x.shape
    return pl.pallas_call(
        _rmsnorm_kernel,
        grid=(s // tm,),
        in_specs=[pl.BlockSpec((tm, d), lambda i: (i, 0)),
                  pl.BlockSpec((1, d), lambda i: (0, 0))],
        out_specs=pl.BlockSpec((tm, d), lambda i: (i, 0)),
        out_shape=jax.ShapeDtypeStruct((s, d), BF16),
        compiler_params=pltpu.CompilerParams(dimension_semantics=("arbitrary",)),
        name="rmsnorm_in",
    )(x, g.reshape(1, d))


def _run_side_jobs(side_refs, side_out_refs):
    for src, dst in zip(side_refs, side_out_refs):
        dst[...] = src[...].astype(dst.dtype)


def _side_cast_specs(side, col_block, cols, n_steps, step_of):
    rows = side.shape[0]
    chunk = rows // n_steps
    assert chunk * n_steps == rows and chunk % 16 == 0 and side.shape[1] % cols == 0
    return (pl.BlockSpec((chunk, cols), lambda j, i: (step_of(j, i), col_block)),
            pl.BlockSpec((chunk, cols), lambda j, i: (step_of(j, i), 0)),
            jax.ShapeDtypeStruct((rows, cols), BF16))


def _proj_qk_kernel(u_ref, wq_ref, wk_ref, *refs, scale, n_side):
    side_refs, (q_ref, ks_ref, km_ref), side_out_refs = refs[:n_side], refs[n_side:n_side + 3], refs[n_side + 3:]
    _run_side_jobs(side_refs, side_out_refs)
    u = u_ref[...]
    tm = u.shape[0]
    q = jnp.dot(u, wq_ref[...], preferred_element_type=F32)
    q_ref[...] = q.astype(BF16)
    k = jnp.dot(u, wk_ref[...], preferred_element_type=F32)
    ks_ref[...] = (k * scale).astype(BF16)
    km_ref[0] = jnp.mean(k.reshape(tm // BLOCK, BLOCK, k.shape[1]), axis=1)


def _proj_vz_kernel(u_ref, wv_ref, wz_ref, *refs, n_side):
    side_refs, (v_ref, sz_ref), side_out_refs = refs[:n_side], refs[n_side:n_side + 2], refs[n_side + 2:]
    _run_side_jobs(side_refs, side_out_refs)
    u = u_ref[...]
    v = jnp.dot(u, wv_ref[...], preferred_element_type=F32)
    v_ref[...] = v.astype(BF16)
    z = jnp.dot(u, wz_ref[...], preferred_element_type=F32)
    sz_ref[...] = _silu(z)


def _proj_pair(kernel_fn, name, u, w, width, out_specs, out_shape, side_jobs, tm, tn):
    s, d = u.shape
    nj = width // tn
    ni = s // tm
    jobs = [_side_cast_specs(m, cb, cols, nj * ni, lambda j, i: j * ni + i) for m, cb, cols in side_jobs]
    w_spec = lambda part: pl.BlockSpec((d, tn), lambda j, i, part=part: (0, part * nj + j))
    return pl.pallas_call(
        functools.partial(kernel_fn, n_side=len(jobs)),
        grid=(nj, ni),
        in_specs=[pl.BlockSpec((tm, d), lambda j, i: (i, 0)), w_spec(0), w_spec(1)] + [jb[0] for jb in jobs],
        out_specs=out_specs + [jb[1] for jb in jobs],
        out_shape=out_shape + [jb[2] for jb in jobs],
        compiler_params=pltpu.CompilerParams(
            dimension_semantics=("arbitrary", "arbitrary"),
            vmem_limit_bytes=V7X_VMEM_LIMIT_BYTES),
        name=name,
    )(u, w, w, *[m for m, _, _ in side_jobs])


def _proj_conv_kernel(u_ref, wh_ref, wb_ref, wc_ref, wz_ref, cw_ref, side_ref,
                      o_ref, side_out_ref, carry_ref):
    _run_side_jobs([side_ref], [side_out_ref])
    @pl.when(pl.program_id(1) == 0)
    def _():
        carry_ref[...] = jnp.zeros_like(carry_ref)

    u = u_ref[...]
    tm = u.shape[0]
    hc = jnp.dot(u, wh_ref[...], preferred_element_type=F32)
    c = jnp.dot(u, wc_ref[...], preferred_element_type=F32)
    p = c * hc
    rows = lax.broadcasted_iota(jnp.int32, p.shape, 0)
    prev = carry_ref[...]
    prev1 = prev[7:8, :]
    prev2 = prev[6:7, :]
    p1 = jnp.where(rows == 0, prev1, pltpu.roll(p, 1, 0))
    p2 = jnp.where(rows == 0, prev2, jnp.where(rows == 1, prev1, pltpu.roll(p, 2, 0)))
    carry_ref[...] = p[tm - 8:, :]
    cw = cw_ref[...]
    y = p2 * cw[0:1, :] + p1 * cw[1:2, :] + p * cw[2:3, :]
    b = jnp.dot(u, wb_ref[...], preferred_element_type=F32)
    z = jnp.dot(u, wz_ref[...], preferred_element_type=F32)
    o_ref[...] = (b * y * _silu(z)).astype(o_ref.dtype)


def _proj_conv(u, w_hb, w_cz, conv_w, conv_width, side, tm=1024, tn=256):
    s, d = u.shape
    nj = conv_width // tn
    ni = s // tm
    side_in, side_out, side_shape = _side_cast_specs(
        side, 0, side.shape[1], nj * ni, lambda j, i: j * ni + i)
    w_spec = lambda part: pl.BlockSpec((d, tn), lambda j, i, part=part: (0, part * nj + j))
    return pl.pallas_call(
        _proj_conv_kernel,
        grid=(nj, ni),
        in_specs=[pl.BlockSpec((tm, d), lambda j, i: (i, 0)),
                  w_spec(0), w_spec(1), w_spec(0), w_spec(1),
                  pl.BlockSpec((conv_w.shape[0], tn), lambda j, i: (0, j)), side_in],
        out_specs=[pl.BlockSpec((tm, tn), lambda j, i: (i, j)), side_out],
        out_shape=[jax.ShapeDtypeStruct((s, conv_width), BF16), side_shape],
        scratch_shapes=[pltpu.VMEM((8, tn), F32)],
        compiler_params=pltpu.CompilerParams(
            dimension_semantics=("arbitrary", "arbitrary"),
            vmem_limit_bytes=V7X_VMEM_LIMIT_BYTES),
        name="proj_conv",
    )(u, w_hb, w_hb, w_cz, w_cz, conv_w, side)


def _rel_bucket_np(dist):
    n = np.maximum(dist, 0)
    max_exact = NUM_BUCKETS // 2
    nf = np.maximum(n, 1).astype(np.float32)
    ratio = np.log(nf / np.float32(max_exact)) / np.float32(math.log(MAX_DISTANCE / max_exact))
    large = max_exact + (ratio * np.float32(NUM_BUCKETS - max_exact)).astype(np.int32)
    large = np.minimum(large, NUM_BUCKETS - 1)
    return np.where(n < max_exact, n, large).astype(np.int32)


def _bucket_tables():
    key = np.arange(BLOCK)[:, None]
    qry = np.arange(BLOCK)[None, :]
    d_own = qry - key
    own = np.where(d_own >= 0, _rel_bucket_np(d_own), NUM_BUCKETS).astype(np.int32)
    adj = _rel_bucket_np(BLOCK + qry - key)
    return own, adj


GROUP = 4
QPAIR = 2


def _attn_kernel(relb_ref, idx_own_ref, idx_adj_ref, q_ref, ks_ref, v_ref, km_ref, sz_ref,
                 o_ref,
                 kaug_ref, vt_ref, bown_ref, badj_ref, qat_ref, pen_ref, m_ref, l_ref, acc_ref,
                 sa_ref, sb_ref, *, nb):
    h = pl.program_id(0)
    gk = GROUP * BLOCK
    pw = QPAIR * BLOCK
    dot = functools.partial(jnp.dot, preferred_element_type=F32)

    @pl.when(h == 0)
    def _():
        lane = lax.broadcasted_iota(jnp.int32, (BLOCK, HEAD_DIM), 1)
        for blk in range(nb):
            r0 = (blk % GROUP) * BLOCK
            kaug_ref[blk // GROUP, r0:r0 + BLOCK, HEAD_DIM:] = jnp.where(lane == blk, 1.0, 0.0).astype(BF16)

    far_bias = relb_ref[NUM_BUCKETS - 1, h]
    io = idx_own_ref[...]
    ia = idx_adj_ref[...]
    own = jnp.full((BLOCK, BLOCK), NEG, F32)
    adj = jnp.zeros((BLOCK, BLOCK), F32)
    for b in range(NUM_BUCKETS):
        t = (relb_ref[b, h] - far_bias) * LOG2E
        own = jnp.where(io == b, t, own)
        adj = jnp.where(ia == b, t, adj)
    bown_ref[...] = own
    badj_ref[...] = adj

    def stage(g, carry):
        rows = pl.ds(pl.multiple_of(g * gk, gk), gk)
        kaug_ref[g, :, :HEAD_DIM] = ks_ref[rows, :]
        vt_ref[g] = v_ref[rows, :].astype(F32).T.astype(BF16)
        return carry

    lax.fori_loop(0, nb // GROUP, stage, 0)

    km = km_ref[...]
    km_hi = km.astype(BF16)
    km_lo = (km - km_hi.astype(F32)).astype(BF16)

    def route(c, carry):
        rows = pl.ds(pl.multiple_of(c * gk, gk), gk)
        q_t = q_ref[rows, :].astype(F32).T.astype(BF16)
        gate = dot(km_hi, q_t) + dot(km_lo, q_t)
        blk_id = lax.broadcasted_iota(jnp.int32, gate.shape, 0)
        qb = c * GROUP + lax.shift_right_logical(
            lax.broadcasted_iota(jnp.int32, gate.shape, 1), int(math.log2(BLOCK)))
        gate = jnp.where(blk_id < qb, gate, -jnp.inf)
        sel = jnp.zeros(gate.shape, jnp.bool_)
        for r in range(min(TOPK, nb)):
            mx = jnp.max(gate, axis=0, keepdims=True)
            first = jnp.min(jnp.where(gate == mx, blk_id, nb), axis=0, keepdims=True)
            pick = (blk_id == first) & (qb > r)
            sel = sel | pick
            gate = jnp.where(pick, -jnp.inf, gate)
        sel_far = jnp.where(sel & (blk_id < qb - 1), 0.0, NEG)
        sel_rows = jnp.concatenate(
            [sel_far, jnp.zeros((HEAD_DIM - nb, gk), F32)], axis=0).astype(BF16)
        adj_sel = jnp.max(jnp.where(sel & (blk_id == qb - 1), 1.0, 0.0), axis=0, keepdims=True)
        pen = jnp.where(adj_sel > 0.0, 0.0, NEG)
        for t in range(GROUP // QPAIR):
            cols = slice(t * pw, (t + 1) * pw)
            qat_ref[c * (GROUP // QPAIR) + t, :HEAD_DIM, :] = q_t[:, cols]
            qat_ref[c * (GROUP // QPAIR) + t, HEAD_DIM:, :] = sel_rows[:, cols]
        for j in range(GROUP):
            cols = slice(j * BLOCK, (j + 1) * BLOCK)
            pen_ref[c * GROUP + j] = jnp.broadcast_to(pen[:, cols], (8, BLOCK))
        return carry

    lax.fori_loop(0, nb // GROUP, route, 0)

    def colmax(s):
        return jnp.max(s, axis=0, keepdims=True)

    def colsum(p):
        return jnp.sum(p, axis=0, keepdims=True)

    def pv(g, first, p):
        keys = slice(first * BLOCK, first * BLOCK + p.shape[0])
        return dot(vt_ref[g, :, keys], p.astype(BF16))

    def do_class(c, carry):
        def far_logits(g, t):
            return dot(kaug_ref[g], qat_ref[c * (GROUP // QPAIR) + t])

        def far_park(g, t, s_ref):
            s = far_logits(g, t)
            s_ref[...] = s
            return colmax(s)

        lanes_of = [slice((j % QPAIR) * BLOCK, (j % QPAIR + 1) * BLOCK) for j in range(GROUP)]
        near = []
        for j in range(GROUP):
            qb = c * GROUP + j
            pair = c * (GROUP // QPAIR) + j // QPAIR
            q_t = qat_ref[pair, :HEAD_DIM, lanes_of[j]]
            parts = [(dot(kaug_ref[c, j * BLOCK:(j + 1) * BLOCK, :HEAD_DIM], q_t) + bown_ref[...], c, j)]
            if qb > 0:
                g_adj, t_adj = (c, j - 1) if j > 0 else (c - 1, GROUP - 1)
                k_adj = kaug_ref[g_adj, t_adj * BLOCK:(t_adj + 1) * BLOCK, :HEAD_DIM]
                parts.append((dot(k_adj, q_t) + (badj_ref[...] + pen_ref[qb, 0:1, :]), g_adj, t_adj))
            if j >= 2:
                parts.append((dot(kaug_ref[c, :(j - 1) * BLOCK, :], qat_ref[pair, :, lanes_of[j]]), c, 0))
            near.append(parts)

        if c > 0:
            cmax_a = far_park(0, 0, sa_ref)

        for j, parts in enumerate(near):
            m = functools.reduce(jnp.maximum, [colmax(s) for s, _, _ in parts])
            probs = [(jnp.exp2(s - m), g, first) for s, g, first in parts]
            m_ref[j // QPAIR, :, lanes_of[j]] = m
            l_ref[j // QPAIR, :, lanes_of[j]] = functools.reduce(jnp.add, [colsum(p) for p, _, _ in probs])
            acc_ref[j // QPAIR, :, lanes_of[j]] = functools.reduce(
                jnp.add, [pv(g, first, p) for p, g, first in probs])

        def far_update(g, t, s_ref, cmax):
            m_old = m_ref[t]
            m_new = jnp.maximum(m_old, cmax)
            alpha = jnp.exp2(m_old - m_new)
            l_new = alpha * l_ref[t]
            acc_new = alpha * acc_ref[t]
            for k in range(GROUP):
                p = jnp.exp2(s_ref[k * BLOCK:(k + 1) * BLOCK, :] - m_new)
                l_new = l_new + colsum(p)
                acc_new = acc_new + pv(g, k, p)
            l_ref[t] = l_new
            acc_ref[t] = acc_new
            m_ref[t] = m_new

        for g in range(c):
            cmax_b = far_park(g, 1, sb_ref)
            far_update(g, 0, sa_ref, cmax_a)
            if g + 1 < c:
                cmax_a = far_park(g + 1, 0, sa_ref)
            far_update(g, 1, sb_ref, cmax_b)

        for j in range(GROUP):
            rows = pl.ds((c * GROUP + j) * BLOCK, BLOCK)
            out_t = acc_ref[j // QPAIR, :, lanes_of[j]] * (1.0 / l_ref[j // QPAIR, :, lanes_of[j]])
            o_ref[rows, :] = (out_t.T * sz_ref[rows, :]).astype(o_ref.dtype)
        return carry

    one = jnp.minimum(h + 1, 1)
    for c in range(nb // GROUP):
        lax.fori_loop(0, one, lambda _, carry, c=c: do_class(c, carry), 0)


def _moba_attention(q, ks, v, kmean, sz, rel_bias):
    s, attn_w = q.shape
    nh = attn_w // HEAD_DIM
    nb = s // BLOCK
    assert nb % GROUP == 0 and nb <= HEAD_DIM
    idx_own, idx_adj = _bucket_tables()
    head_cols = pl.BlockSpec((s, HEAD_DIM), lambda h: (0, h))
    const_tile = pl.BlockSpec((BLOCK, BLOCK), lambda h: (0, 0))
    return pl.pallas_call(
        functools.partial(_attn_kernel, nb=nb),
        grid=(nh,),
        in_specs=[pl.BlockSpec(memory_space=pltpu.SMEM),
                  const_tile, const_tile,
                  head_cols, head_cols, head_cols,
                  pl.BlockSpec((nb, HEAD_DIM), lambda h: (0, h)),
                  head_cols],
        out_specs=head_cols,
        out_shape=jax.ShapeDtypeStruct((s, attn_w), BF16),
        scratch_shapes=[pltpu.VMEM((nb // GROUP, GROUP * BLOCK, 2 * HEAD_DIM), BF16),
                        pltpu.VMEM((nb // GROUP, HEAD_DIM, GROUP * BLOCK), BF16),
                        pltpu.VMEM((BLOCK, BLOCK), F32),
                        pltpu.VMEM((BLOCK, BLOCK), F32),
                        pltpu.VMEM((nb // QPAIR, 2 * HEAD_DIM, QPAIR * BLOCK), BF16),
                        pltpu.VMEM((nb, 8, BLOCK), F32),
                        pltpu.VMEM((GROUP // QPAIR, 1, QPAIR * BLOCK), F32),
                        pltpu.VMEM((GROUP // QPAIR, 1, QPAIR * BLOCK), F32),
                        pltpu.VMEM((GROUP // QPAIR, HEAD_DIM, QPAIR * BLOCK), F32),
                        pltpu.VMEM((GROUP * BLOCK, QPAIR * BLOCK), F32),
                        pltpu.VMEM((GROUP * BLOCK, QPAIR * BLOCK), F32)],
        compiler_params=pltpu.CompilerParams(
            dimension_semantics=("arbitrary",),
            vmem_limit_bytes=V7X_VMEM_LIMIT_BYTES),
        name="moba_attn",
    )(rel_bias, jnp.asarray(idx_own), jnp.asarray(idx_adj), q, ks, v, kmean, sz)


def _out_kernel(a_ref, c_ref, wa_ref, wc_ref, x_ref, g_ref, o_ref):
    h = (x_ref[...]
         + jnp.dot(a_ref[...], wa_ref[...], preferred_element_type=F32)
         + jnp.dot(c_ref[...], wc_ref[...], preferred_element_type=F32))
    ms = jnp.mean(h * h, axis=-1, keepdims=True)
    o_ref[...] = h * lax.rsqrt(ms + EPS) * g_ref[...]


def _out_proj(attn, conv, w_out, x, g, tm=256):
    s, d = x.shape
    ka = attn.shape[1]
    kc = conv.shape[1]
    assert ka == kc
    resident = pl.Buffered(1)
    return pl.pallas_call(
        _out_kernel,
        grid=(s // tm,),
        in_specs=[pl.BlockSpec((tm, ka), lambda i: (i, 0)),
                  pl.BlockSpec((tm, kc), lambda i: (i, 0)),
                  pl.BlockSpec((ka, d), lambda i: (0, 0), pipeline_mode=resident),
                  pl.BlockSpec((kc, d), lambda i: (1, 0), pipeline_mode=resident),
                  pl.BlockSpec((tm, d), lambda i: (i, 0)),
                  pl.BlockSpec((1, d), lambda i: (0, 0))],
        out_specs=pl.BlockSpec((tm, d), lambda i: (i, 0)),
        out_shape=jax.ShapeDtypeStruct((s, d), F32),
        compiler_params=pltpu.CompilerParams(
            dimension_semantics=("arbitrary",),
            vmem_limit_bytes=V7X_OUT_PROJ_VMEM_LIMIT_BYTES),
        name="out_proj_norm",
    )(attn, conv, w_out, w_out, x, g.reshape(1, d))


def kernel(x, norm_gain, w_in, conv_w, w_out, rel_bias, final_gain):
    b, s, d = x.shape
    depth = norm_gain.shape[0]
    assert b == 1 and depth == 1, "kernel is written for one sequence and one layer"
    attn_w = d // 2
    conv_width = d - attn_w
    assert rel_bias.shape == (NUM_BUCKETS, attn_w // HEAD_DIM)
    assert w_in.shape[2] == 4 * attn_w + 4 * conv_width

    x2 = x[0]
    assert attn_w == conv_width
    pair_w = 2 * attn_w
    w_qk_b = w_in[0, :, :pair_w].astype(BF16)

    u = _rmsnorm_bf16(x2, norm_gain[0])
    tm, tn = 1024, 512
    tile = pl.BlockSpec((tm, tn), lambda j, i: (i, j))
    act = jax.ShapeDtypeStruct((s, attn_w), BF16)
    q, ks, kmean, w_vz_b, w_hb_b = _proj_pair(
        functools.partial(_proj_qk_kernel, scale=HEAD_DIM ** -0.5 * LOG2E), "proj_qk", u, w_qk_b, attn_w,
        [tile, tile, pl.BlockSpec((1, tm // BLOCK, tn), lambda j, i: (i, 0, j))],
        [act, act, jax.ShapeDtypeStruct((s // tm, tm // BLOCK, attn_w), F32)],
        [(w_in[0], 1, pair_w), (w_in[0], 2, pair_w)], tm, tn)
    v, sz, w_cz_b = _proj_pair(
        _proj_vz_kernel, "proj_vz", u, w_vz_b, attn_w,
        [tile, tile], [act, jax.ShapeDtypeStruct((s, attn_w), F32)],
        [(w_in[0], 3, pair_w)], tm, tn)
    conv, w_out_b = _proj_conv(u, w_hb_b, w_cz_b, conv_w[0], conv_width, w_out[0])
    attn = _moba_attention(q, ks, v, kmean.reshape(s // BLOCK, attn_w), sz, rel_bias)
    out = _out_proj(attn, conv, w_out_b, x2, final_gain)
    return out[None]
```

```python
import functools
import math

import numpy as np
import jax
import jax.numpy as jnp
from jax import lax
from jax.experimental import pallas as pl
from jax.experimental.pallas import tpu as pltpu

HEAD_DIM = 128
BLOCK = 256
TOPK = 3
NUM_BUCKETS = 32
MAX_DISTANCE = 128
EPS = 1e-6
NEG = -1e30
LOG2E = math.log2(math.e)

F32 = jnp.float32
BF16 = jnp.bfloat16

V7X_VMEM_LIMIT_BYTES = 56 * 1024 * 1024
V7X_OUT_PROJ_VMEM_LIMIT_BYTES = 60 * 1024 * 1024


def _silu(z):
    return z * (1.0 / (1.0 + jnp.exp(-z)))


def _rmsnorm_kernel(x_ref, g_ref, o_ref):
    x = x_ref[...]
    ms = jnp.mean(x * x, axis=-1, keepdims=True)
    o_ref[...] = (x * lax.rsqrt(ms + EPS) * g_ref[...]).astype(o_ref.dtype)


def _rmsnorm_bf16(x, g, tm=512):
    s, d = x.shape
    return pl.pallas_call(
        _rmsnorm_kernel,
        grid=(s // tm,),
        in_specs=[pl.BlockSpec((tm, d), lambda i: (i, 0)),
                  pl.BlockSpec((1, d), lambda i: (0, 0))],
        out_specs=pl.BlockSpec((tm, d), lambda i: (i, 0)),
        out_shape=jax.ShapeDtypeStruct((s, d), BF16),
        compiler_params=pltpu.CompilerParams(dimension_semantics=("arbitrary",)),
        name="rmsnorm_in",
    )(x, g.reshape(1, d))


def _run_side_jobs(side_refs, side_out_refs):
    for src, dst in zip(side_refs, side_out_refs):
        dst[...] = src[...].astype(dst.dtype)


def _side_cast_specs(side, col_block, cols, n_steps, step_of):
    rows = side.shape[0]
    chunk = rows // n_steps
    assert chunk * n_steps == rows and chunk % 16 == 0 and side.shape[1] % cols == 0
    return (pl.BlockSpec((chunk, cols), lambda j, i: (step_of(j, i), col_block)),
            pl.BlockSpec((chunk, cols), lambda j, i: (step_of(j, i), 0)),
            jax.ShapeDtypeStruct((rows, cols), BF16))


def _proj_qk_kernel(u_ref, wq_ref, wk_ref, *refs, scale, n_side):
    side_refs, (q_ref, ks_ref, km_ref), side_out_refs = refs[:n_side], refs[n_side:n_side + 3], refs[n_side + 3:]
    _run_side_jobs(side_refs, side_out_refs)
    u = u_ref[...]
    tm = u.shape[0]
    q = jnp.dot(u, wq_ref[...], preferred_element_type=F32)
    q_ref[...] = q.astype(BF16)
    k = jnp.dot(u, wk_ref[...], preferred_element_type=F32)
    ks_ref[...] = (k * scale).astype(BF16)
    km_ref[0] = jnp.mean(k.reshape(tm // BLOCK, BLOCK, k.shape[1]), axis=1)


def _proj_vz_kernel(u_ref, wv_ref, wz_ref, *refs, n_side):
    side_refs, (v_ref, sz_ref), side_out_refs = refs[:n_side], refs[n_side:n_side + 2], refs[n_side + 2:]
    _run_side_jobs(side_refs, side_out_refs)
    u = u_ref[...]
    v = jnp.dot(u, wv_ref[...], preferred_element_type=F32)
    v_ref[...] = v.astype(BF16)
    z = jnp.dot(u, wz_ref[...], preferred_element_type=F32)
    sz_ref[...] = _silu(z)


def _proj_pair(kernel_fn, name, u, w, width, out_specs, out_shape, side_jobs, tm, tn):
    s, d = u.shape
    nj = width // tn
    ni = s // tm
    jobs = [_side_cast_specs(m, cb, cols, nj * ni, lambda j, i: j * ni + i) for m, cb, cols in side_jobs]
    w_spec = lambda part: pl.BlockSpec((d, tn), lambda j, i, part=part: (0, part * nj + j))
    return pl.pallas_call(
        functools.partial(kernel_fn, n_side=len(jobs)),
        grid=(nj, ni),
        in_specs=[pl.BlockSpec((tm, d), lambda j, i: (i, 0)), w_spec(0), w_spec(1)] + [jb[0] for jb in jobs],
        out_specs=out_specs + [jb[1] for jb in jobs],
        out_shape=out_shape + [jb[2] for jb in jobs],
        compiler_params=pltpu.CompilerParams(
            dimension_semantics=("arbitrary", "arbitrary"),
            vmem_limit_bytes=V7X_VMEM_LIMIT_BYTES),
        name=name,
    )(u, w, w, *[m for m, _, _ in side_jobs])


def _proj_conv_kernel(u_ref, wh_ref, wb_ref, wc_ref, wz_ref, cw_ref, side_ref,
                      o_ref, side_out_ref, carry_ref):
    _run_side_jobs([side_ref], [side_out_ref])
    @pl.when(pl.program_id(1) == 0)
    def _():
        carry_ref[...] = jnp.zeros_like(carry_ref)

    u = u_ref[...]
    tm = u.shape[0]
    hc = jnp.dot(u, wh_ref[...], preferred_element_type=F32)
    c = jnp.dot(u, wc_ref[...], preferred_element_type=F32)
    p = c * hc
    rows = lax.broadcasted_iota(jnp.int32, p.shape, 0)
    prev = carry_ref[...]
    prev1 = prev[7:8, :]
    prev2 = prev[6:7, :]
    p1 = jnp.where(rows == 0, prev1, pltpu.roll(p, 1, 0))
    p2 = jnp.where(rows == 0, prev2, jnp.where(rows == 1, prev1, pltpu.roll(p, 2, 0)))
    carry_ref[...] = p[tm - 8:, :]
    cw = cw_ref[...]
    y = p2 * cw[0:1, :] + p1 * cw[1:2, :] + p * cw[2:3, :]
    b = jnp.dot(u, wb_ref[...], preferred_element_type=F32)
    z = jnp.dot(u, wz_ref[...], preferred_element_type=F32)
    o_ref[...] = (b * y * _silu(z)).astype(o_ref.dtype)


def _proj_conv(u, w_hb, w_cz, conv_w, conv_width, side, tm=1024, tn=256):
    s, d = u.shape
    nj = conv_width // tn
    ni = s // tm
    side_in, side_out, side_shape = _side_cast_specs(
        side, 0, side.shape[1], nj * ni, lambda j, i: j * ni + i)
    w_spec = lambda part: pl.BlockSpec((d, tn), lambda j, i, part=part: (0, part * nj + j))
    return pl.pallas_call(
        _proj_conv_kernel,
        grid=(nj, ni),
        in_specs=[pl.BlockSpec((tm, d), lambda j, i: (i, 0)),
                  w_spec(0), w_spec(1), w_spec(0), w_spec(1),
                  pl.BlockSpec((conv_w.shape[0], tn), lambda j, i: (0, j)), side_in],
        out_specs=[pl.BlockSpec((tm, tn), lambda j, i: (i, j)), side_out],
        out_shape=[jax.ShapeDtypeStruct((s, conv_width), BF16), side_shape],
        scratch_shapes=[pltpu.VMEM((8, tn), F32)],
        compiler_params=pltpu.CompilerParams(
            dimension_semantics=("arbitrary", "arbitrary"),
            vmem_limit_bytes=V7X_VMEM_LIMIT_BYTES),
        name="proj_conv",
    )(u, w_hb, w_hb, w_cz, w_cz, conv_w, side)


def _rel_bucket_np(dist):
    n = np.maximum(dist, 0)
    max_exact = NUM_BUCKETS // 2
    nf = np.maximum(n, 1).astype(np.float32)
    ratio = np.log(nf / np.float32(max_exact)) / np.float32(math.log(MAX_DISTANCE / max_exact))
    large = max_exact + (ratio * np.float32(NUM_BUCKETS - max_exact)).astype(np.int32)
    large = np.minimum(large, NUM_BUCKETS - 1)
    return np.where(n < max_exact, n, large).astype(np.int32)


def _bucket_tables():
    key = np.arange(BLOCK)[:, None]
    qry = np.arange(BLOCK)[None, :]
    d_own = qry - key
    own = np.where(d_own >= 0, _rel_bucket_np(d_own), NUM_BUCKETS).astype(np.int32)
    adj = _rel_bucket_np(BLOCK + qry - key)
    return own, adj


GROUP = 4
QPAIR = 2


def _attn_kernel(relb_ref, idx_own_ref, idx_adj_ref, q_ref, ks_ref, v_ref, km_ref, sz_ref,
                 o_ref,
                 kaug_ref, vt_ref, bown_ref, badj_ref, qat_ref, pen_ref, m_ref, l_ref, acc_ref,
                 sa_ref, sb_ref, *, nb):
    h = pl.program_id(0)
    gk = GROUP * BLOCK
    pw = QPAIR * BLOCK
    dot = functools.partial(jnp.dot, preferred_element_type=F32)

    @pl.when(h == 0)
    def _():
        lane = lax.broadcasted_iota(jnp.int32, (BLOCK, HEAD_DIM), 1)
        for blk in range(nb):
            r0 = (blk % GROUP) * BLOCK
            kaug_ref[blk // GROUP, r0:r0 + BLOCK, HEAD_DIM:] = jnp.where(lane == blk, 1.0, 0.0).astype(BF16)

    far_bias = relb_ref[NUM_BUCKETS - 1, h]
    io = idx_own_ref[...]
    ia = idx_adj_ref[...]
    own = jnp.full((BLOCK, BLOCK), NEG, F32)
    adj = jnp.zeros((BLOCK, BLOCK), F32)
    for b in range(NUM_BUCKETS):
        t = (relb_ref[b, h] - far_bias) * LOG2E
        own = jnp.where(io == b, t, own)
        adj = jnp.where(ia == b, t, adj)
    bown_ref[...] = own
    badj_ref[...] = adj

    def stage(g, carry):
        rows = pl.ds(pl.multiple_of(g * gk, gk), gk)
        kaug_ref[g, :, :HEAD_DIM] = ks_ref[rows, :]
        vt_ref[g] = v_ref[rows, :].astype(F32).T.astype(BF16)
        return carry

    lax.fori_loop(0, nb // GROUP, stage, 0)

    km = km_ref[...]
    km_hi = km.astype(BF16)
    km_lo = (km - km_hi.astype(F32)).astype(BF16)

    def route(c, carry):
        rows = pl.ds(pl.multiple_of(c * gk, gk), gk)
        q_t = q_ref[rows, :].astype(F32).T.astype(BF16)
        gate = dot(km_hi, q_t) + dot(km_lo, q_t)
        blk_id = lax.broadcasted_iota(jnp.int32, gate.shape, 0)
        qb = c * GROUP + lax.shift_right_logical(
            lax.broadcasted_iota(jnp.int32, gate.shape, 1), int(math.log2(BLOCK)))
        gate = jnp.where(blk_id < qb, gate, -jnp.inf)
        sel = jnp.zeros(gate.shape, jnp.bool_)
        for r in range(min(TOPK, nb)):
            mx = jnp.max(gate, axis=0, keepdims=True)
            first = jnp.min(jnp.where(gate == mx, blk_id, nb), axis=0, keepdims=True)
            pick = (blk_id == first) & (qb > r)
            sel = sel | pick
            gate = jnp.where(pick, -jnp.inf, gate)
        sel_far = jnp.where(sel & (blk_id < qb - 1), 0.0, NEG)
        sel_rows = jnp.concatenate(
            [sel_far, jnp.zeros((HEAD_DIM - nb, gk), F32)], axis=0).astype(BF16)
        adj_sel = jnp.max(jnp.where(sel & (blk_id == qb - 1), 1.0, 0.0), axis=0, keepdims=True)
        pen = jnp.where(adj_sel > 0.0, 0.0, NEG)
        for t in range(GROUP // QPAIR):
            cols = slice(t * pw, (t + 1) * pw)
            qat_ref[c * (GROUP // QPAIR) + t, :HEAD_DIM, :] = q_t[:, cols]
            qat_ref[c * (GROUP // QPAIR) + t, HEAD_DIM:, :] = sel_rows[:, cols]
        for j in range(GROUP):
            cols = slice(j * BLOCK, (j + 1) * BLOCK)
            pen_ref[c * GROUP + j] = jnp.broadcast_to(pen[:, cols], (8, BLOCK))
        return carry

    lax.fori_loop(0, nb // GROUP, route, 0)

    def colmax(s):
        return jnp.max(s, axis=0, keepdims=True)

    def colsum(p):
        return jnp.sum(p, axis=0, keepdims=True)

    def pv(g, first, p):
        keys = slice(first * BLOCK, first * BLOCK + p.shape[0])
        return dot(vt_ref[g, :, keys], p.astype(BF16))

    def do_class(c, carry):
        def far_logits(g, t):
            return dot(kaug_ref[g], qat_ref[c * (GROUP // QPAIR) + t])

        def far_park(g, t, s_ref):
            s = far_logits(g, t)
            s_ref[...] = s
            return colmax(s)

        lanes_of = [slice((j % QPAIR) * BLOCK, (j % QPAIR + 1) * BLOCK) for j in range(GROUP)]
        near = []
        for j in range(GROUP):
            qb = c * GROUP + j
            pair = c * (GROUP // QPAIR) + j // QPAIR
            q_t = qat_ref[pair, :HEAD_DIM, lanes_of[j]]
            parts = [(dot(kaug_ref[c, j * BLOCK:(j + 1) * BLOCK, :HEAD_DIM], q_t) + bown_ref[...], c, j)]
            if qb > 0:
                g_adj, t_adj = (c, j - 1) if j > 0 else (c - 1, GROUP - 1)
                k_adj = kaug_ref[g_adj, t_adj * BLOCK:(t_adj + 1) * BLOCK, :HEAD_DIM]
                parts.append((dot(k_adj, q_t) + (badj_ref[...] + pen_ref[qb, 0:1, :]), g_adj, t_adj))
            if j >= 2:
                parts.append((dot(kaug_ref[c, :(j - 1) * BLOCK, :], qat_ref[pair, :, lanes_of[j]]), c, 0))
            near.append(parts)

        if c > 0:
            cmax_a = far_park(0, 0, sa_ref)

        for j, parts in enumerate(near):
            m = functools.reduce(jnp.maximum, [colmax(s) for s, _, _ in parts])
            probs = [(jnp.exp2(s - m), g, first) for s, g, first in parts]
            m_ref[j // QPAIR, :, lanes_of[j]] = m
            l_ref[j // QPAIR, :, lanes_of[j]] = functools.reduce(jnp.add, [colsum(p) for p, _, _ in probs])
            acc_ref[j // QPAIR, :, lanes_of[j]] = functools.reduce(
                jnp.add, [pv(g, first, p) for p, g, first in probs])

        def far_update(g, t, s_ref, cmax):
            m_old = m_ref[t]
            m_new = jnp.maximum(m_old, cmax)
            alpha = jnp.exp2(m_old - m_new)
            l_new = alpha * l_ref[t]
            acc_new = alpha * acc_ref[t]
            for k in range(GROUP):
                p = jnp.exp2(s_ref[k * BLOCK:(k + 1) * BLOCK, :] - m_new)
                l_new = l_new + colsum(p)
                acc_new = acc_new + pv(g, k, p)
            l_ref[t] = l_new
            acc_ref[t] = acc_new
            m_ref[t] = m_new

        for g in range(c):
            cmax_b = far_park(g, 1, sb_ref)
            far_update(g, 0, sa_ref, cmax_a)
            if g + 1 < c:
                cmax_a = far_park(g + 1, 0, sa_ref)
            far_update(g, 1, sb_ref, cmax_b)

        for j in range(GROUP):
            rows = pl.ds((c * GROUP + j) * BLOCK, BLOCK)
            out_t = acc_ref[j // QPAIR, :, lanes_of[j]] * (1.0 / l_ref[j // QPAIR, :, lanes_of[j]])
            o_ref[rows, :] = (out_t.T * sz_ref[rows, :]).astype(o_ref.dtype)
        return carry

    one = jnp.minimum(h + 1, 1)
    for c in range(nb // GROUP):
        lax.fori_loop(0, one, lambda _, carry, c=c: do_class(c, carry), 0)


def _moba_attention(q, ks, v, kmean, sz, rel_bias):
    s, attn_w = q.shape
    nh = attn_w // HEAD_DIM
    nb = s // BLOCK
    assert nb % GROUP == 0 and nb <= HEAD_DIM
    idx_own, idx_adj = _bucket_tables()
    head_cols = pl.BlockSpec((s, HEAD_DIM), lambda h: (0, h))
    const_tile = pl.BlockSpec((BLOCK, BLOCK), lambda h: (0, 0))
    return pl.pallas_call(
        functools.partial(_attn_kernel, nb=nb),
        grid=(nh,),
        in_specs=[pl.BlockSpec(memory_space=pltpu.SMEM),
                  const_tile, const_tile,
                  head_cols, head_cols, head_cols,
                  pl.BlockSpec((nb, HEAD_DIM), lambda h: (0, h)),
                  head_cols],
        out_specs=head_cols,
        out_shape=jax.ShapeDtypeStruct((s, attn_w), BF16),
        scratch_shapes=[pltpu.VMEM((nb // GROUP, GROUP * BLOCK, 2 * HEAD_DIM), BF16),
                        pltpu.VMEM((nb // GROUP, HEAD_DIM, GROUP * BLOCK), BF16),
                        pltpu.VMEM((BLOCK, BLOCK), F32),
                        pltpu.VMEM((BLOCK, BLOCK), F32),
                        pltpu.VMEM((nb // QPAIR, 2 * HEAD_DIM, QPAIR * BLOCK), BF16),
                        pltpu.VMEM((nb, 8, BLOCK), F32),
                        pltpu.VMEM((GROUP // QPAIR, 1, QPAIR * BLOCK), F32),
                        pltpu.VMEM((GROUP // QPAIR, 1, QPAIR * BLOCK), F32),
                        pltpu.VMEM((GROUP // QPAIR, HEAD_DIM, QPAIR * BLOCK), F32),
                        pltpu.VMEM((GROUP * BLOCK, QPAIR * BLOCK), F32),
                        pltpu.VMEM((GROUP * BLOCK, QPAIR * BLOCK), F32)],
        compiler_params=pltpu.CompilerParams(
            dimension_semantics=("arbitrary",),
            vmem_limit_bytes=V7X_VMEM_LIMIT_BYTES),
        name="moba_attn",
    )(rel_bias, jnp.asarray(idx_own), jnp.asarray(idx_adj), q, ks, v, kmean, sz)


def _out_kernel(a_ref, c_ref, wa_ref, wc_ref, x_ref, g_ref, o_ref, *, n_chunks):
    d = o_ref.shape[1]
    w = d // n_chunks
    a = a_ref[...]
    c = c_ref[...]
    hs = []
    ss = jnp.zeros((o_ref.shape[0], 1), F32)
    for n in range(n_chunks):
        cols = slice(n * w, (n + 1) * w)
        h = (x_ref[:, cols]
             + jnp.dot(a, wa_ref[:, cols], preferred_element_type=F32)
             + jnp.dot(c, wc_ref[:, cols], preferred_element_type=F32))
        ss = ss + jnp.sum(h * h, axis=-1, keepdims=True)
        hs.append(h)
    inv = lax.rsqrt(ss / d + EPS)
    for n, h in enumerate(hs):
        cols = slice(n * w, (n + 1) * w)
        o_ref[:, cols] = h * inv * g_ref[:, cols]


def _out_proj(attn, conv, w_out, x, g, tm=256):
    s, d = x.shape
    ka = attn.shape[1]
    kc = conv.shape[1]
    assert ka == kc
    resident = pl.Buffered(1)
    return pl.pallas_call(
        functools.partial(_out_kernel, n_chunks=4),
        grid=(s // tm,),
        in_specs=[pl.BlockSpec((tm, ka), lambda i: (i, 0)),
                  pl.BlockSpec((tm, kc), lambda i: (i, 0)),
                  pl.BlockSpec((ka, d), lambda i: (0, 0), pipeline_mode=resident),
                  pl.BlockSpec((kc, d), lambda i: (1, 0), pipeline_mode=resident),
                  pl.BlockSpec((tm, d), lambda i: (i, 0)),
                  pl.BlockSpec((1, d), lambda i: (0, 0))],
        out_specs=pl.BlockSpec((tm, d), lambda i: (i, 0)),
        out_shape=jax.ShapeDtypeStruct((s, d), F32),
        compiler_params=pltpu.CompilerParams(
            dimension_semantics=("arbitrary",),
            vmem_limit_bytes=V7X_OUT_PROJ_VMEM_LIMIT_BYTES),
        name="out_proj_norm",
    )(attn, conv, w_out, w_out, x, g.reshape(1, d))


def kernel(x, norm_gain, w_in, conv_w, w_out, rel_bias, final_gain):
    b, s, d = x.shape
    depth = norm_gain.shape[0]
    assert b == 1 and depth == 1, "kernel is written for one sequence and one layer"
    attn_w = d // 2
    conv_width = d - attn_w
    assert rel_bias.shape == (NUM_BUCKETS, attn_w // HEAD_DIM)
    assert w_in.shape[2] == 4 * attn_w + 4 * conv_width

    x2 = x[0]
    assert attn_w == conv_width
    pair_w = 2 * attn_w
    w_qk_b = w_in[0, :, :pair_w].astype(BF16)

    u = _rmsnorm_bf16(x2, norm_gain[0])
    tm, tn = 1024, 512
    tile = pl.BlockSpec((tm, tn), lambda j, i: (i, j))
    act = jax.ShapeDtypeStruct((s, attn_w), BF16)
    q, ks, kmean, w_vz_b, w_hb_b = _proj_pair(
        functools.partial(_proj_qk_kernel, scale=HEAD_DIM ** -0.5 * LOG2E), "proj_qk", u, w_qk_b, attn_w,
        [tile, tile, pl.BlockSpec((1, tm // BLOCK, tn), lambda j, i: (i, 0, j))],
        [act, act, jax.ShapeDtypeStruct((s // tm, tm // BLOCK, attn_w), F32)],
        [(w_in[0], 1, pair_w), (w_in[0], 2, pair_w)], tm, tn)
    v, sz, w_cz_b = _proj_pair(
        _proj_vz_kernel, "proj_vz", u, w_vz_b, attn_w,
        [tile, tile], [act, jax.ShapeDtypeStruct((s, attn_w), F32)],
        [(w_in[0], 3, pair_w)], tm, tn)
    conv, w_out_b = _proj_conv(u, w_hb_b, w_cz_b, conv_w[0], conv_width, w_out[0])
    attn = _moba_attention(q, ks, v, kmean.reshape(s // BLOCK, attn_w), sz, rel_bias)
    out = _out_proj(attn, conv, w_out_b, x2, final_gain)
    return out[None]
```

```python
import functools
import math

import numpy as np
import jax
import jax.numpy as jnp
from jax import lax
from jax.experimental import pallas as pl
from jax.experimental.pallas import tpu as pltpu

HEAD_DIM = 128
BLOCK = 256
TOPK = 3
NUM_BUCKETS = 32
MAX_DISTANCE = 128
EPS = 1e-6
NEG = -1e30
LOG2E = math.log2(math.e)

F32 = jnp.float32
BF16 = jnp.bfloat16

V7X_VMEM_LIMIT_BYTES = 56 * 1024 * 1024
V7X_OUT_PROJ_VMEM_LIMIT_BYTES = 60 * 1024 * 1024


def _silu(z):
    return z * (1.0 / (1.0 + jnp.exp(-z)))


def _rmsnorm_kernel(x_ref, g_ref, o_ref):
    x = x_ref[...]
    ms = jnp.mean(x * x, axis=-1, keepdims=True)
    o_ref[...] = (x * lax.rsqrt(ms + EPS) * g_ref[...]).astype(o_ref.dtype)


def _rmsnorm_bf16(x, g, tm=512):
    s, d = x.shape
    return pl.pallas_call(
        _rmsnorm_kernel,
        grid=(s // tm,),
        in_specs=[pl.BlockSpec((tm, d), lambda i: (i, 0)),
                  pl.BlockSpec((1, d), lambda i: (0, 0))],
        out_specs=pl.BlockSpec((tm, d), lambda i: (i, 0)),
        out_shape=jax.ShapeDtypeStruct((s, d), BF16),
        compiler_params=pltpu.CompilerParams(dimension_semantics=("arbitrary",)),
        name="rmsnorm_in",
    )(x, g.reshape(1, d))


def _run_side_jobs(side_refs, side_out_refs):
    for src, dst in zip(side_refs, side_out_refs):
        dst[...] = src[...].astype(dst.dtype)


def _side_cast_specs(side, col_block, cols, n_steps, step_of):
    rows = side.shape[0]
    chunk = rows // n_steps
    assert chunk * n_steps == rows and chunk % 16 == 0 and side.shape[1] % cols == 0
    return (pl.BlockSpec((chunk, cols), lambda j, i: (step_of(j, i), col_block)),
            pl.BlockSpec((chunk, cols), lambda j, i: (step_of(j, i), 0)),
            jax.ShapeDtypeStruct((rows, cols), BF16))


def _proj_qk_kernel(u_ref, wq_ref, wk_ref, *refs, scale, n_side):
    side_refs, (q_ref, ks_ref, km_ref), side_out_refs = refs[:n_side], refs[n_side:n_side + 3], refs[n_side + 3:]
    _run_side_jobs(side_refs, side_out_refs)
    u = u_ref[...]
    tm = u.shape[0]
    k = jnp.dot(u, wk_ref[...], preferred_element_type=F32)
    ks_ref[...] = (k * scale).astype(BF16)
    km_ref[0] = jnp.mean(k.reshape(tm // BLOCK, BLOCK, k.shape[1]), axis=1)
    q = jnp.dot(u, wq_ref[...], preferred_element_type=F32)
    q_ref[...] = q.astype(BF16)


def _proj_vz_kernel(u_ref, wv_ref, wz_ref, *refs, n_side):
    side_refs, (v_ref, sz_ref), side_out_refs = refs[:n_side], refs[n_side:n_side + 2], refs[n_side + 2:]
    _run_side_jobs(side_refs, side_out_refs)
    u = u_ref[...]
    z = jnp.dot(u, wz_ref[...], preferred_element_type=F32)
    sz_ref[...] = _silu(z)
    v = jnp.dot(u, wv_ref[...], preferred_element_type=F32)
    v_ref[...] = v.astype(BF16)


def _proj_pair(kernel_fn, name, u, w, width, out_specs, out_shape, side_jobs, tm, tn):
    s, d = u.shape
    nj = width // tn
    ni = s // tm
    jobs = [_side_cast_specs(m, cb, cols, nj * ni, lambda j, i: j * ni + i) for m, cb, cols in side_jobs]
    w_spec = lambda part: pl.BlockSpec((d, tn), lambda j, i, part=part: (0, part * nj + j))
    return pl.pallas_call(
        functools.partial(kernel_fn, n_side=len(jobs)),
        grid=(nj, ni),
        in_specs=[pl.BlockSpec((tm, d), lambda j, i: (i, 0)), w_spec(0), w_spec(1)] + [jb[0] for jb in jobs],
        out_specs=out_specs + [jb[1] for jb in jobs],
        out_shape=out_shape + [jb[2] for jb in jobs],
        compiler_params=pltpu.CompilerParams(
            dimension_semantics=("arbitrary", "arbitrary"),
            vmem_limit_bytes=V7X_VMEM_LIMIT_BYTES),
        name=name,
    )(u, w, w, *[m for m, _, _ in side_jobs])


def _proj_conv_kernel(u_ref, wh_ref, wb_ref, wc_ref, wz_ref, cw_ref, side_ref,
                      o_ref, side_out_ref, carry_ref):
    _run_side_jobs([side_ref], [side_out_ref])
    @pl.when(pl.program_id(1) == 0)
    def _():
        carry_ref[...] = jnp.zeros_like(carry_ref)

    u = u_ref[...]
    tm = u.shape[0]
    hc = jnp.dot(u, wh_ref[...], preferred_element_type=F32)
    c = jnp.dot(u, wc_ref[...], preferred_element_type=F32)
    p = c * hc
    rows = lax.broadcasted_iota(jnp.int32, p.shape, 0)
    prev = carry_ref[...]
    prev1 = prev[7:8, :]
    prev2 = prev[6:7, :]
    p1 = jnp.where(rows == 0, prev1, pltpu.roll(p, 1, 0))
    p2 = jnp.where(rows == 0, prev2, jnp.where(rows == 1, prev1, pltpu.roll(p, 2, 0)))
    carry_ref[...] = p[tm - 8:, :]
    cw = cw_ref[...]
    y = p2 * cw[0:1, :] + p1 * cw[1:2, :] + p * cw[2:3, :]
    z = jnp.dot(u, wz_ref[...], preferred_element_type=F32)
    gate = _silu(z)
    b = jnp.dot(u, wb_ref[...], preferred_element_type=F32)
    o_ref[...] = (b * y * gate).astype(o_ref.dtype)


def _proj_conv(u, w_hb, w_cz, conv_w, conv_width, side, tm=1024, tn=256):
    s, d = u.shape
    nj = conv_width // tn
    ni = s // tm
    side_in, side_out, side_shape = _side_cast_specs(
        side, 0, side.shape[1], nj * ni, lambda j, i: j * ni + i)
    w_spec = lambda part: pl.BlockSpec((d, tn), lambda j, i, part=part: (0, part * nj + j))
    return pl.pallas_call(
        _proj_conv_kernel,
        grid=(nj, ni),
        in_specs=[pl.BlockSpec((tm, d), lambda j, i: (i, 0)),
                  w_spec(0), w_spec(1), w_spec(0), w_spec(1),
                  pl.BlockSpec((conv_w.shape[0], tn), lambda j, i: (0, j)), side_in],
        out_specs=[pl.BlockSpec((tm, tn), lambda j, i: (i, j)), side_out],
        out_shape=[jax.ShapeDtypeStruct((s, conv_width), BF16), side_shape],
        scratch_shapes=[pltpu.VMEM((8, tn), F32)],
        compiler_params=pltpu.CompilerParams(
            dimension_semantics=("arbitrary", "arbitrary"),
            vmem_limit_bytes=V7X_VMEM_LIMIT_BYTES),
        name="proj_conv",
    )(u, w_hb, w_hb, w_cz, w_cz, conv_w, side)


def _rel_bucket_np(dist):
    n = np.maximum(dist, 0)
    max_exact = NUM_BUCKETS // 2
    nf = np.maximum(n, 1).astype(np.float32)
    ratio = np.log(nf / np.float32(max_exact)) / np.float32(math.log(MAX_DISTANCE / max_exact))
    large = max_exact + (ratio * np.float32(NUM_BUCKETS - max_exact)).astype(np.int32)
    large = np.minimum(large, NUM_BUCKETS - 1)
    return np.where(n < max_exact, n, large).astype(np.int32)


def _bucket_tables():
    key = np.arange(BLOCK)[:, None]
    qry = np.arange(BLOCK)[None, :]
    d_own = qry - key
    own = np.where(d_own >= 0, _rel_bucket_np(d_own), NUM_BUCKETS).astype(np.int32)
    adj = _rel_bucket_np(BLOCK + qry - key)
    return own, adj


GROUP = 4
QPAIR = 2


def _attn_kernel(relb_ref, idx_own_ref, idx_adj_ref, q_ref, ks_ref, v_ref, km_ref, sz_ref,
                 o_ref,
                 kaug_ref, vt_ref, bown_ref, badj_ref, qat_ref, pen_ref, m_ref, l_ref, acc_ref,
                 sa_ref, sb_ref, *, nb):
    h = pl.program_id(0)
    gk = GROUP * BLOCK
    pw = QPAIR * BLOCK
    dot = functools.partial(jnp.dot, preferred_element_type=F32)

    @pl.when(h == 0)
    def _():
        lane = lax.broadcasted_iota(jnp.int32, (BLOCK, HEAD_DIM), 1)
        for blk in range(nb):
            r0 = (blk % GROUP) * BLOCK
            kaug_ref[blk // GROUP, r0:r0 + BLOCK, HEAD_DIM:] = jnp.where(lane == blk, 1.0, 0.0).astype(BF16)

    far_bias = relb_ref[NUM_BUCKETS - 1, h]
    io = idx_own_ref[...]
    ia = idx_adj_ref[...]
    own = jnp.full((BLOCK, BLOCK), NEG, F32)
    adj = jnp.zeros((BLOCK, BLOCK), F32)
    for b in range(NUM_BUCKETS):
        t = (relb_ref[b, h] - far_bias) * LOG2E
        own = jnp.where(io == b, t, own)
        adj = jnp.where(ia == b, t, adj)
    bown_ref[...] = own
    badj_ref[...] = adj

    def stage(g, carry):
        rows = pl.ds(pl.multiple_of(g * gk, gk), gk)
        kaug_ref[g, :, :HEAD_DIM] = ks_ref[rows, :]
        vt_ref[g] = v_ref[rows, :].astype(F32).T.astype(BF16)
        return carry

    lax.fori_loop(0, nb // GROUP, stage, 0)

    km = km_ref[...]
    km_hi = km.astype(BF16)
    km_lo = (km - km_hi.astype(F32)).astype(BF16)

    def route(c, carry):
        rows = pl.ds(pl.multiple_of(c * gk, gk), gk)
        q_t = q_ref[rows, :].astype(F32).T.astype(BF16)
        gate = dot(km_hi, q_t) + dot(km_lo, q_t)
        blk_id = lax.broadcasted_iota(jnp.int32, gate.shape, 0)
        qb = c * GROUP + lax.shift_right_logical(
            lax.broadcasted_iota(jnp.int32, gate.shape, 1), int(math.log2(BLOCK)))
        gate = jnp.where(blk_id < qb, gate, -jnp.inf)
        sel = jnp.zeros(gate.shape, jnp.bool_)
        for r in range(min(TOPK, nb)):
            mx = jnp.max(gate, axis=0, keepdims=True)
            first = jnp.min(jnp.where(gate == mx, blk_id, nb), axis=0, keepdims=True)
            pick = (blk_id == first) & (qb > r)
            sel = sel | pick
            gate = jnp.where(pick, -jnp.inf, gate)
        sel_far = jnp.where(sel & (blk_id < qb - 1), 0.0, NEG)
        sel_rows = jnp.concatenate(
            [sel_far, jnp.zeros((HEAD_DIM - nb, gk), F32)], axis=0).astype(BF16)
        adj_sel = jnp.max(jnp.where(sel & (blk_id == qb - 1), 1.0, 0.0), axis=0, keepdims=True)
        pen = jnp.where(adj_sel > 0.0, 0.0, NEG)
        for t in range(GROUP // QPAIR):
            cols = slice(t * pw, (t + 1) * pw)
            qat_ref[c * (GROUP // QPAIR) + t, :HEAD_DIM, :] = q_t[:, cols]
            qat_ref[c * (GROUP // QPAIR) + t, HEAD_DIM:, :] = sel_rows[:, cols]
        for j in range(GROUP):
            cols = slice(j * BLOCK, (j + 1) * BLOCK)
            pen_ref[c * GROUP + j] = jnp.broadcast_to(pen[:, cols], (8, BLOCK))
        return carry

    lax.fori_loop(0, nb // GROUP, route, 0)

    def colmax(s):
        return jnp.max(s, axis=0, keepdims=True)

    def colsum(p):
        return jnp.sum(p, axis=0, keepdims=True)

    def pv(g, first, p):
        keys = slice(first * BLOCK, first * BLOCK + p.shape[0])
        return dot(vt_ref[g, :, keys], p.astype(BF16))

    def do_class(c, carry):
        def far_logits(g, t):
            return dot(kaug_ref[g], qat_ref[c * (GROUP // QPAIR) + t])

        def far_park(g, t, s_ref):
            s = far_logits(g, t)
            s_ref[...] = s
            return colmax(s)

        lanes_of = [slice((j % QPAIR) * BLOCK, (j % QPAIR + 1) * BLOCK) for j in range(GROUP)]
        near = []
        for j in range(GROUP):
            qb = c * GROUP + j
            pair = c * (GROUP // QPAIR) + j // QPAIR
            q_t = qat_ref[pair, :HEAD_DIM, lanes_of[j]]
            parts = [(dot(kaug_ref[c, j * BLOCK:(j + 1) * BLOCK, :HEAD_DIM], q_t) + bown_ref[...], c, j)]
            if qb > 0:
                g_adj, t_adj = (c, j - 1) if j > 0 else (c - 1, GROUP - 1)
                k_adj = kaug_ref[g_adj, t_adj * BLOCK:(t_adj + 1) * BLOCK, :HEAD_DIM]
                parts.append((dot(k_adj, q_t) + (badj_ref[...] + pen_ref[qb, 0:1, :]), g_adj, t_adj))
            if j >= 2:
                parts.append((dot(kaug_ref[c, :(j - 1) * BLOCK, :], qat_ref[pair, :, lanes_of[j]]), c, 0))
            near.append(parts)

        if c > 0:
            cmax_a = far_park(0, 0, sa_ref)

        for j, parts in enumerate(near):
            m = functools.reduce(jnp.maximum, [colmax(s) for s, _, _ in parts])
            probs = [(jnp.exp2(s - m), g, first) for s, g, first in parts]
            m_ref[j // QPAIR, :, lanes_of[j]] = m
            l_ref[j // QPAIR, :, lanes_of[j]] = functools.reduce(jnp.add, [colsum(p) for p, _, _ in probs])
            acc_ref[j // QPAIR, :, lanes_of[j]] = functools.reduce(
                jnp.add, [pv(g, first, p) for p, g, first in probs])

        def far_update(g, t, s_ref, cmax):
            m_old = m_ref[t]
            m_new = jnp.maximum(m_old, cmax)
            alpha = jnp.exp2(m_old - m_new)
            l_new = alpha * l_ref[t]
            acc_new = alpha * acc_ref[t]
            for k in range(GROUP):
                p = jnp.exp2(s_ref[k * BLOCK:(k + 1) * BLOCK, :] - m_new)
                l_new = l_new + colsum(p)
                acc_new = acc_new + pv(g, k, p)
            l_ref[t] = l_new
            acc_ref[t] = acc_new
            m_ref[t] = m_new

        for g in range(c):
            cmax_b = far_park(g, 1, sb_ref)
            far_update(g, 0, sa_ref, cmax_a)
            if g + 1 < c:
                cmax_a = far_park(g + 1, 0, sa_ref)
            far_update(g, 1, sb_ref, cmax_b)

        for j in range(GROUP):
            rows = pl.ds((c * GROUP + j) * BLOCK, BLOCK)
            out_t = acc_ref[j // QPAIR, :, lanes_of[j]] * (1.0 / l_ref[j // QPAIR, :, lanes_of[j]])
            o_ref[rows, :] = (out_t.T * sz_ref[rows, :]).astype(o_ref.dtype)
        return carry

    one = jnp.minimum(h + 1, 1)
    for c in range(nb // GROUP):
        lax.fori_loop(0, one, lambda _, carry, c=c: do_class(c, carry), 0)


def _moba_attention(q, ks, v, kmean, sz, rel_bias):
    s, attn_w = q.shape
    nh = attn_w // HEAD_DIM
    nb = s // BLOCK
    assert nb % GROUP == 0 and nb <= HEAD_DIM
    idx_own, idx_adj = _bucket_tables()
    head_cols = pl.BlockSpec((s, HEAD_DIM), lambda h: (0, h))
    const_tile = pl.BlockSpec((BLOCK, BLOCK), lambda h: (0, 0))
    return pl.pallas_call(
        functools.partial(_attn_kernel, nb=nb),
        grid=(nh,),
        in_specs=[pl.BlockSpec(memory_space=pltpu.SMEM),
                  const_tile, const_tile,
                  head_cols, head_cols, head_cols,
                  pl.BlockSpec((nb, HEAD_DIM), lambda h: (0, h)),
                  head_cols],
        out_specs=head_cols,
        out_shape=jax.ShapeDtypeStruct((s, attn_w), BF16),
        scratch_shapes=[pltpu.VMEM((nb // GROUP, GROUP * BLOCK, 2 * HEAD_DIM), BF16),
                        pltpu.VMEM((nb // GROUP, HEAD_DIM, GROUP * BLOCK), BF16),
                        pltpu.VMEM((BLOCK, BLOCK), F32),
                        pltpu.VMEM((BLOCK, BLOCK), F32),
                        pltpu.VMEM((nb // QPAIR, 2 * HEAD_DIM, QPAIR * BLOCK), BF16),
                        pltpu.VMEM((nb, 8, BLOCK), F32),
                        pltpu.VMEM((GROUP // QPAIR, 1, QPAIR * BLOCK), F32),
                        pltpu.VMEM((GROUP // QPAIR, 1, QPAIR * BLOCK), F32),
                        pltpu.VMEM((GROUP // QPAIR, HEAD_DIM, QPAIR * BLOCK), F32),
                        pltpu.VMEM((GROUP * BLOCK, QPAIR * BLOCK), F32),
                        pltpu.VMEM((GROUP * BLOCK, QPAIR * BLOCK), F32)],
        compiler_params=pltpu.CompilerParams(
            dimension_semantics=("arbitrary",),
            vmem_limit_bytes=V7X_VMEM_LIMIT_BYTES),
        name="moba_attn",
    )(rel_bias, jnp.asarray(idx_own), jnp.asarray(idx_adj), q, ks, v, kmean, sz)


def _out_kernel(a_ref, c_ref, wa_ref, wc_ref, x_ref, g_ref, o_ref, *, n_chunks):
    d = o_ref.shape[1]
    w = d // n_chunks
    a = a_ref[...]
    c = c_ref[...]
    hs = []
    ss = jnp.zeros((o_ref.shape[0], 1), F32)
    for n in range(n_chunks):
        cols = slice(n * w, (n + 1) * w)
        h = (x_ref[:, cols]
             + jnp.dot(a, wa_ref[:, cols], preferred_element_type=F32)
             + jnp.dot(c, wc_ref[:, cols], preferred_element_type=F32))
        ss = ss + jnp.sum(h * h, axis=-1, keepdims=True)
        hs.append(h)
    inv = lax.rsqrt(ss / d + EPS)
    for n, h in enumerate(hs):
        cols = slice(n * w, (n + 1) * w)
        o_ref[:, cols] = h * inv * g_ref[:, cols]


def _out_proj(attn, conv, w_out, x, g, tm=256):
    s, d = x.shape
    ka = attn.shape[1]
    kc = conv.shape[1]
    assert ka == kc
    resident = pl.Buffered(1)
    return pl.pallas_call(
        functools.partial(_out_kernel, n_chunks=4),
        grid=(s // tm,),
        in_specs=[pl.BlockSpec((tm, ka), lambda i: (i, 0)),
                  pl.BlockSpec((tm, kc), lambda i: (i, 0)),
                  pl.BlockSpec((ka, d), lambda i: (0, 0), pipeline_mode=resident),
                  pl.BlockSpec((kc, d), lambda i: (1, 0), pipeline_mode=resident),
                  pl.BlockSpec((tm, d), lambda i: (i, 0)),
                  pl.BlockSpec((1, d), lambda i: (0, 0))],
        out_specs=pl.BlockSpec((tm, d), lambda i: (i, 0)),
        out_shape=jax.ShapeDtypeStruct((s, d), F32),
        compiler_params=pltpu.CompilerParams(
            dimension_semantics=("arbitrary",),
            vmem_limit_bytes=V7X_OUT_PROJ_VMEM_LIMIT_BYTES),
        name="out_proj_norm",
    )(attn, conv, w_out, w_out, x, g.reshape(1, d))


def kernel(x, norm_gain, w_in, conv_w, w_out, rel_bias, final_gain):
    b, s, d = x.shape
    depth = norm_gain.shape[0]
    assert b == 1 and depth == 1, "kernel is written for one sequence and one layer"
    attn_w = d // 2
    conv_width = d - attn_w
    assert rel_bias.shape == (NUM_BUCKETS, attn_w // HEAD_DIM)
    assert w_in.shape[2] == 4 * attn_w + 4 * conv_width

    x2 = x[0]
    assert attn_w == conv_width
    pair_w = 2 * attn_w
    w_qk_b = w_in[0, :, :pair_w].astype(BF16)

    u = _rmsnorm_bf16(x2, norm_gain[0])
    tm, tn = 1024, 512
    tile = pl.BlockSpec((tm, tn), lambda j, i: (i, j))
    act = jax.ShapeDtypeStruct((s, attn_w), BF16)
    q, ks, kmean, w_vz_b, w_hb_b = _proj_pair(
        functools.partial(_proj_qk_kernel, scale=HEAD_DIM ** -0.5 * LOG2E), "proj_qk", u, w_qk_b, attn_w,
        [tile, tile, pl.BlockSpec((1, tm // BLOCK, tn), lambda j, i: (i, 0, j))],
        [act, act, jax.ShapeDtypeStruct((s // tm, tm // BLOCK, attn_w), F32)],
        [(w_in[0], 1, pair_w), (w_in[0], 2, pair_w)], tm, tn)
    v, sz, w_cz_b = _proj_pair(
        _proj_vz_kernel, "proj_vz", u, w_vz_b, attn_w,
        [tile, tile], [act, jax.ShapeDtypeStruct((s, attn_w), F32)],
        [(w_in[0], 3, pair_w)], tm, tn)
    conv, w_out_b = _proj_conv(u, w_hb_b, w_cz_b, conv_w[0], conv_width, w_out[0])
    attn = _moba_attention(q, ks, v, kmean.reshape(s // BLOCK, attn_w), sz, rel_bias)
    out = _out_proj(attn, conv, w_out_b, x2, final_gain)
    return out[None]
```

```python
import functools
import math

import numpy as np
import jax
import jax.numpy as jnp
from jax import lax
from jax.experimental import pallas as pl
from jax.experimental.pallas import tpu as pltpu

HEAD_DIM = 128
BLOCK = 256
TOPK = 3
NUM_BUCKETS = 32
MAX_DISTANCE = 128
EPS = 1e-6
NEG = -1e30
LOG2E = math.log2(math.e)

F32 = jnp.float32
BF16 = jnp.bfloat16

V7X_VMEM_LIMIT_BYTES = 56 * 1024 * 1024
V7X_OUT_PROJ_VMEM_LIMIT_BYTES = 60 * 1024 * 1024


def _silu(z):
    return z * (1.0 / (1.0 + jnp.exp(-z)))


def _rmsnorm_kernel(x_ref, g_ref, o_ref):
    x = x_ref[...]
    ms = jnp.mean(x * x, axis=-1, keepdims=True)
    o_ref[...] = (x * lax.rsqrt(ms + EPS) * g_ref[...]).astype(o_ref.dtype)


def _rmsnorm_bf16(x, g, tm=512):
    s, d = x.shape
    return pl.pallas_call(
        _rmsnorm_kernel,
        grid=(s // tm,),
        in_specs=[pl.BlockSpec((tm, d), lambda i: (i, 0)),
                  pl.BlockSpec((1, d), lambda i: (0, 0))],
        out_specs=pl.BlockSpec((tm, d), lambda i: (i, 0)),
        out_shape=jax.ShapeDtypeStruct((s, d), BF16),
        compiler_params=pltpu.CompilerParams(dimension_semantics=("arbitrary",)),
        name="rmsnorm_in",
    )(x, g.reshape(1, d))


def _run_side_jobs(side_refs, side_out_refs):
    for src, dst in zip(side_refs, side_out_refs):
        dst[...] = src[...].astype(dst.dtype)


def _side_cast_specs(side, col_block, cols, n_steps, step_of):
    rows = side.shape[0]
    chunk = rows // n_steps
    assert chunk * n_steps == rows and chunk % 16 == 0 and side.shape[1] % cols == 0
    return (pl.BlockSpec((chunk, cols), lambda j, i: (step_of(j, i), col_block)),
            pl.BlockSpec((chunk, cols), lambda j, i: (step_of(j, i), 0)),
            jax.ShapeDtypeStruct((rows, cols), BF16))


def _proj_qk_kernel(u_ref, wq_ref, wk_ref, *refs, scale, n_side):
    side_refs, (q_ref, ks_ref, km_ref), side_out_refs = refs[:n_side], refs[n_side:n_side + 3], refs[n_side + 3:]
    _run_side_jobs(side_refs, side_out_refs)
    u = u_ref[...]
    tm = u.shape[0]
    k = jnp.dot(u, wk_ref[...], preferred_element_type=F32)
    ks_ref[...] = (k * scale).astype(BF16)
    km_ref[0] = jnp.mean(k.reshape(tm // BLOCK, BLOCK, k.shape[1]), axis=1)
    q = jnp.dot(u, wq_ref[...], preferred_element_type=F32)
    q_ref[...] = q.astype(BF16)


def _proj_vz_kernel(u_ref, wv_ref, wz_ref, *refs, n_side):
    side_refs, (v_ref, sz_ref), side_out_refs = refs[:n_side], refs[n_side:n_side + 2], refs[n_side + 2:]
    _run_side_jobs(side_refs, side_out_refs)
    u = u_ref[...]
    z = jnp.dot(u, wz_ref[...], preferred_element_type=F32)
    sz_ref[...] = _silu(z)
    v = jnp.dot(u, wv_ref[...], preferred_element_type=F32)
    v_ref[...] = v.astype(BF16)


def _proj_pair(kernel_fn, name, u, w, width, out_specs, out_shape, side_jobs, tm, tn):
    s, d = u.shape
    nj = width // tn
    ni = s // tm
    jobs = [_side_cast_specs(m, cb, cols, nj * ni, lambda j, i: j * ni + i) for m, cb, cols in side_jobs]
    w_spec = lambda part: pl.BlockSpec((d, tn), lambda j, i, part=part: (0, part * nj + j))
    return pl.pallas_call(
        functools.partial(kernel_fn, n_side=len(jobs)),
        grid=(nj, ni),
        in_specs=[pl.BlockSpec((tm, d), lambda j, i: (i, 0)), w_spec(0), w_spec(1)] + [jb[0] for jb in jobs],
        out_specs=out_specs + [jb[1] for jb in jobs],
        out_shape=out_shape + [jb[2] for jb in jobs],
        compiler_params=pltpu.CompilerParams(
            dimension_semantics=("arbitrary", "arbitrary"),
            vmem_limit_bytes=V7X_VMEM_LIMIT_BYTES),
        name=name,
    )(u, w, w, *[m for m, _, _ in side_jobs])


def _proj_conv_kernel(u_ref, wh_ref, wb_ref, wc_ref, wz_ref, cw_ref, side_ref,
                      o_ref, side_out_ref, carry_ref):
    _run_side_jobs([side_ref], [side_out_ref])
    @pl.when(pl.program_id(1) == 0)
    def _():
        carry_ref[...] = jnp.zeros_like(carry_ref)

    u = u_ref[...]
    tm = u.shape[0]
    hc = jnp.dot(u, wh_ref[...], preferred_element_type=F32)
    c = jnp.dot(u, wc_ref[...], preferred_element_type=F32)
    p = c * hc
    rows = lax.broadcasted_iota(jnp.int32, p.shape, 0)
    prev = carry_ref[...]
    prev1 = prev[7:8, :]
    prev2 = prev[6:7, :]
    p1 = jnp.where(rows == 0, prev1, pltpu.roll(p, 1, 0))
    p2 = jnp.where(rows == 0, prev2, jnp.where(rows == 1, prev1, pltpu.roll(p, 2, 0)))
    carry_ref[...] = p[tm - 8:, :]
    cw = cw_ref[...]
    y = p2 * cw[0:1, :] + p1 * cw[1:2, :] + p * cw[2:3, :]
    z = jnp.dot(u, wz_ref[...], preferred_element_type=F32)
    gate = _silu(z)
    b = jnp.dot(u, wb_ref[...], preferred_element_type=F32)
    o_ref[...] = (b * y * gate).astype(o_ref.dtype)


def _proj_conv(u, w_hb, w_cz, conv_w, conv_width, side, tm=1024, tn=256):
    s, d = u.shape
    nj = conv_width // tn
    ni = s // tm
    side_in, side_out, side_shape = _side_cast_specs(
        side, 0, side.shape[1], nj * ni, lambda j, i: j * ni + i)
    w_spec = lambda part: pl.BlockSpec((d, tn), lambda j, i, part=part: (0, part * nj + j))
    return pl.pallas_call(
        _proj_conv_kernel,
        grid=(nj, ni),
        in_specs=[pl.BlockSpec((tm, d), lambda j, i: (i, 0)),
                  w_spec(0), w_spec(1), w_spec(0), w_spec(1),
                  pl.BlockSpec((conv_w.shape[0], tn), lambda j, i: (0, j)), side_in],
        out_specs=[pl.BlockSpec((tm, tn), lambda j, i: (i, j)), side_out],
        out_shape=[jax.ShapeDtypeStruct((s, conv_width), BF16), side_shape],
        scratch_shapes=[pltpu.VMEM((8, tn), F32)],
        compiler_params=pltpu.CompilerParams(
            dimension_semantics=("arbitrary", "arbitrary"),
            vmem_limit_bytes=V7X_VMEM_LIMIT_BYTES),
        name="proj_conv",
    )(u, w_hb, w_hb, w_cz, w_cz, conv_w, side)


def _rel_bucket_np(dist):
    n = np.maximum(dist, 0)
    max_exact = NUM_BUCKETS // 2
    nf = np.maximum(n, 1).astype(np.float32)
    ratio = np.log(nf / np.float32(max_exact)) / np.float32(math.log(MAX_DISTANCE / max_exact))
    large = max_exact + (ratio * np.float32(NUM_BUCKETS - max_exact)).astype(np.int32)
    large = np.minimum(large, NUM_BUCKETS - 1)
    return np.where(n < max_exact, n, large).astype(np.int32)


def _bucket_tables():
    key = np.arange(BLOCK)[:, None]
    qry = np.arange(BLOCK)[None, :]
    d_own = qry - key
    own = np.where(d_own >= 0, _rel_bucket_np(d_own), NUM_BUCKETS).astype(np.int32)
    adj = _rel_bucket_np(BLOCK + qry - key)
    return own, adj


GROUP = 4
QPAIR = 2


def _attn_kernel(relb_ref, idx_own_ref, idx_adj_ref, q_ref, ks_ref, v_ref, km_ref, sz_ref,
                 o_ref,
                 kaug_ref, vt_ref, bown_ref, badj_ref, qat_ref, pen_ref, m_ref, l_ref, acc_ref,
                 sa_ref, sb_ref, *, nb):
    h = pl.program_id(0)
    gk = GROUP * BLOCK
    pw = QPAIR * BLOCK
    dot = functools.partial(jnp.dot, preferred_element_type=F32)

    @pl.when(h == 0)
    def _():
        lane = lax.broadcasted_iota(jnp.int32, (BLOCK, HEAD_DIM), 1)
        for blk in range(nb):
            r0 = (blk % GROUP) * BLOCK
            kaug_ref[blk // GROUP, r0:r0 + BLOCK, HEAD_DIM:] = jnp.where(lane == blk, 1.0, 0.0).astype(BF16)

    far_bias = relb_ref[NUM_BUCKETS - 1, h]
    io = idx_own_ref[...]
    ia = idx_adj_ref[...]
    own = jnp.full((BLOCK, BLOCK), NEG, F32)
    adj = jnp.zeros((BLOCK, BLOCK), F32)
    for b in range(NUM_BUCKETS):
        t = (relb_ref[b, h] - far_bias) * LOG2E
        own = jnp.where(io == b, t, own)
        adj = jnp.where(ia == b, t, adj)
    bown_ref[...] = own
    badj_ref[...] = adj

    def stage(g, carry):
        rows = pl.ds(pl.multiple_of(g * gk, gk), gk)
        kaug_ref[g, :, :HEAD_DIM] = ks_ref[rows, :]
        vt_ref[g] = v_ref[rows, :].astype(F32).T.astype(BF16)
        return carry

    lax.fori_loop(0, nb // GROUP, stage, 0)

    km = km_ref[...]
    km_hi = km.astype(BF16)
    km_lo = (km - km_hi.astype(F32)).astype(BF16)

    ROUTE_CLASSES = 2
    rq = ROUTE_CLASSES * gk

    def route(cc, carry):
        rows = pl.ds(pl.multiple_of(cc * rq, rq), rq)
        q_t = q_ref[rows, :].astype(F32).T.astype(BF16)
        gate = dot(km_hi, q_t) + dot(km_lo, q_t)
        blk_id = lax.broadcasted_iota(jnp.int32, gate.shape, 0)
        qb = cc * (ROUTE_CLASSES * GROUP) + lax.shift_right_logical(
            lax.broadcasted_iota(jnp.int32, gate.shape, 1), int(math.log2(BLOCK)))
        gate = jnp.where(blk_id < qb, gate, -jnp.inf)
        sel = jnp.zeros(gate.shape, jnp.bool_)
        for r in range(min(TOPK, nb)):
            mx = jnp.max(gate, axis=0, keepdims=True)
            first = jnp.min(jnp.where(gate == mx, blk_id, nb), axis=0, keepdims=True)
            pick = (blk_id == first) & (qb > r)
            sel = sel | pick
            gate = jnp.where(pick, -jnp.inf, gate)
        sel_far = jnp.where(sel & (blk_id < qb - 1), 0.0, NEG)
        sel_rows = jnp.concatenate(
            [sel_far, jnp.zeros((HEAD_DIM - nb, rq), F32)], axis=0).astype(BF16)
        adj_sel = jnp.max(jnp.where(sel & (blk_id == qb - 1), 1.0, 0.0), axis=0, keepdims=True)
        pen = jnp.where(adj_sel > 0.0, 0.0, NEG)
        for t in range(rq // pw):
            cols = slice(t * pw, (t + 1) * pw)
            qat_ref[cc * (rq // pw) + t, :HEAD_DIM, :] = q_t[:, cols]
            qat_ref[cc * (rq // pw) + t, HEAD_DIM:, :] = sel_rows[:, cols]
        for j in range(rq // BLOCK):
            cols = slice(j * BLOCK, (j + 1) * BLOCK)
            pen_ref[cc * (rq // BLOCK) + j] = jnp.broadcast_to(pen[:, cols], (8, BLOCK))
        return carry

    lax.fori_loop(0, nb // (ROUTE_CLASSES * GROUP), route, 0)

    def colmax(s):
        return jnp.max(s, axis=0, keepdims=True)

    def colsum(p):
        return jnp.sum(p, axis=0, keepdims=True)

    def pv(g, first, p):
        keys = slice(first * BLOCK, first * BLOCK + p.shape[0])
        return dot(vt_ref[g, :, keys], p.astype(BF16))

    def do_class(c, carry):
        def far_logits(g, t):
            return dot(kaug_ref[g], qat_ref[c * (GROUP // QPAIR) + t])

        def far_park(g, t, s_ref):
            s = far_logits(g, t)
            s_ref[...] = s
            return colmax(s)

        lanes_of = [slice((j % QPAIR) * BLOCK, (j % QPAIR + 1) * BLOCK) for j in range(GROUP)]
        near = []
        for j in range(GROUP):
            qb = c * GROUP + j
            pair = c * (GROUP // QPAIR) + j // QPAIR
            q_t = qat_ref[pair, :HEAD_DIM, lanes_of[j]]
            parts = [(dot(kaug_ref[c, j * BLOCK:(j + 1) * BLOCK, :HEAD_DIM], q_t) + bown_ref[...], c, j)]
            if qb > 0:
                g_adj, t_adj = (c, j - 1) if j > 0 else (c - 1, GROUP - 1)
                k_adj = kaug_ref[g_adj, t_adj * BLOCK:(t_adj + 1) * BLOCK, :HEAD_DIM]
                parts.append((dot(k_adj, q_t) + (badj_ref[...] + pen_ref[qb, 0:1, :]), g_adj, t_adj))
            if j >= 2:
                parts.append((dot(kaug_ref[c, :(j - 1) * BLOCK, :], qat_ref[pair, :, lanes_of[j]]), c, 0))
            near.append(parts)

        if c > 0:
            cmax_a = far_park(0, 0, sa_ref)

        for j, parts in enumerate(near):
            m = functools.reduce(jnp.maximum, [colmax(s) for s, _, _ in parts])
            probs = [(jnp.exp2(s - m), g, first) for s, g, first in parts]
            m_ref[j // QPAIR, :, lanes_of[j]] = m
            l_ref[j // QPAIR, :, lanes_of[j]] = functools.reduce(jnp.add, [colsum(p) for p, _, _ in probs])
            acc_ref[j // QPAIR, :, lanes_of[j]] = functools.reduce(
                jnp.add, [pv(g, first, p) for p, g, first in probs])

        def far_update(g, t, s_ref, cmax):
            m_old = m_ref[t]
            m_new = jnp.maximum(m_old, cmax)
            alpha = jnp.exp2(m_old - m_new)
            l_new = alpha * l_ref[t]
            acc_new = alpha * acc_ref[t]
            for k in range(GROUP):
                p = jnp.exp2(s_ref[k * BLOCK:(k + 1) * BLOCK, :] - m_new)
                l_new = l_new + colsum(p)
                acc_new = acc_new + pv(g, k, p)
            l_ref[t] = l_new
            acc_ref[t] = acc_new
            m_ref[t] = m_new

        for g in range(c):
            cmax_b = far_park(g, 1, sb_ref)
            far_update(g, 0, sa_ref, cmax_a)
            if g + 1 < c:
                cmax_a = far_park(g + 1, 0, sa_ref)
            far_update(g, 1, sb_ref, cmax_b)

        for j in range(GROUP):
            rows = pl.ds((c * GROUP + j) * BLOCK, BLOCK)
            out_t = acc_ref[j // QPAIR, :, lanes_of[j]] * (1.0 / l_ref[j // QPAIR, :, lanes_of[j]])
            o_ref[rows, :] = (out_t.T * sz_ref[rows, :]).astype(o_ref.dtype)
        return carry

    one = jnp.minimum(h + 1, 1)
    for c in range(nb // GROUP):
        lax.fori_loop(0, one, lambda _, carry, c=c: do_class(c, carry), 0)


def _moba_attention(q, ks, v, kmean, sz, rel_bias):
    s, attn_w = q.shape
    nh = attn_w // HEAD_DIM
    nb = s // BLOCK
    assert nb % GROUP == 0 and nb <= HEAD_DIM
    idx_own, idx_adj = _bucket_tables()
    head_cols = pl.BlockSpec((s, HEAD_DIM), lambda h: (0, h))
    const_tile = pl.BlockSpec((BLOCK, BLOCK), lambda h: (0, 0))
    return pl.pallas_call(
        functools.partial(_attn_kernel, nb=nb),
        grid=(nh,),
        in_specs=[pl.BlockSpec(memory_space=pltpu.SMEM),
                  const_tile, const_tile,
                  head_cols, head_cols, head_cols,
                  pl.BlockSpec((nb, HEAD_DIM), lambda h: (0, h)),
                  head_cols],
        out_specs=head_cols,
        out_shape=jax.ShapeDtypeStruct((s, attn_w), BF16),
        scratch_shapes=[pltpu.VMEM((nb // GROUP, GROUP * BLOCK, 2 * HEAD_DIM), BF16),
                        pltpu.VMEM((nb // GROUP, HEAD_DIM, GROUP * BLOCK), BF16),
                        pltpu.VMEM((BLOCK, BLOCK), F32),
                        pltpu.VMEM((BLOCK, BLOCK), F32),
                        pltpu.VMEM((nb // QPAIR, 2 * HEAD_DIM, QPAIR * BLOCK), BF16),
                        pltpu.VMEM((nb, 8, BLOCK), F32),
                        pltpu.VMEM((GROUP // QPAIR, 1, QPAIR * BLOCK), F32),
                        pltpu.VMEM((GROUP // QPAIR, 1, QPAIR * BLOCK), F32),
                        pltpu.VMEM((GROUP // QPAIR, HEAD_DIM, QPAIR * BLOCK), F32),
                        pltpu.VMEM((GROUP * BLOCK, QPAIR * BLOCK), F32),
                        pltpu.VMEM((GROUP * BLOCK, QPAIR * BLOCK), F32)],
        compiler_params=pltpu.CompilerParams(
            dimension_semantics=("arbitrary",),
            vmem_limit_bytes=V7X_VMEM_LIMIT_BYTES),
        name="moba_attn",
    )(rel_bias, jnp.asarray(idx_own), jnp.asarray(idx_adj), q, ks, v, kmean, sz)


def _out_kernel(a_ref, c_ref, wa_ref, wc_ref, x_ref, g_ref, o_ref, *, n_chunks):
    d = o_ref.shape[1]
    w = d // n_chunks
    a = a_ref[...]
    c = c_ref[...]
    hs = []
    ss = jnp.zeros((o_ref.shape[0], 1), F32)
    for n in range(n_chunks):
        cols = slice(n * w, (n + 1) * w)
        h = (x_ref[:, cols]
             + jnp.dot(a, wa_ref[:, cols], preferred_element_type=F32)
             + jnp.dot(c, wc_ref[:, cols], preferred_element_type=F32))
        ss = ss + jnp.sum(h * h, axis=-1, keepdims=True)
        hs.append(h)
    inv = lax.rsqrt(ss / d + EPS)
    for n, h in enumerate(hs):
        cols = slice(n * w, (n + 1) * w)
        o_ref[:, cols] = h * inv * g_ref[:, cols]


def _out_proj(attn, conv, w_out, x, g, tm=256):
    s, d = x.shape
    ka = attn.shape[1]
    kc = conv.shape[1]
    assert ka == kc
    resident = pl.Buffered(1)
    return pl.pallas_call(
        functools.partial(_out_kernel, n_chunks=4),
        grid=(s // tm,),
        in_specs=[pl.BlockSpec((tm, ka), lambda i: (i, 0)),
                  pl.BlockSpec((tm, kc), lambda i: (i, 0)),
                  pl.BlockSpec((ka, d), lambda i: (0, 0), pipeline_mode=resident),
                  pl.BlockSpec((kc, d), lambda i: (1, 0), pipeline_mode=resident),
                  pl.BlockSpec((tm, d), lambda i: (i, 0)),
                  pl.BlockSpec((1, d), lambda i: (0, 0))],
        out_specs=pl.BlockSpec((tm, d), lambda i: (i, 0)),
        out_shape=jax.ShapeDtypeStruct((s, d), F32),
        compiler_params=pltpu.CompilerParams(
            dimension_semantics=("arbitrary",),
            vmem_limit_bytes=V7X_OUT_PROJ_VMEM_LIMIT_BYTES),
        name="out_proj_norm",
    )(attn, conv, w_out, w_out, x, g.reshape(1, d))


def kernel(x, norm_gain, w_in, conv_w, w_out, rel_bias, final_gain):
    b, s, d = x.shape
    depth = norm_gain.shape[0]
    assert b == 1 and depth == 1, "kernel is written for one sequence and one layer"
    attn_w = d // 2
    conv_width = d - attn_w
    assert rel_bias.shape == (NUM_BUCKETS, attn_w // HEAD_DIM)
    assert w_in.shape[2] == 4 * attn_w + 4 * conv_width

    x2 = x[0]
    assert attn_w == conv_width
    pair_w = 2 * attn_w
    w_qk_b = w_in[0, :, :pair_w].astype(BF16)

    u = _rmsnorm_bf16(x2, norm_gain[0])
    tm, tn = 1024, 512
    tile = pl.BlockSpec((tm, tn), lambda j, i: (i, j))
    act = jax.ShapeDtypeStruct((s, attn_w), BF16)
    q, ks, kmean, w_vz_b, w_hb_b = _proj_pair(
        functools.partial(_proj_qk_kernel, scale=HEAD_DIM ** -0.5 * LOG2E), "proj_qk", u, w_qk_b, attn_w,
        [tile, tile, pl.BlockSpec((1, tm // BLOCK, tn), lambda j, i: (i, 0, j))],
        [act, act, jax.ShapeDtypeStruct((s // tm, tm // BLOCK, attn_w), F32)],
        [(w_in[0], 1, pair_w), (w_in[0], 2, pair_w)], tm, tn)
    v, sz, w_cz_b = _proj_pair(
        _proj_vz_kernel, "proj_vz", u, w_vz_b, attn_w,
        [tile, tile], [act, jax.ShapeDtypeStruct((s, attn_w), F32)],
        [(w_in[0], 3, pair_w)], tm, tn)
    conv, w_out_b = _proj_conv(u, w_hb_b, w_cz_b, conv_w[0], conv_width, w_out[0])
    attn = _moba_attention(q, ks, v, kmean.reshape(s // BLOCK, attn_w), sz, rel_bias)
    out = _out_proj(attn, conv, w_out_b, x2, final_gain)
    return out[None]
```

```python
import functools
import math

import numpy as np
import jax
import jax.numpy as jnp
from jax import lax
from jax.experimental import pallas as pl
from jax.experimental.pallas import tpu as pltpu

HEAD_DIM = 128
BLOCK = 256
TOPK = 3
NUM_BUCKETS = 32
MAX_DISTANCE = 128
EPS = 1e-6
NEG = -1e30
LOG2E = math.log2(math.e)

F32 = jnp.float32
BF16 = jnp.bfloat16

V7X_VMEM_LIMIT_BYTES = 56 * 1024 * 1024
V7X_OUT_PROJ_VMEM_LIMIT_BYTES = 60 * 1024 * 1024


def _silu(z):
    return z * (1.0 / (1.0 + jnp.exp(-z)))


def _rmsnorm_kernel(x_ref, g_ref, o_ref):
    x = x_ref[...]
    ms = jnp.mean(x * x, axis=-1, keepdims=True)
    o_ref[...] = (x * lax.rsqrt(ms + EPS) * g_ref[...]).astype(o_ref.dtype)


def _rmsnorm_bf16(x, g, tm=512):
    s, d = x.shape
    return pl.pallas_call(
        _rmsnorm_kernel,
        grid=(s // tm,),
        in_specs=[pl.BlockSpec((tm, d), lambda i: (i, 0)),
                  pl.BlockSpec((1, d), lambda i: (0, 0))],
        out_specs=pl.BlockSpec((tm, d), lambda i: (i, 0)),
        out_shape=jax.ShapeDtypeStruct((s, d), BF16),
        compiler_params=pltpu.CompilerParams(dimension_semantics=("arbitrary",)),
        name="rmsnorm_in",
    )(x, g.reshape(1, d))


def _run_side_jobs(side_refs, side_out_refs):
    for src, dst in zip(side_refs, side_out_refs):
        dst[...] = src[...].astype(dst.dtype)


def _side_cast_specs(side, col_block, cols, n_steps, step_of):
    rows = side.shape[0]
    chunk = rows // n_steps
    assert chunk * n_steps == rows and chunk % 16 == 0 and side.shape[1] % cols == 0
    return (pl.BlockSpec((chunk, cols), lambda j, i: (step_of(j, i), col_block)),
            pl.BlockSpec((chunk, cols), lambda j, i: (step_of(j, i), 0)),
            jax.ShapeDtypeStruct((rows, cols), BF16))


def _proj_qk_kernel(u_ref, wq_ref, wk_ref, *refs, scale, n_side):
    side_refs, (q_ref, ks_ref, km_ref), side_out_refs = refs[:n_side], refs[n_side:n_side + 3], refs[n_side + 3:]
    _run_side_jobs(side_refs, side_out_refs)
    u = u_ref[...]
    tm = u.shape[0]
    k = jnp.dot(u, wk_ref[...], preferred_element_type=F32)
    ks_ref[...] = (k * scale).astype(BF16)
    km_ref[0] = jnp.mean(k.reshape(tm // BLOCK, BLOCK, k.shape[1]), axis=1)
    q = jnp.dot(u, wq_ref[...], preferred_element_type=F32)
    q_ref[...] = q.astype(BF16)


def _proj_vz_kernel(u_ref, wv_ref, wz_ref, *refs, n_side):
    side_refs, (v_ref, sz_ref), side_out_refs = refs[:n_side], refs[n_side:n_side + 2], refs[n_side + 2:]
    _run_side_jobs(side_refs, side_out_refs)
    u = u_ref[...]
    z = jnp.dot(u, wz_ref[...], preferred_element_type=F32)
    sz_ref[...] = _silu(z)
    v = jnp.dot(u, wv_ref[...], preferred_element_type=F32)
    v_ref[...] = v.astype(BF16)


def _proj_pair(kernel_fn, name, u, w, width, out_specs, out_shape, side_jobs, tm, tn):
    s, d = u.shape
    nj = width // tn
    ni = s // tm
    jobs = [_side_cast_specs(m, cb, cols, nj * ni, lambda j, i: j * ni + i) for m, cb, cols in side_jobs]
    w_spec = lambda part: pl.BlockSpec((d, tn), lambda j, i, part=part: (0, part * nj + j))
    return pl.pallas_call(
        functools.partial(kernel_fn, n_side=len(jobs)),
        grid=(nj, ni),
        in_specs=[pl.BlockSpec((tm, d), lambda j, i: (i, 0)), w_spec(0), w_spec(1)] + [jb[0] for jb in jobs],
        out_specs=out_specs + [jb[1] for jb in jobs],
        out_shape=out_shape + [jb[2] for jb in jobs],
        compiler_params=pltpu.CompilerParams(
            dimension_semantics=("arbitrary", "arbitrary"),
            vmem_limit_bytes=V7X_VMEM_LIMIT_BYTES),
        name=name,
    )(u, w, w, *[m for m, _, _ in side_jobs])


def _proj_conv_kernel(u_ref, wh_ref, wb_ref, wc_ref, wz_ref, cw_ref, side_ref,
                      o_ref, side_out_ref, carry_ref):
    _run_side_jobs([side_ref], [side_out_ref])
    @pl.when(pl.program_id(1) == 0)
    def _():
        carry_ref[...] = jnp.zeros_like(carry_ref)

    u = u_ref[...]
    tm = u.shape[0]
    hc = jnp.dot(u, wh_ref[...], preferred_element_type=F32)
    c = jnp.dot(u, wc_ref[...], preferred_element_type=F32)
    p = c * hc
    rows = lax.broadcasted_iota(jnp.int32, p.shape, 0)
    prev = carry_ref[...]
    prev1 = prev[7:8, :]
    prev2 = prev[6:7, :]
    p1 = jnp.where(rows == 0, prev1, pltpu.roll(p, 1, 0))
    p2 = jnp.where(rows == 0, prev2, jnp.where(rows == 1, prev1, pltpu.roll(p, 2, 0)))
    carry_ref[...] = p[tm - 8:, :]
    cw = cw_ref[...]
    y = p2 * cw[0:1, :] + p1 * cw[1:2, :] + p * cw[2:3, :]
    z = jnp.dot(u, wz_ref[...], preferred_element_type=F32)
    gate = _silu(z)
    b = jnp.dot(u, wb_ref[...], preferred_element_type=F32)
    o_ref[...] = (b * y * gate).astype(o_ref.dtype)


def _proj_conv(u, w_hb, w_cz, conv_w, conv_width, side, tm=1024, tn=256):
    s, d = u.shape
    nj = conv_width // tn
    ni = s // tm
    side_in, side_out, side_shape = _side_cast_specs(
        side, 0, side.shape[1], nj * ni, lambda j, i: j * ni + i)
    w_spec = lambda part: pl.BlockSpec((d, tn), lambda j, i, part=part: (0, part * nj + j))
    return pl.pallas_call(
        _proj_conv_kernel,
        grid=(nj, ni),
        in_specs=[pl.BlockSpec((tm, d), lambda j, i: (i, 0)),
                  w_spec(0), w_spec(1), w_spec(0), w_spec(1),
                  pl.BlockSpec((conv_w.shape[0], tn), lambda j, i: (0, j)), side_in],
        out_specs=[pl.BlockSpec((tm, tn), lambda j, i: (i, j)), side_out],
        out_shape=[jax.ShapeDtypeStruct((s, conv_width), BF16), side_shape],
        scratch_shapes=[pltpu.VMEM((8, tn), F32)],
        compiler_params=pltpu.CompilerParams(
            dimension_semantics=("arbitrary", "arbitrary"),
            vmem_limit_bytes=V7X_VMEM_LIMIT_BYTES),
        name="proj_conv",
    )(u, w_hb, w_hb, w_cz, w_cz, conv_w, side)


def _rel_bucket_np(dist):
    n = np.maximum(dist, 0)
    max_exact = NUM_BUCKETS // 2
    nf = np.maximum(n, 1).astype(np.float32)
    ratio = np.log(nf / np.float32(max_exact)) / np.float32(math.log(MAX_DISTANCE / max_exact))
    large = max_exact + (ratio * np.float32(NUM_BUCKETS - max_exact)).astype(np.int32)
    large = np.minimum(large, NUM_BUCKETS - 1)
    return np.where(n < max_exact, n, large).astype(np.int32)


def _bucket_tables():
    key = np.arange(BLOCK)[:, None]
    qry = np.arange(BLOCK)[None, :]
    d_own = qry - key
    own = np.where(d_own >= 0, _rel_bucket_np(d_own), NUM_BUCKETS).astype(np.int32)
    adj = _rel_bucket_np(BLOCK + qry - key)
    return own, adj


GROUP = 4
QPAIR = 2


def _attn_kernel(relb_ref, idx_own_ref, idx_adj_ref, q_ref, ks_ref, v_ref, km_ref, sz_ref,
                 o_ref,
                 kaug_ref, vt_ref, bown_ref, badj_ref, qat_ref, pen_ref, m_ref, l_ref, acc_ref,
                 sa_ref, sb_ref, *, nb):
    h = pl.program_id(0)
    gk = GROUP * BLOCK
    pw = QPAIR * BLOCK
    dot = functools.partial(jnp.dot, preferred_element_type=F32)

    @pl.when(h == 0)
    def _():
        lane = lax.broadcasted_iota(jnp.int32, (BLOCK, HEAD_DIM), 1)
        for blk in range(nb):
            r0 = (blk % GROUP) * BLOCK
            kaug_ref[blk // GROUP, r0:r0 + BLOCK, HEAD_DIM:] = jnp.where(lane == blk, 1.0, 0.0).astype(BF16)

    far_bias = relb_ref[NUM_BUCKETS - 1, h]
    io = idx_own_ref[...]
    ia = idx_adj_ref[...]
    own = jnp.full((BLOCK, BLOCK), NEG, F32)
    adj = jnp.zeros((BLOCK, BLOCK), F32)
    for b in range(NUM_BUCKETS):
        t = (relb_ref[b, h] - far_bias) * LOG2E
        own = jnp.where(io == b, t, own)
        adj = jnp.where(ia == b, t, adj)
    bown_ref[...] = own
    badj_ref[...] = adj

    def stage(g, carry):
        rows = pl.ds(pl.multiple_of(g * gk, gk), gk)
        kaug_ref[g, :, :HEAD_DIM] = ks_ref[rows, :]
        vt_ref[g] = v_ref[rows, :].astype(F32).T.astype(BF16)
        return carry

    lax.fori_loop(0, nb // GROUP, stage, 0)

    km = km_ref[...]
    km_hi = km.astype(BF16)
    km_lo = (km - km_hi.astype(F32)).astype(BF16)

    ROUTE_CLASSES = 4
    rq = ROUTE_CLASSES * gk

    def route(cc, carry):
        rows = pl.ds(pl.multiple_of(cc * rq, rq), rq)
        q_t = q_ref[rows, :].astype(F32).T.astype(BF16)
        gate = dot(km_hi, q_t) + dot(km_lo, q_t)
        blk_id = lax.broadcasted_iota(jnp.int32, gate.shape, 0)
        qb = cc * (ROUTE_CLASSES * GROUP) + lax.shift_right_logical(
            lax.broadcasted_iota(jnp.int32, gate.shape, 1), int(math.log2(BLOCK)))
        gate = jnp.where(blk_id < qb, gate, -jnp.inf)
        sel = jnp.zeros(gate.shape, jnp.bool_)
        for r in range(min(TOPK, nb)):
            mx = jnp.max(gate, axis=0, keepdims=True)
            first = jnp.min(jnp.where(gate == mx, blk_id, nb), axis=0, keepdims=True)
            pick = (blk_id == first) & (qb > r)
            sel = sel | pick
            gate = jnp.where(pick, -jnp.inf, gate)
        sel_far = jnp.where(sel & (blk_id < qb - 1), 0.0, NEG)
        sel_rows = jnp.concatenate(
            [sel_far, jnp.zeros((HEAD_DIM - nb, rq), F32)], axis=0).astype(BF16)
        adj_sel = jnp.max(jnp.where(sel & (blk_id == qb - 1), 1.0, 0.0), axis=0, keepdims=True)
        pen = jnp.where(adj_sel > 0.0, 0.0, NEG)
        for t in range(rq // pw):
            cols = slice(t * pw, (t + 1) * pw)
            qat_ref[cc * (rq // pw) + t, :HEAD_DIM, :] = q_t[:, cols]
            qat_ref[cc * (rq // pw) + t, HEAD_DIM:, :] = sel_rows[:, cols]
        for j in range(rq // BLOCK):
            cols = slice(j * BLOCK, (j + 1) * BLOCK)
            pen_ref[cc * (rq // BLOCK) + j] = jnp.broadcast_to(pen[:, cols], (8, BLOCK))
        return carry

    lax.fori_loop(0, nb // (ROUTE_CLASSES * GROUP), route, 0)

    def colmax(s):
        return jnp.max(s, axis=0, keepdims=True)

    def colsum(p):
        return jnp.sum(p, axis=0, keepdims=True)

    def pv(g, first, p):
        keys = slice(first * BLOCK, first * BLOCK + p.shape[0])
        return dot(vt_ref[g, :, keys], p.astype(BF16))

    def do_class(c, carry):
        def far_logits(g, t):
            return dot(kaug_ref[g], qat_ref[c * (GROUP // QPAIR) + t])

        def far_park(g, t, s_ref):
            s = far_logits(g, t)
            s_ref[...] = s
            return colmax(s)

        lanes_of = [slice((j % QPAIR) * BLOCK, (j % QPAIR + 1) * BLOCK) for j in range(GROUP)]
        near = []
        for j in range(GROUP):
            qb = c * GROUP + j
            pair = c * (GROUP // QPAIR) + j // QPAIR
            q_t = qat_ref[pair, :HEAD_DIM, lanes_of[j]]
            parts = [(dot(kaug_ref[c, j * BLOCK:(j + 1) * BLOCK, :HEAD_DIM], q_t) + bown_ref[...], c, j)]
            if qb > 0:
                g_adj, t_adj = (c, j - 1) if j > 0 else (c - 1, GROUP - 1)
                k_adj = kaug_ref[g_adj, t_adj * BLOCK:(t_adj + 1) * BLOCK, :HEAD_DIM]
                parts.append((dot(k_adj, q_t) + (badj_ref[...] + pen_ref[qb, 0:1, :]), g_adj, t_adj))
            if j >= 2:
                parts.append((dot(kaug_ref[c, :(j - 1) * BLOCK, :], qat_ref[pair, :, lanes_of[j]]), c, 0))
            near.append(parts)

        if c > 0:
            cmax_a = far_park(0, 0, sa_ref)

        for j, parts in enumerate(near):
            m = functools.reduce(jnp.maximum, [colmax(s) for s, _, _ in parts])
            probs = [(jnp.exp2(s - m), g, first) for s, g, first in parts]
            m_ref[j // QPAIR, :, lanes_of[j]] = m
            l_ref[j // QPAIR, :, lanes_of[j]] = functools.reduce(jnp.add, [colsum(p) for p, _, _ in probs])
            acc_ref[j // QPAIR, :, lanes_of[j]] = functools.reduce(
                jnp.add, [pv(g, first, p) for p, g, first in probs])

        def far_update(g, t, s_ref, cmax):
            m_old = m_ref[t]
            m_new = jnp.maximum(m_old, cmax)
            alpha = jnp.exp2(m_old - m_new)
            l_new = alpha * l_ref[t]
            acc_new = alpha * acc_ref[t]
            for k in range(GROUP):
                p = jnp.exp2(s_ref[k * BLOCK:(k + 1) * BLOCK, :] - m_new)
                l_new = l_new + colsum(p)
                acc_new = acc_new + pv(g, k, p)
            l_ref[t] = l_new
            acc_ref[t] = acc_new
            m_ref[t] = m_new

        for g in range(c):
            cmax_b = far_park(g, 1, sb_ref)
            far_update(g, 0, sa_ref, cmax_a)
            if g + 1 < c:
                cmax_a = far_park(g + 1, 0, sa_ref)
            far_update(g, 1, sb_ref, cmax_b)

        for j in range(GROUP):
            rows = pl.ds((c * GROUP + j) * BLOCK, BLOCK)
            out_t = acc_ref[j // QPAIR, :, lanes_of[j]] * (1.0 / l_ref[j // QPAIR, :, lanes_of[j]])
            o_ref[rows, :] = (out_t.T * sz_ref[rows, :]).astype(o_ref.dtype)
        return carry

    one = jnp.minimum(h + 1, 1)
    for c in range(nb // GROUP):
        lax.fori_loop(0, one, lambda _, carry, c=c: do_class(c, carry), 0)


def _moba_attention(q, ks, v, kmean, sz, rel_bias):
    s, attn_w = q.shape
    nh = attn_w // HEAD_DIM
    nb = s // BLOCK
    assert nb % GROUP == 0 and nb <= HEAD_DIM
    idx_own, idx_adj = _bucket_tables()
    head_cols = pl.BlockSpec((s, HEAD_DIM), lambda h: (0, h))
    const_tile = pl.BlockSpec((BLOCK, BLOCK), lambda h: (0, 0))
    return pl.pallas_call(
        functools.partial(_attn_kernel, nb=nb),
        grid=(nh,),
        in_specs=[pl.BlockSpec(memory_space=pltpu.SMEM),
                  const_tile, const_tile,
                  head_cols, head_cols, head_cols,
                  pl.BlockSpec((nb, HEAD_DIM), lambda h: (0, h)),
                  head_cols],
        out_specs=head_cols,
        out_shape=jax.ShapeDtypeStruct((s, attn_w), BF16),
        scratch_shapes=[pltpu.VMEM((nb // GROUP, GROUP * BLOCK, 2 * HEAD_DIM), BF16),
                        pltpu.VMEM((nb // GROUP, HEAD_DIM, GROUP * BLOCK), BF16),
                        pltpu.VMEM((BLOCK, BLOCK), F32),
                        pltpu.VMEM((BLOCK, BLOCK), F32),
                        pltpu.VMEM((nb // QPAIR, 2 * HEAD_DIM, QPAIR * BLOCK), BF16),
                        pltpu.VMEM((nb, 8, BLOCK), F32),
                        pltpu.VMEM((GROUP // QPAIR, 1, QPAIR * BLOCK), F32),
                        pltpu.VMEM((GROUP // QPAIR, 1, QPAIR * BLOCK), F32),
                        pltpu.VMEM((GROUP // QPAIR, HEAD_DIM, QPAIR * BLOCK), F32),
                        pltpu.VMEM((GROUP * BLOCK, QPAIR * BLOCK), F32),
                        pltpu.VMEM((GROUP * BLOCK, QPAIR * BLOCK), F32)],
        compiler_params=pltpu.CompilerParams(
            dimension_semantics=("arbitrary",),
            vmem_limit_bytes=V7X_VMEM_LIMIT_BYTES),
        name="moba_attn",
    )(rel_bias, jnp.asarray(idx_own), jnp.asarray(idx_adj), q, ks, v, kmean, sz)


def _out_kernel(a_ref, c_ref, wa_ref, wc_ref, x_ref, g_ref, o_ref, *, n_chunks):
    d = o_ref.shape[1]
    w = d // n_chunks
    a = a_ref[...]
    c = c_ref[...]
    hs = []
    ss = jnp.zeros((o_ref.shape[0], 1), F32)
    for n in range(n_chunks):
        cols = slice(n * w, (n + 1) * w)
        h = (x_ref[:, cols]
             + jnp.dot(a, wa_ref[:, cols], preferred_element_type=F32)
             + jnp.dot(c, wc_ref[:, cols], preferred_element_type=F32))
        ss = ss + jnp.sum(h * h, axis=-1, keepdims=True)
        hs.append(h)
    inv = lax.rsqrt(ss / d + EPS)
    for n, h in enumerate(hs):
        cols = slice(n * w, (n + 1) * w)
        o_ref[:, cols] = h * inv * g_ref[:, cols]


def _out_proj(attn, conv, w_out, x, g, tm=256):
    s, d = x.shape
    ka = attn.shape[1]
    kc = conv.shape[1]
    assert ka == kc
    resident = pl.Buffered(1)
    return pl.pallas_call(
        functools.partial(_out_kernel, n_chunks=4),
        grid=(s // tm,),
        in_specs=[pl.BlockSpec((tm, ka), lambda i: (i, 0)),
                  pl.BlockSpec((tm, kc), lambda i: (i, 0)),
                  pl.BlockSpec((ka, d), lambda i: (0, 0), pipeline_mode=resident),
                  pl.BlockSpec((kc, d), lambda i: (1, 0), pipeline_mode=resident),
                  pl.BlockSpec((tm, d), lambda i: (i, 0)),
                  pl.BlockSpec((1, d), lambda i: (0, 0))],
        out_specs=pl.BlockSpec((tm, d), lambda i: (i, 0)),
        out_shape=jax.ShapeDtypeStruct((s, d), F32),
        compiler_params=pltpu.CompilerParams(
            dimension_semantics=("arbitrary",),
            vmem_limit_bytes=V7X_OUT_PROJ_VMEM_LIMIT_BYTES),
        name="out_proj_norm",
    )(attn, conv, w_out, w_out, x, g.reshape(1, d))


def kernel(x, norm_gain, w_in, conv_w, w_out, rel_bias, final_gain):
    b, s, d = x.shape
    depth = norm_gain.shape[0]
    assert b == 1 and depth == 1, "kernel is written for one sequence and one layer"
    attn_w = d // 2
    conv_width = d - attn_w
    assert rel_bias.shape == (NUM_BUCKETS, attn_w // HEAD_DIM)
    assert w_in.shape[2] == 4 * attn_w + 4 * conv_width

    x2 = x[0]
    assert attn_w == conv_width
    pair_w = 2 * attn_w
    w_qk_b = w_in[0, :, :pair_w].astype(BF16)

    u = _rmsnorm_bf16(x2, norm_gain[0])
    tm, tn = 1024, 512
    tile = pl.BlockSpec((tm, tn), lambda j, i: (i, j))
    act = jax.ShapeDtypeStruct((s, attn_w), BF16)
    q, ks, kmean, w_vz_b, w_hb_b = _proj_pair(
        functools.partial(_proj_qk_kernel, scale=HEAD_DIM ** -0.5 * LOG2E), "proj_qk", u, w_qk_b, attn_w,
        [tile, tile, pl.BlockSpec((1, tm // BLOCK, tn), lambda j, i: (i, 0, j))],
        [act, act, jax.ShapeDtypeStruct((s // tm, tm // BLOCK, attn_w), F32)],
        [(w_in[0], 1, pair_w), (w_in[0], 2, pair_w)], tm, tn)
    v, sz, w_cz_b = _proj_pair(
        _proj_vz_kernel, "proj_vz", u, w_vz_b, attn_w,
        [tile, tile], [act, jax.ShapeDtypeStruct((s, attn_w), F32)],
        [(w_in[0], 3, pair_w)], tm, tn)
    conv, w_out_b = _proj_conv(u, w_hb_b, w_cz_b, conv_w[0], conv_width, w_out[0])
    attn = _moba_attention(q, ks, v, kmean.reshape(s // BLOCK, attn_w), sz, rel_bias)
    out = _out_proj(attn, conv, w_out_b, x2, final_gain)
    return out[None]
```

```python
import functools
import math

import numpy as np
import jax
import jax.numpy as jnp
from jax import lax
from jax.experimental import pallas as pl
from jax.experimental.pallas import tpu as pltpu

HEAD_DIM = 128
BLOCK = 256
TOPK = 3
NUM_BUCKETS = 32
MAX_DISTANCE = 128
EPS = 1e-6
NEG = -1e30
LOG2E = math.log2(math.e)

F32 = jnp.float32
BF16 = jnp.bfloat16

V7X_VMEM_LIMIT_BYTES = 56 * 1024 * 1024
V7X_OUT_PROJ_VMEM_LIMIT_BYTES = 60 * 1024 * 1024


def _silu(z):
    return z * (1.0 / (1.0 + jnp.exp(-z)))


def _rmsnorm_kernel(x_ref, g_ref, o_ref):
    x = x_ref[...]
    ms = jnp.mean(x * x, axis=-1, keepdims=True)
    o_ref[...] = (x * lax.rsqrt(ms + EPS) * g_ref[...]).astype(o_ref.dtype)


def _rmsnorm_bf16(x, g, tm=512):
    s, d = x.shape
    return pl.pallas_call(
        _rmsnorm_kernel,
        grid=(s // tm,),
        in_specs=[pl.BlockSpec((tm, d), lambda i: (i, 0)),
                  pl.BlockSpec((1, d), lambda i: (0, 0))],
        out_specs=pl.BlockSpec((tm, d), lambda i: (i, 0)),
        out_shape=jax.ShapeDtypeStruct((s, d), BF16),
        compiler_params=pltpu.CompilerParams(dimension_semantics=("arbitrary",)),
        name="rmsnorm_in",
    )(x, g.reshape(1, d))


def _run_side_jobs(side_refs, side_out_refs):
    for src, dst in zip(side_refs, side_out_refs):
        dst[...] = src[...].astype(dst.dtype)


def _side_cast_specs(side, col_block, cols, n_steps, step_of):
    rows = side.shape[0]
    chunk = rows // n_steps
    assert chunk * n_steps == rows and chunk % 16 == 0 and side.shape[1] % cols == 0
    return (pl.BlockSpec((chunk, cols), lambda j, i: (step_of(j, i), col_block)),
            pl.BlockSpec((chunk, cols), lambda j, i: (step_of(j, i), 0)),
            jax.ShapeDtypeStruct((rows, cols), BF16))


def _proj_qk_kernel(u_ref, wq_ref, wk_ref, *refs, scale, n_side):
    side_refs, (q_ref, ks_ref, km_ref), side_out_refs = refs[:n_side], refs[n_side:n_side + 3], refs[n_side + 3:]
    _run_side_jobs(side_refs, side_out_refs)
    u = u_ref[...]
    tm = u.shape[0]
    k = jnp.dot(u, wk_ref[...], preferred_element_type=F32)
    ks_ref[...] = (k * scale).astype(BF16)
    km_ref[0] = jnp.mean(k.reshape(tm // BLOCK, BLOCK, k.shape[1]), axis=1)
    q = jnp.dot(u, wq_ref[...], preferred_element_type=F32)
    q_ref[...] = q.astype(BF16)


def _proj_vz_kernel(u_ref, wv_ref, wz_ref, *refs, n_side):
    side_refs, (v_ref, sz_ref), side_out_refs = refs[:n_side], refs[n_side:n_side + 2], refs[n_side + 2:]
    _run_side_jobs(side_refs, side_out_refs)
    u = u_ref[...]
    z = jnp.dot(u, wz_ref[...], preferred_element_type=F32)
    sz_ref[...] = _silu(z)
    v = jnp.dot(u, wv_ref[...], preferred_element_type=F32)
    v_ref[...] = v.T.astype(BF16)


def _proj_pair(kernel_fn, name, u, w, width, out_specs, out_shape, side_jobs, tm, tn):
    s, d = u.shape
    nj = width // tn
    ni = s // tm
    jobs = [_side_cast_specs(m, cb, cols, nj * ni, lambda j, i: j * ni + i) for m, cb, cols in side_jobs]
    w_spec = lambda part: pl.BlockSpec((d, tn), lambda j, i, part=part: (0, part * nj + j))
    return pl.pallas_call(
        functools.partial(kernel_fn, n_side=len(jobs)),
        grid=(nj, ni),
        in_specs=[pl.BlockSpec((tm, d), lambda j, i: (i, 0)), w_spec(0), w_spec(1)] + [jb[0] for jb in jobs],
        out_specs=out_specs + [jb[1] for jb in jobs],
        out_shape=out_shape + [jb[2] for jb in jobs],
        compiler_params=pltpu.CompilerParams(
            dimension_semantics=("arbitrary", "arbitrary"),
            vmem_limit_bytes=V7X_VMEM_LIMIT_BYTES),
        name=name,
    )(u, w, w, *[m for m, _, _ in side_jobs])


def _proj_conv_kernel(u_ref, wh_ref, wb_ref, wc_ref, wz_ref, cw_ref, side_ref,
                      o_ref, side_out_ref, carry_ref):
    _run_side_jobs([side_ref], [side_out_ref])
    @pl.when(pl.program_id(1) == 0)
    def _():
        carry_ref[...] = jnp.zeros_like(carry_ref)

    u = u_ref[...]
    tm = u.shape[0]
    hc = jnp.dot(u, wh_ref[...], preferred_element_type=F32)
    c = jnp.dot(u, wc_ref[...], preferred_element_type=F32)
    p = c * hc
    rows = lax.broadcasted_iota(jnp.int32, p.shape, 0)
    prev = carry_ref[...]
    prev1 = prev[7:8, :]
    prev2 = prev[6:7, :]
    p1 = jnp.where(rows == 0, prev1, pltpu.roll(p, 1, 0))
    p2 = jnp.where(rows == 0, prev2, jnp.where(rows == 1, prev1, pltpu.roll(p, 2, 0)))
    carry_ref[...] = p[tm - 8:, :]
    cw = cw_ref[...]
    y = p2 * cw[0:1, :] + p1 * cw[1:2, :] + p * cw[2:3, :]
    z = jnp.dot(u, wz_ref[...], preferred_element_type=F32)
    gate = _silu(z)
    b = jnp.dot(u, wb_ref[...], preferred_element_type=F32)
    o_ref[...] = (b * y * gate).astype(o_ref.dtype)


def _proj_conv(u, w_hb, w_cz, conv_w, conv_width, side, tm=1024, tn=256):
    s, d = u.shape
    nj = conv_width // tn
    ni = s // tm
    side_in, side_out, side_shape = _side_cast_specs(
        side, 0, side.shape[1], nj * ni, lambda j, i: j * ni + i)
    w_spec = lambda part: pl.BlockSpec((d, tn), lambda j, i, part=part: (0, part * nj + j))
    return pl.pallas_call(
        _proj_conv_kernel,
        grid=(nj, ni),
        in_specs=[pl.BlockSpec((tm, d), lambda j, i: (i, 0)),
                  w_spec(0), w_spec(1), w_spec(0), w_spec(1),
                  pl.BlockSpec((conv_w.shape[0], tn), lambda j, i: (0, j)), side_in],
        out_specs=[pl.BlockSpec((tm, tn), lambda j, i: (i, j)), side_out],
        out_shape=[jax.ShapeDtypeStruct((s, conv_width), BF16), side_shape],
        scratch_shapes=[pltpu.VMEM((8, tn), F32)],
        compiler_params=pltpu.CompilerParams(
            dimension_semantics=("arbitrary", "arbitrary"),
            vmem_limit_bytes=V7X_VMEM_LIMIT_BYTES),
        name="proj_conv",
    )(u, w_hb, w_hb, w_cz, w_cz, conv_w, side)


def _rel_bucket_np(dist):
    n = np.maximum(dist, 0)
    max_exact = NUM_BUCKETS // 2
    nf = np.maximum(n, 1).astype(np.float32)
    ratio = np.log(nf / np.float32(max_exact)) / np.float32(math.log(MAX_DISTANCE / max_exact))
    large = max_exact + (ratio * np.float32(NUM_BUCKETS - max_exact)).astype(np.int32)
    large = np.minimum(large, NUM_BUCKETS - 1)
    return np.where(n < max_exact, n, large).astype(np.int32)


def _bucket_tables():
    key = np.arange(BLOCK)[:, None]
    qry = np.arange(BLOCK)[None, :]
    d_own = qry - key
    own = np.where(d_own >= 0, _rel_bucket_np(d_own), NUM_BUCKETS).astype(np.int32)
    adj = _rel_bucket_np(BLOCK + qry - key)
    return own, adj


GROUP = 4
QPAIR = 2


def _attn_kernel(relb_ref, idx_own_ref, idx_adj_ref, q_ref, ks_ref, vt_ref, km_ref, sz_ref,
                 o_ref,
                 kaug_ref, bown_ref, badj_ref, qat_ref, pen_ref, m_ref, l_ref, acc_ref,
                 sa_ref, sb_ref, *, nb):
    h = pl.program_id(0)
    gk = GROUP * BLOCK
    pw = QPAIR * BLOCK
    dot = functools.partial(jnp.dot, preferred_element_type=F32)

    @pl.when(h == 0)
    def _():
        lane = lax.broadcasted_iota(jnp.int32, (BLOCK, HEAD_DIM), 1)
        for blk in range(nb):
            r0 = (blk % GROUP) * BLOCK
            kaug_ref[blk // GROUP, r0:r0 + BLOCK, HEAD_DIM:] = jnp.where(lane == blk, 1.0, 0.0).astype(BF16)

    far_bias = relb_ref[NUM_BUCKETS - 1, h]
    io = idx_own_ref[...]
    ia = idx_adj_ref[...]
    own = jnp.full((BLOCK, BLOCK), NEG, F32)
    adj = jnp.zeros((BLOCK, BLOCK), F32)
    for b in range(NUM_BUCKETS):
        t = (relb_ref[b, h] - far_bias) * LOG2E
        own = jnp.where(io == b, t, own)
        adj = jnp.where(ia == b, t, adj)
    bown_ref[...] = own
    badj_ref[...] = adj

    def stage(g, carry):
        rows = pl.ds(pl.multiple_of(g * gk, gk), gk)
        kaug_ref[g, :, :HEAD_DIM] = ks_ref[rows, :]
        return carry

    lax.fori_loop(0, nb // GROUP, stage, 0)

    km = km_ref[...]
    km_hi = km.astype(BF16)
    km_lo = (km - km_hi.astype(F32)).astype(BF16)

    ROUTE_CLASSES = 4
    rq = ROUTE_CLASSES * gk

    def route(cc, carry):
        rows = pl.ds(pl.multiple_of(cc * rq, rq), rq)
        q_t = q_ref[rows, :].astype(F32).T.astype(BF16)
        gate = dot(km_hi, q_t) + dot(km_lo, q_t)
        blk_id = lax.broadcasted_iota(jnp.int32, gate.shape, 0)
        qb = cc * (ROUTE_CLASSES * GROUP) + lax.shift_right_logical(
            lax.broadcasted_iota(jnp.int32, gate.shape, 1), int(math.log2(BLOCK)))
        gate = jnp.where(blk_id < qb, gate, -jnp.inf)
        sel = jnp.zeros(gate.shape, jnp.bool_)
        for r in range(min(TOPK, nb)):
            mx = jnp.max(gate, axis=0, keepdims=True)
            first = jnp.min(jnp.where(gate == mx, blk_id, nb), axis=0, keepdims=True)
            pick = (blk_id == first) & (qb > r)
            sel = sel | pick
            gate = jnp.where(pick, -jnp.inf, gate)
        sel_far = jnp.where(sel & (blk_id < qb - 1), 0.0, NEG)
        sel_rows = jnp.concatenate(
            [sel_far, jnp.zeros((HEAD_DIM - nb, rq), F32)], axis=0).astype(BF16)
        adj_sel = jnp.max(jnp.where(sel & (blk_id == qb - 1), 1.0, 0.0), axis=0, keepdims=True)
        pen = jnp.where(adj_sel > 0.0, 0.0, NEG)
        for t in range(rq // pw):
            cols = slice(t * pw, (t + 1) * pw)
            qat_ref[cc * (rq // pw) + t, :HEAD_DIM, :] = q_t[:, cols]
            qat_ref[cc * (rq // pw) + t, HEAD_DIM:, :] = sel_rows[:, cols]
        for j in range(rq // BLOCK):
            cols = slice(j * BLOCK, (j + 1) * BLOCK)
            pen_ref[cc * (rq // BLOCK) + j] = jnp.broadcast_to(pen[:, cols], (8, BLOCK))
        return carry

    lax.fori_loop(0, nb // (ROUTE_CLASSES * GROUP), route, 0)

    def colmax(s):
        return jnp.max(s, axis=0, keepdims=True)

    def colsum(p):
        return jnp.sum(p, axis=0, keepdims=True)

    def pv(g, first, p):
        keys = slice((g * GROUP + first) * BLOCK, (g * GROUP + first) * BLOCK + p.shape[0])
        return dot(vt_ref[:, keys], p.astype(BF16))

    def do_class(c, carry):
        def far_logits(g, t):
            return dot(kaug_ref[g], qat_ref[c * (GROUP // QPAIR) + t])

        def far_park(g, t, s_ref):
            s = far_logits(g, t)
            s_ref[...] = s
            return colmax(s)

        lanes_of = [slice((j % QPAIR) * BLOCK, (j % QPAIR + 1) * BLOCK) for j in range(GROUP)]
        near = []
        for j in range(GROUP):
            qb = c * GROUP + j
            pair = c * (GROUP // QPAIR) + j // QPAIR
            q_t = qat_ref[pair, :HEAD_DIM, lanes_of[j]]
            parts = [(dot(kaug_ref[c, j * BLOCK:(j + 1) * BLOCK, :HEAD_DIM], q_t) + bown_ref[...], c, j)]
            if qb > 0:
                g_adj, t_adj = (c, j - 1) if j > 0 else (c - 1, GROUP - 1)
                k_adj = kaug_ref[g_adj, t_adj * BLOCK:(t_adj + 1) * BLOCK, :HEAD_DIM]
                parts.append((dot(k_adj, q_t) + (badj_ref[...] + pen_ref[qb, 0:1, :]), g_adj, t_adj))
            if j >= 2:
                parts.append((dot(kaug_ref[c, :(j - 1) * BLOCK, :], qat_ref[pair, :, lanes_of[j]]), c, 0))
            near.append(parts)

        if c > 0:
            cmax_a = far_park(0, 0, sa_ref)

        for j, parts in enumerate(near):
            m = functools.reduce(jnp.maximum, [colmax(s) for s, _, _ in parts])
            probs = [(jnp.exp2(s - m), g, first) for s, g, first in parts]
            m_ref[j // QPAIR, :, lanes_of[j]] = m
            l_ref[j // QPAIR, :, lanes_of[j]] = functools.reduce(jnp.add, [colsum(p) for p, _, _ in probs])
            acc_ref[j // QPAIR, :, lanes_of[j]] = functools.reduce(
                jnp.add, [pv(g, first, p) for p, g, first in probs])

        def far_update(g, t, s_ref, cmax):
            m_old = m_ref[t]
            m_new = jnp.maximum(m_old, cmax)
            alpha = jnp.exp2(m_old - m_new)
            l_new = alpha * l_ref[t]
            acc_new = alpha * acc_ref[t]
            for k in range(GROUP):
                p = jnp.exp2(s_ref[k * BLOCK:(k + 1) * BLOCK, :] - m_new)
                l_new = l_new + colsum(p)
                acc_new = acc_new + pv(g, k, p)
            l_ref[t] = l_new
            acc_ref[t] = acc_new
            m_ref[t] = m_new

        for g in range(c):
            cmax_b = far_park(g, 1, sb_ref)
            far_update(g, 0, sa_ref, cmax_a)
            if g + 1 < c:
                cmax_a = far_park(g + 1, 0, sa_ref)
            far_update(g, 1, sb_ref, cmax_b)

        for j in range(GROUP):
            rows = pl.ds((c * GROUP + j) * BLOCK, BLOCK)
            out_t = acc_ref[j // QPAIR, :, lanes_of[j]] * (1.0 / l_ref[j // QPAIR, :, lanes_of[j]])
            o_ref[rows, :] = (out_t.T * sz_ref[rows, :]).astype(o_ref.dtype)
        return carry

    one = jnp.minimum(h + 1, 1)
    for c in range(nb // GROUP):
        lax.fori_loop(0, one, lambda _, carry, c=c: do_class(c, carry), 0)


def _moba_attention(q, ks, v_t, kmean, sz, rel_bias):
    s, attn_w = q.shape
    nh = attn_w // HEAD_DIM
    nb = s // BLOCK
    assert nb % GROUP == 0 and nb <= HEAD_DIM
    idx_own, idx_adj = _bucket_tables()
    head_cols = pl.BlockSpec((s, HEAD_DIM), lambda h: (0, h))
    const_tile = pl.BlockSpec((BLOCK, BLOCK), lambda h: (0, 0))
    return pl.pallas_call(
        functools.partial(_attn_kernel, nb=nb),
        grid=(nh,),
        in_specs=[pl.BlockSpec(memory_space=pltpu.SMEM),
                  const_tile, const_tile,
                  head_cols, head_cols,
                  pl.BlockSpec((HEAD_DIM, s), lambda h: (h, 0)),
                  pl.BlockSpec((nb, HEAD_DIM), lambda h: (0, h)),
                  head_cols],
        out_specs=head_cols,
        out_shape=jax.ShapeDtypeStruct((s, attn_w), BF16),
        scratch_shapes=[pltpu.VMEM((nb // GROUP, GROUP * BLOCK, 2 * HEAD_DIM), BF16),
                        pltpu.VMEM((BLOCK, BLOCK), F32),
                        pltpu.VMEM((BLOCK, BLOCK), F32),
                        pltpu.VMEM((nb // QPAIR, 2 * HEAD_DIM, QPAIR * BLOCK), BF16),
                        pltpu.VMEM((nb, 8, BLOCK), F32),
                        pltpu.VMEM((GROUP // QPAIR, 1, QPAIR * BLOCK), F32),
                        pltpu.VMEM((GROUP // QPAIR, 1, QPAIR * BLOCK), F32),
                        pltpu.VMEM((GROUP // QPAIR, HEAD_DIM, QPAIR * BLOCK), F32),
                        pltpu.VMEM((GROUP * BLOCK, QPAIR * BLOCK), F32),
                        pltpu.VMEM((GROUP * BLOCK, QPAIR * BLOCK), F32)],
        compiler_params=pltpu.CompilerParams(
            dimension_semantics=("arbitrary",),
            vmem_limit_bytes=V7X_VMEM_LIMIT_BYTES),
        name="moba_attn",
    )(rel_bias, jnp.asarray(idx_own), jnp.asarray(idx_adj), q, ks, v_t, kmean, sz)


def _out_kernel(a_ref, c_ref, wa_ref, wc_ref, x_ref, g_ref, o_ref, *, n_chunks):
    d = o_ref.shape[1]
    w = d // n_chunks
    a = a_ref[...]
    c = c_ref[...]
    hs = []
    ss = jnp.zeros((o_ref.shape[0], 1), F32)
    for n in range(n_chunks):
        cols = slice(n * w, (n + 1) * w)
        h = (x_ref[:, cols]
             + jnp.dot(a, wa_ref[:, cols], preferred_element_type=F32)
             + jnp.dot(c, wc_ref[:, cols], preferred_element_type=F32))
        ss = ss + jnp.sum(h * h, axis=-1, keepdims=True)
        hs.append(h)
    inv = lax.rsqrt(ss / d + EPS)
    for n, h in enumerate(hs):
        cols = slice(n * w, (n + 1) * w)
        o_ref[:, cols] = h * inv * g_ref[:, cols]


def _out_proj(attn, conv, w_out, x, g, tm=256):
    s, d = x.shape
    ka = attn.shape[1]
    kc = conv.shape[1]
    assert ka == kc
    resident = pl.Buffered(1)
    return pl.pallas_call(
        functools.partial(_out_kernel, n_chunks=4),
        grid=(s // tm,),
        in_specs=[pl.BlockSpec((tm, ka), lambda i: (i, 0)),
                  pl.BlockSpec((tm, kc), lambda i: (i, 0)),
                  pl.BlockSpec((ka, d), lambda i: (0, 0), pipeline_mode=resident),
                  pl.BlockSpec((kc, d), lambda i: (1, 0), pipeline_mode=resident),
                  pl.BlockSpec((tm, d), lambda i: (i, 0)),
                  pl.BlockSpec((1, d), lambda i: (0, 0))],
        out_specs=pl.BlockSpec((tm, d), lambda i: (i, 0)),
        out_shape=jax.ShapeDtypeStruct((s, d), F32),
        compiler_params=pltpu.CompilerParams(
            dimension_semantics=("arbitrary",),
            vmem_limit_bytes=V7X_OUT_PROJ_VMEM_LIMIT_BYTES),
        name="out_proj_norm",
    )(attn, conv, w_out, w_out, x, g.reshape(1, d))


def kernel(x, norm_gain, w_in, conv_w, w_out, rel_bias, final_gain):
    b, s, d = x.shape
    depth = norm_gain.shape[0]
    assert b == 1 and depth == 1, "kernel is written for one sequence and one layer"
    attn_w = d // 2
    conv_width = d - attn_w
    assert rel_bias.shape == (NUM_BUCKETS, attn_w // HEAD_DIM)
    assert w_in.shape[2] == 4 * attn_w + 4 * conv_width

    x2 = x[0]
    assert attn_w == conv_width
    pair_w = 2 * attn_w
    w_qk_b = w_in[0, :, :pair_w].astype(BF16)

    u = _rmsnorm_bf16(x2, norm_gain[0])
    tm, tn = 1024, 512
    tile = pl.BlockSpec((tm, tn), lambda j, i: (i, j))
    act = jax.ShapeDtypeStruct((s, attn_w), BF16)
    q, ks, kmean, w_vz_b, w_hb_b = _proj_pair(
        functools.partial(_proj_qk_kernel, scale=HEAD_DIM ** -0.5 * LOG2E), "proj_qk", u, w_qk_b, attn_w,
        [tile, tile, pl.BlockSpec((1, tm // BLOCK, tn), lambda j, i: (i, 0, j))],
        [act, act, jax.ShapeDtypeStruct((s // tm, tm // BLOCK, attn_w), F32)],
        [(w_in[0], 1, pair_w), (w_in[0], 2, pair_w)], tm, tn)
    v_t, sz, w_cz_b = _proj_pair(
        _proj_vz_kernel, "proj_vz", u, w_vz_b, attn_w,
        [pl.BlockSpec((tn, tm), lambda j, i: (j, i)), tile],
        [jax.ShapeDtypeStruct((attn_w, s), BF16), jax.ShapeDtypeStruct((s, attn_w), F32)],
        [(w_in[0], 3, pair_w)], tm, tn)
    conv, w_out_b = _proj_conv(u, w_hb_b, w_cz_b, conv_w[0], conv_width, w_out[0])
    attn = _moba_attention(q, ks, v_t, kmean.reshape(s // BLOCK, attn_w), sz, rel_bias)
    out = _out_proj(attn, conv, w_out_b, x2, final_gain)
    return out[None]
```

```python
import functools
import math

import numpy as np
import jax
import jax.numpy as jnp
from jax import lax
from jax.experimental import pallas as pl
from jax.experimental.pallas import tpu as pltpu

HEAD_DIM = 128
BLOCK = 256
TOPK = 3
NUM_BUCKETS = 32
MAX_DISTANCE = 128
EPS = 1e-6
NEG = -1e30
LOG2E = math.log2(math.e)

F32 = jnp.float32
BF16 = jnp.bfloat16

V7X_VMEM_LIMIT_BYTES = 56 * 1024 * 1024
V7X_OUT_PROJ_VMEM_LIMIT_BYTES = 60 * 1024 * 1024


def _silu(z):
    return z * (1.0 / (1.0 + jnp.exp(-z)))


def _rmsnorm_kernel(x_ref, g_ref, o_ref):
    x = x_ref[...]
    ms = jnp.mean(x * x, axis=-1, keepdims=True)
    o_ref[...] = (x * lax.rsqrt(ms + EPS) * g_ref[...]).astype(o_ref.dtype)


def _rmsnorm_bf16(x, g, tm=512):
    s, d = x.shape
    return pl.pallas_call(
        _rmsnorm_kernel,
        grid=(s // tm,),
        in_specs=[pl.BlockSpec((tm, d), lambda i: (i, 0)),
                  pl.BlockSpec((1, d), lambda i: (0, 0))],
        out_specs=pl.BlockSpec((tm, d), lambda i: (i, 0)),
        out_shape=jax.ShapeDtypeStruct((s, d), BF16),
        compiler_params=pltpu.CompilerParams(dimension_semantics=("arbitrary",)),
        name="rmsnorm_in",
    )(x, g.reshape(1, d))


def _run_side_jobs(side_refs, side_out_refs):
    for src, dst in zip(side_refs, side_out_refs):
        dst[...] = src[...].astype(dst.dtype)


def _side_cast_specs(side, col_block, cols, n_steps, step_of):
    rows = side.shape[0]
    chunk = rows // n_steps
    assert chunk * n_steps == rows and chunk % 16 == 0 and side.shape[1] % cols == 0
    return (pl.BlockSpec((chunk, cols), lambda j, i: (step_of(j, i), col_block)),
            pl.BlockSpec((chunk, cols), lambda j, i: (step_of(j, i), 0)),
            jax.ShapeDtypeStruct((rows, cols), BF16))


def _proj_qk_kernel(u_ref, wq_ref, wk_ref, *refs, scale, n_side):
    side_refs, (q_ref, ks_ref, km_ref), side_out_refs = refs[:n_side], refs[n_side:n_side + 3], refs[n_side + 3:]
    _run_side_jobs(side_refs, side_out_refs)
    u = u_ref[...]
    tm = u.shape[0]
    q = jnp.dot(u, wq_ref[...], preferred_element_type=F32)
    q_ref[...] = q.T.astype(BF16)
    k = jnp.dot(u, wk_ref[...], preferred_element_type=F32)
    ks_ref[...] = (k * scale).astype(BF16)
    km_ref[0] = jnp.mean(k.reshape(tm // BLOCK, BLOCK, k.shape[1]), axis=1)


def _proj_vz_kernel(u_ref, wv_ref, wz_ref, *refs, n_side):
    side_refs, (v_ref, sz_ref), side_out_refs = refs[:n_side], refs[n_side:n_side + 2], refs[n_side + 2:]
    _run_side_jobs(side_refs, side_out_refs)
    u = u_ref[...]
    z = jnp.dot(u, wz_ref[...], preferred_element_type=F32)
    sz_ref[...] = _silu(z)
    v = jnp.dot(u, wv_ref[...], preferred_element_type=F32)
    v_ref[...] = v.T.astype(BF16)


def _proj_pair(kernel_fn, name, u, w, width, out_specs, out_shape, side_jobs, tm, tn):
    s, d = u.shape
    nj = width // tn
    ni = s // tm
    jobs = [_side_cast_specs(m, cb, cols, nj * ni, lambda j, i: j * ni + i) for m, cb, cols in side_jobs]
    w_spec = lambda part: pl.BlockSpec((d, tn), lambda j, i, part=part: (0, part * nj + j))
    return pl.pallas_call(
        functools.partial(kernel_fn, n_side=len(jobs)),
        grid=(nj, ni),
        in_specs=[pl.BlockSpec((tm, d), lambda j, i: (i, 0)), w_spec(0), w_spec(1)] + [jb[0] for jb in jobs],
        out_specs=out_specs + [jb[1] for jb in jobs],
        out_shape=out_shape + [jb[2] for jb in jobs],
        compiler_params=pltpu.CompilerParams(
            dimension_semantics=("arbitrary", "arbitrary"),
            vmem_limit_bytes=V7X_VMEM_LIMIT_BYTES),
        name=name,
    )(u, w, w, *[m for m, _, _ in side_jobs])


def _proj_conv_kernel(u_ref, wh_ref, wb_ref, wc_ref, wz_ref, cw_ref, side_ref,
                      o_ref, side_out_ref, carry_ref):
    _run_side_jobs([side_ref], [side_out_ref])
    @pl.when(pl.program_id(1) == 0)
    def _():
        carry_ref[...] = jnp.zeros_like(carry_ref)

    u = u_ref[...]
    tm = u.shape[0]
    hc = jnp.dot(u, wh_ref[...], preferred_element_type=F32)
    c = jnp.dot(u, wc_ref[...], preferred_element_type=F32)
    p = c * hc
    rows = lax.broadcasted_iota(jnp.int32, p.shape, 0)
    prev = carry_ref[...]
    prev1 = prev[7:8, :]
    prev2 = prev[6:7, :]
    p1 = jnp.where(rows == 0, prev1, pltpu.roll(p, 1, 0))
    p2 = jnp.where(rows == 0, prev2, jnp.where(rows == 1, prev1, pltpu.roll(p, 2, 0)))
    carry_ref[...] = p[tm - 8:, :]
    cw = cw_ref[...]
    y = p2 * cw[0:1, :] + p1 * cw[1:2, :] + p * cw[2:3, :]
    z = jnp.dot(u, wz_ref[...], preferred_element_type=F32)
    gate = _silu(z)
    b = jnp.dot(u, wb_ref[...], preferred_element_type=F32)
    o_ref[...] = (b * y * gate).astype(o_ref.dtype)


def _proj_conv(u, w_hb, w_cz, conv_w, conv_width, side, tm=1024, tn=256):
    s, d = u.shape
    nj = conv_width // tn
    ni = s // tm
    side_in, side_out, side_shape = _side_cast_specs(
        side, 0, side.shape[1], nj * ni, lambda j, i: j * ni + i)
    w_spec = lambda part: pl.BlockSpec((d, tn), lambda j, i, part=part: (0, part * nj + j))
    return pl.pallas_call(
        _proj_conv_kernel,
        grid=(nj, ni),
        in_specs=[pl.BlockSpec((tm, d), lambda j, i: (i, 0)),
                  w_spec(0), w_spec(1), w_spec(0), w_spec(1),
                  pl.BlockSpec((conv_w.shape[0], tn), lambda j, i: (0, j)), side_in],
        out_specs=[pl.BlockSpec((tm, tn), lambda j, i: (i, j)), side_out],
        out_shape=[jax.ShapeDtypeStruct((s, conv_width), BF16), side_shape],
        scratch_shapes=[pltpu.VMEM((8, tn), F32)],
        compiler_params=pltpu.CompilerParams(
            dimension_semantics=("arbitrary", "arbitrary"),
            vmem_limit_bytes=V7X_VMEM_LIMIT_BYTES),
        name="proj_conv",
    )(u, w_hb, w_hb, w_cz, w_cz, conv_w, side)


def _rel_bucket_np(dist):
    n = np.maximum(dist, 0)
    max_exact = NUM_BUCKETS // 2
    nf = np.maximum(n, 1).astype(np.float32)
    ratio = np.log(nf / np.float32(max_exact)) / np.float32(math.log(MAX_DISTANCE / max_exact))
    large = max_exact + (ratio * np.float32(NUM_BUCKETS - max_exact)).astype(np.int32)
    large = np.minimum(large, NUM_BUCKETS - 1)
    return np.where(n < max_exact, n, large).astype(np.int32)


def _bucket_tables():
    key = np.arange(BLOCK)[:, None]
    qry = np.arange(BLOCK)[None, :]
    d_own = qry - key
    own = np.where(d_own >= 0, _rel_bucket_np(d_own), NUM_BUCKETS).astype(np.int32)
    adj = _rel_bucket_np(BLOCK + qry - key)
    return own, adj


GROUP = 4
QPAIR = 2


def _attn_kernel(relb_ref, idx_own_ref, idx_adj_ref, q_ref, ks_ref, vt_ref, km_ref, sz_ref,
                 o_ref,
                 kaug_ref, bown_ref, badj_ref, qat_ref, pen_ref, m_ref, l_ref, acc_ref,
                 sa_ref, sb_ref, *, nb):
    h = pl.program_id(0)
    gk = GROUP * BLOCK
    pw = QPAIR * BLOCK
    dot = functools.partial(jnp.dot, preferred_element_type=F32)

    @pl.when(h == 0)
    def _():
        lane = lax.broadcasted_iota(jnp.int32, (BLOCK, HEAD_DIM), 1)
        for blk in range(nb):
            r0 = (blk % GROUP) * BLOCK
            kaug_ref[blk // GROUP, r0:r0 + BLOCK, HEAD_DIM:] = jnp.where(lane == blk, 1.0, 0.0).astype(BF16)

    far_bias = relb_ref[NUM_BUCKETS - 1, h]
    io = idx_own_ref[...]
    ia = idx_adj_ref[...]
    own = jnp.full((BLOCK, BLOCK), NEG, F32)
    adj = jnp.zeros((BLOCK, BLOCK), F32)
    for b in range(NUM_BUCKETS):
        t = (relb_ref[b, h] - far_bias) * LOG2E
        own = jnp.where(io == b, t, own)
        adj = jnp.where(ia == b, t, adj)
    bown_ref[...] = own
    badj_ref[...] = adj

    def stage(g, carry):
        rows = pl.ds(pl.multiple_of(g * gk, gk), gk)
        kaug_ref[g, :, :HEAD_DIM] = ks_ref[rows, :]
        return carry

    lax.fori_loop(0, nb // GROUP, stage, 0)

    km = km_ref[...]
    km_hi = km.astype(BF16)
    km_lo = (km - km_hi.astype(F32)).astype(BF16)

    ROUTE_CLASSES = 4
    rq = ROUTE_CLASSES * gk

    def route(cc):
        q_t = q_ref[:, cc * rq:(cc + 1) * rq]
        gate = dot(km_hi, q_t) + dot(km_lo, q_t)
        blk_id = lax.broadcasted_iota(jnp.int32, gate.shape, 0)
        qb = cc * (ROUTE_CLASSES * GROUP) + lax.shift_right_logical(
            lax.broadcasted_iota(jnp.int32, gate.shape, 1), int(math.log2(BLOCK)))
        gate = jnp.where(blk_id < qb, gate, -jnp.inf)
        sel = jnp.zeros(gate.shape, jnp.bool_)
        for r in range(min(TOPK, nb)):
            mx = jnp.max(gate, axis=0, keepdims=True)
            first = jnp.min(jnp.where(gate == mx, blk_id, nb), axis=0, keepdims=True)
            pick = (blk_id == first) & (qb > r)
            sel = sel | pick
            gate = jnp.where(pick, -jnp.inf, gate)
        sel_far = jnp.where(sel & (blk_id < qb - 1), 0.0, NEG)
        sel_rows = jnp.concatenate(
            [sel_far, jnp.zeros((HEAD_DIM - nb, rq), F32)], axis=0).astype(BF16)
        adj_sel = jnp.max(jnp.where(sel & (blk_id == qb - 1), 1.0, 0.0), axis=0, keepdims=True)
        pen = jnp.where(adj_sel > 0.0, 0.0, NEG)
        for t in range(rq // pw):
            cols = slice(t * pw, (t + 1) * pw)
            qat_ref[cc * (rq // pw) + t, :HEAD_DIM, :] = q_t[:, cols]
            qat_ref[cc * (rq // pw) + t, HEAD_DIM:, :] = sel_rows[:, cols]
        for j in range(rq // BLOCK):
            cols = slice(j * BLOCK, (j + 1) * BLOCK)
            pen_ref[cc * (rq // BLOCK) + j] = jnp.broadcast_to(pen[:, cols], (8, BLOCK))

    for cc in range(nb // (ROUTE_CLASSES * GROUP)):
        route(cc)

    def colmax(s):
        return jnp.max(s, axis=0, keepdims=True)

    def colsum(p):
        return jnp.sum(p, axis=0, keepdims=True)

    def pv(g, first, p):
        keys = slice((g * GROUP + first) * BLOCK, (g * GROUP + first) * BLOCK + p.shape[0])
        return dot(vt_ref[:, keys], p.astype(BF16))

    def do_class(c, carry):
        def far_logits(g, t):
            return dot(kaug_ref[g], qat_ref[c * (GROUP // QPAIR) + t])

        def far_park(g, t, s_ref):
            s = far_logits(g, t)
            s_ref[...] = s
            return colmax(s)

        lanes_of = [slice((j % QPAIR) * BLOCK, (j % QPAIR + 1) * BLOCK) for j in range(GROUP)]
        near = []
        for j in range(GROUP):
            qb = c * GROUP + j
            pair = c * (GROUP // QPAIR) + j // QPAIR
            q_t = qat_ref[pair, :HEAD_DIM, lanes_of[j]]
            parts = [(dot(kaug_ref[c, j * BLOCK:(j + 1) * BLOCK, :HEAD_DIM], q_t) + bown_ref[...], c, j)]
            if qb > 0:
                g_adj, t_adj = (c, j - 1) if j > 0 else (c - 1, GROUP - 1)
                k_adj = kaug_ref[g_adj, t_adj * BLOCK:(t_adj + 1) * BLOCK, :HEAD_DIM]
                parts.append((dot(k_adj, q_t) + (badj_ref[...] + pen_ref[qb, 0:1, :]), g_adj, t_adj))
            if j >= 2:
                parts.append((dot(kaug_ref[c, :(j - 1) * BLOCK, :], qat_ref[pair, :, lanes_of[j]]), c, 0))
            near.append(parts)

        if c > 0:
            cmax_a = far_park(0, 0, sa_ref)

        for j, parts in enumerate(near):
            m = functools.reduce(jnp.maximum, [colmax(s) for s, _, _ in parts])
            probs = [(jnp.exp2(s - m), g, first) for s, g, first in parts]
            m_ref[j // QPAIR, :, lanes_of[j]] = m
            l_ref[j // QPAIR, :, lanes_of[j]] = functools.reduce(jnp.add, [colsum(p) for p, _, _ in probs])
            acc_ref[j // QPAIR, :, lanes_of[j]] = functools.reduce(
                jnp.add, [pv(g, first, p) for p, g, first in probs])

        def far_update(g, t, s_ref, cmax):
            m_old = m_ref[t]
            m_new = jnp.maximum(m_old, cmax)
            alpha = jnp.exp2(m_old - m_new)
            l_new = alpha * l_ref[t]
            acc_new = alpha * acc_ref[t]
            for k in range(GROUP):
                p = jnp.exp2(s_ref[k * BLOCK:(k + 1) * BLOCK, :] - m_new)
                l_new = l_new + colsum(p)
                acc_new = acc_new + pv(g, k, p)
            l_ref[t] = l_new
            acc_ref[t] = acc_new
            m_ref[t] = m_new

        for g in range(c):
            cmax_b = far_park(g, 1, sb_ref)
            far_update(g, 0, sa_ref, cmax_a)
            if g + 1 < c:
                cmax_a = far_park(g + 1, 0, sa_ref)
            far_update(g, 1, sb_ref, cmax_b)

        for j in range(GROUP):
            rows = pl.ds((c * GROUP + j) * BLOCK, BLOCK)
            out_t = acc_ref[j // QPAIR, :, lanes_of[j]] * (1.0 / l_ref[j // QPAIR, :, lanes_of[j]])
            o_ref[rows, :] = (out_t.T * sz_ref[rows, :]).astype(o_ref.dtype)
        return carry

    one = jnp.minimum(h + 1, 1)
    for c in range(nb // GROUP):
        lax.fori_loop(0, one, lambda _, carry, c=c: do_class(c, carry), 0)


def _moba_attention(q_t, ks, v_t, kmean, sz, rel_bias):
    s, attn_w = ks.shape
    nh = attn_w // HEAD_DIM
    nb = s // BLOCK
    assert nb % GROUP == 0 and nb <= HEAD_DIM
    idx_own, idx_adj = _bucket_tables()
    head_cols = pl.BlockSpec((s, HEAD_DIM), lambda h: (0, h))
    const_tile = pl.BlockSpec((BLOCK, BLOCK), lambda h: (0, 0))
    return pl.pallas_call(
        functools.partial(_attn_kernel, nb=nb),
        grid=(nh,),
        in_specs=[pl.BlockSpec(memory_space=pltpu.SMEM),
                  const_tile, const_tile,
                  pl.BlockSpec((HEAD_DIM, s), lambda h: (h, 0)),
                  head_cols,
                  pl.BlockSpec((HEAD_DIM, s), lambda h: (h, 0)),
                  pl.BlockSpec((nb, HEAD_DIM), lambda h: (0, h)),
                  head_cols],
        out_specs=head_cols,
        out_shape=jax.ShapeDtypeStruct((s, attn_w), BF16),
        scratch_shapes=[pltpu.VMEM((nb // GROUP, GROUP * BLOCK, 2 * HEAD_DIM), BF16),
                        pltpu.VMEM((BLOCK, BLOCK), F32),
                        pltpu.VMEM((BLOCK, BLOCK), F32),
                        pltpu.VMEM((nb // QPAIR, 2 * HEAD_DIM, QPAIR * BLOCK), BF16),
                        pltpu.VMEM((nb, 8, BLOCK), F32),
                        pltpu.VMEM((GROUP // QPAIR, 1, QPAIR * BLOCK), F32),
                        pltpu.VMEM((GROUP // QPAIR, 1, QPAIR * BLOCK), F32),
                        pltpu.VMEM((GROUP // QPAIR, HEAD_DIM, QPAIR * BLOCK), F32),
                        pltpu.VMEM((GROUP * BLOCK, QPAIR * BLOCK), F32),
                        pltpu.VMEM((GROUP * BLOCK, QPAIR * BLOCK), F32)],
        compiler_params=pltpu.CompilerParams(
            dimension_semantics=("arbitrary",),
            vmem_limit_bytes=V7X_VMEM_LIMIT_BYTES),
        name="moba_attn",
    )(rel_bias, jnp.asarray(idx_own), jnp.asarray(idx_adj), q_t, ks, v_t, kmean, sz)


def _out_kernel(a_ref, c_ref, wa_ref, wc_ref, x_ref, g_ref, o_ref, *, n_chunks):
    d = o_ref.shape[1]
    w = d // n_chunks
    a = a_ref[...]
    c = c_ref[...]
    hs = []
    ss = jnp.zeros((o_ref.shape[0], 1), F32)
    for n in range(n_chunks):
        cols = slice(n * w, (n + 1) * w)
        h = (x_ref[:, cols]
             + jnp.dot(a, wa_ref[:, cols], preferred_element_type=F32)
             + jnp.dot(c, wc_ref[:, cols], preferred_element_type=F32))
        ss = ss + jnp.sum(h * h, axis=-1, keepdims=True)
        hs.append(h)
    inv = lax.rsqrt(ss / d + EPS)
    for n, h in enumerate(hs):
        cols = slice(n * w, (n + 1) * w)
        o_ref[:, cols] = h * inv * g_ref[:, cols]


def _out_proj(attn, conv, w_out, x, g, tm=256):
    s, d = x.shape
    ka = attn.shape[1]
    kc = conv.shape[1]
    assert ka == kc
    resident = pl.Buffered(1)
    return pl.pallas_call(
        functools.partial(_out_kernel, n_chunks=4),
        grid=(s // tm,),
        in_specs=[pl.BlockSpec((tm, ka), lambda i: (i, 0)),
                  pl.BlockSpec((tm, kc), lambda i: (i, 0)),
                  pl.BlockSpec((ka, d), lambda i: (0, 0), pipeline_mode=resident),
                  pl.BlockSpec((kc, d), lambda i: (1, 0), pipeline_mode=resident),
                  pl.BlockSpec((tm, d), lambda i: (i, 0)),
                  pl.BlockSpec((1, d), lambda i: (0, 0))],
        out_specs=pl.BlockSpec((tm, d), lambda i: (i, 0)),
        out_shape=jax.ShapeDtypeStruct((s, d), F32),
        compiler_params=pltpu.CompilerParams(
            dimension_semantics=("arbitrary",),
            vmem_limit_bytes=V7X_OUT_PROJ_VMEM_LIMIT_BYTES),
        name="out_proj_norm",
    )(attn, conv, w_out, w_out, x, g.reshape(1, d))


def kernel(x, norm_gain, w_in, conv_w, w_out, rel_bias, final_gain):
    b, s, d = x.shape
    depth = norm_gain.shape[0]
    assert b == 1 and depth == 1, "kernel is written for one sequence and one layer"
    attn_w = d // 2
    conv_width = d - attn_w
    assert rel_bias.shape == (NUM_BUCKETS, attn_w // HEAD_DIM)
    assert w_in.shape[2] == 4 * attn_w + 4 * conv_width

    x2 = x[0]
    assert attn_w == conv_width
    pair_w = 2 * attn_w
    w_qk_b = w_in[0, :, :pair_w].astype(BF16)

    u = _rmsnorm_bf16(x2, norm_gain[0])
    tm, tn = 1024, 512
    tile = pl.BlockSpec((tm, tn), lambda j, i: (i, j))
    act = jax.ShapeDtypeStruct((s, attn_w), BF16)
    tile_t = pl.BlockSpec((tn, tm), lambda j, i: (j, i))
    act_t = jax.ShapeDtypeStruct((attn_w, s), BF16)
    q_t, ks, kmean, w_vz_b, w_hb_b = _proj_pair(
        functools.partial(_proj_qk_kernel, scale=HEAD_DIM ** -0.5 * LOG2E), "proj_qk", u, w_qk_b, attn_w,
        [tile_t, tile, pl.BlockSpec((1, tm // BLOCK, tn), lambda j, i: (i, 0, j))],
        [act_t, act, jax.ShapeDtypeStruct((s // tm, tm // BLOCK, attn_w), F32)],
        [(w_in[0], 1, pair_w), (w_in[0], 2, pair_w)], tm, tn)
    v_t, sz, w_cz_b = _proj_pair(
        _proj_vz_kernel, "proj_vz", u, w_vz_b, attn_w,
        [pl.BlockSpec((tn, tm), lambda j, i: (j, i)), tile],
        [jax.ShapeDtypeStruct((attn_w, s), BF16), jax.ShapeDtypeStruct((s, attn_w), F32)],
        [(w_in[0], 3, pair_w)], tm, tn)
    conv, w_out_b = _proj_conv(u, w_hb_b, w_cz_b, conv_w[0], conv_width, w_out[0])
    attn = _moba_attention(q_t, ks, v_t, kmean.reshape(s // BLOCK, attn_w), sz, rel_bias)
    out = _out_proj(attn, conv, w_out_b, x2, final_gain)
    return out[None]
```

```python
import functools
import math

import numpy as np
import jax
import jax.numpy as jnp
from jax import lax
from jax.experimental import pallas as pl
from jax.experimental.pallas import tpu as pltpu

HEAD_DIM = 128
BLOCK = 256
TOPK = 3
NUM_BUCKETS = 32
MAX_DISTANCE = 128
EPS = 1e-6
NEG = -1e30
LOG2E = math.log2(math.e)

F32 = jnp.float32
BF16 = jnp.bfloat16

V7X_VMEM_LIMIT_BYTES = 56 * 1024 * 1024
V7X_OUT_PROJ_VMEM_LIMIT_BYTES = 60 * 1024 * 1024


def _silu(z):
    return z * (1.0 / (1.0 + jnp.exp(-z)))


def _rmsnorm_kernel(x_ref, g_ref, o_ref):
    x = x_ref[...]
    ms = jnp.mean(x * x, axis=-1, keepdims=True)
    o_ref[...] = (x * lax.rsqrt(ms + EPS) * g_ref[...]).astype(o_ref.dtype)


def _rmsnorm_bf16(x, g, tm=512):
    s, d = x.shape
    return pl.pallas_call(
        _rmsnorm_kernel,
        grid=(s // tm,),
        in_specs=[pl.BlockSpec((tm, d), lambda i: (i, 0)),
                  pl.BlockSpec((1, d), lambda i: (0, 0))],
        out_specs=pl.BlockSpec((tm, d), lambda i: (i, 0)),
        out_shape=jax.ShapeDtypeStruct((s, d), BF16),
        compiler_params=pltpu.CompilerParams(dimension_semantics=("arbitrary",)),
        name="rmsnorm_in",
    )(x, g.reshape(1, d))


def _run_side_jobs(side_refs, side_out_refs):
    for src, dst in zip(side_refs, side_out_refs):
        dst[...] = src[...].astype(dst.dtype)


def _side_cast_specs(side, col_block, cols, n_steps, step_of):
    rows = side.shape[0]
    chunk = rows // n_steps
    assert chunk * n_steps == rows and chunk % 16 == 0 and side.shape[1] % cols == 0
    return (pl.BlockSpec((chunk, cols), lambda j, i: (step_of(j, i), col_block)),
            pl.BlockSpec((chunk, cols), lambda j, i: (step_of(j, i), 0)),
            jax.ShapeDtypeStruct((rows, cols), BF16))


def _proj_qk_kernel(u_ref, wq_ref, wk_ref, *refs, scale, n_side):
    side_refs, (q_ref, ks_ref, km_ref), side_out_refs = refs[:n_side], refs[n_side:n_side + 3], refs[n_side + 3:]
    _run_side_jobs(side_refs, side_out_refs)
    u = u_ref[...]
    tm = u.shape[0]
    q = jnp.dot(u, wq_ref[...], preferred_element_type=F32)
    q_ref[...] = q.T.astype(BF16)
    k = jnp.dot(u, wk_ref[...], preferred_element_type=F32)
    ks_ref[...] = (k * scale).astype(BF16)
    km_ref[0] = jnp.mean(k.reshape(tm // BLOCK, BLOCK, k.shape[1]), axis=1)


def _proj_vz_kernel(u_ref, wv_ref, wz_ref, *refs, n_side):
    side_refs, (v_ref, sz_ref), side_out_refs = refs[:n_side], refs[n_side:n_side + 2], refs[n_side + 2:]
    _run_side_jobs(side_refs, side_out_refs)
    u = u_ref[...]
    z = jnp.dot(u, wz_ref[...], preferred_element_type=F32)
    sz_ref[...] = _silu(z)
    v = jnp.dot(u, wv_ref[...], preferred_element_type=F32)
    v_ref[...] = v.T.astype(BF16)


def _proj_pair(kernel_fn, name, u, w, width, out_specs, out_shape, side_jobs, tm, tn):
    s, d = u.shape
    nj = width // tn
    ni = s // tm
    jobs = [_side_cast_specs(m, cb, cols, nj * ni, lambda j, i: j * ni + i) for m, cb, cols in side_jobs]
    w_spec = lambda part: pl.BlockSpec((d, tn), lambda j, i, part=part: (0, part * nj + j))
    return pl.pallas_call(
        functools.partial(kernel_fn, n_side=len(jobs)),
        grid=(nj, ni),
        in_specs=[pl.BlockSpec((tm, d), lambda j, i: (i, 0)), w_spec(0), w_spec(1)] + [jb[0] for jb in jobs],
        out_specs=out_specs + [jb[1] for jb in jobs],
        out_shape=out_shape + [jb[2] for jb in jobs],
        compiler_params=pltpu.CompilerParams(
            dimension_semantics=("arbitrary", "arbitrary"),
            vmem_limit_bytes=V7X_VMEM_LIMIT_BYTES),
        name=name,
    )(u, w, w, *[m for m, _, _ in side_jobs])


def _proj_conv_kernel(u_ref, wh_ref, wb_ref, wc_ref, wz_ref, cw_ref, side_ref,
                      o_ref, side_out_ref, carry_ref):
    _run_side_jobs([side_ref], [side_out_ref])
    @pl.when(pl.program_id(1) == 0)
    def _():
        carry_ref[...] = jnp.zeros_like(carry_ref)

    u = u_ref[...]
    tm = u.shape[0]
    hc = jnp.dot(u, wh_ref[...], preferred_element_type=F32)
    c = jnp.dot(u, wc_ref[...], preferred_element_type=F32)
    p = c * hc
    rows = lax.broadcasted_iota(jnp.int32, p.shape, 0)
    prev = carry_ref[...]
    prev1 = prev[7:8, :]
    prev2 = prev[6:7, :]
    p1 = jnp.where(rows == 0, prev1, pltpu.roll(p, 1, 0))
    p2 = jnp.where(rows == 0, prev2, jnp.where(rows == 1, prev1, pltpu.roll(p, 2, 0)))
    carry_ref[...] = p[tm - 8:, :]
    cw = cw_ref[...]
    y = p2 * cw[0:1, :] + p1 * cw[1:2, :] + p * cw[2:3, :]
    z = jnp.dot(u, wz_ref[...], preferred_element_type=F32)
    gate = _silu(z)
    b = jnp.dot(u, wb_ref[...], preferred_element_type=F32)
    o_ref[...] = (b * y * gate).astype(o_ref.dtype)


def _proj_conv(u, w_hb, w_cz, conv_w, conv_width, side, tm=1024, tn=256):
    s, d = u.shape
    nj = conv_width // tn
    ni = s // tm
    side_in, side_out, side_shape = _side_cast_specs(
        side, 0, side.shape[1], nj * ni, lambda j, i: j * ni + i)
    w_spec = lambda part: pl.BlockSpec((d, tn), lambda j, i, part=part: (0, part * nj + j))
    return pl.pallas_call(
        _proj_conv_kernel,
        grid=(nj, ni),
        in_specs=[pl.BlockSpec((tm, d), lambda j, i: (i, 0)),
                  w_spec(0), w_spec(1), w_spec(0), w_spec(1),
                  pl.BlockSpec((conv_w.shape[0], tn), lambda j, i: (0, j)), side_in],
        out_specs=[pl.BlockSpec((tm, tn), lambda j, i: (i, j)), side_out],
        out_shape=[jax.ShapeDtypeStruct((s, conv_width), BF16), side_shape],
        scratch_shapes=[pltpu.VMEM((8, tn), F32)],
        compiler_params=pltpu.CompilerParams(
            dimension_semantics=("arbitrary", "arbitrary"),
            vmem_limit_bytes=V7X_VMEM_LIMIT_BYTES),
        name="proj_conv",
    )(u, w_hb, w_hb, w_cz, w_cz, conv_w, side)


def _rel_bucket_np(dist):
    n = np.maximum(dist, 0)
    max_exact = NUM_BUCKETS // 2
    nf = np.maximum(n, 1).astype(np.float32)
    ratio = np.log(nf / np.float32(max_exact)) / np.float32(math.log(MAX_DISTANCE / max_exact))
    large = max_exact + (ratio * np.float32(NUM_BUCKETS - max_exact)).astype(np.int32)
    large = np.minimum(large, NUM_BUCKETS - 1)
    return np.where(n < max_exact, n, large).astype(np.int32)


def _bucket_tables():
    key = np.arange(BLOCK)[:, None]
    qry = np.arange(BLOCK)[None, :]
    d_own = qry - key
    own = np.where(d_own >= 0, _rel_bucket_np(d_own), NUM_BUCKETS).astype(np.int32)
    adj = _rel_bucket_np(BLOCK + qry - key)
    return own, adj


GROUP = 4
QPAIR = 2


def _attn_kernel(relb_ref, idx_own_ref, idx_adj_ref, q_ref, ks_ref, vt_ref, km_ref, sz_ref,
                 o_ref,
                 kaug_ref, bown_ref, badj_ref, qat_ref, pen_ref, m_ref, l_ref, acc_ref,
                 sa_ref, sb_ref, *, nb):
    h = pl.program_id(0)
    gk = GROUP * BLOCK
    pw = QPAIR * BLOCK
    dot = functools.partial(jnp.dot, preferred_element_type=F32)

    @pl.when(h == 0)
    def _():
        lane = lax.broadcasted_iota(jnp.int32, (BLOCK, HEAD_DIM), 1)
        for blk in range(nb):
            r0 = (blk % GROUP) * BLOCK
            kaug_ref[blk // GROUP, r0:r0 + BLOCK, HEAD_DIM:] = jnp.where(lane == blk, 1.0, 0.0).astype(BF16)

    far_bias = relb_ref[NUM_BUCKETS - 1, h]
    io = idx_own_ref[...]
    ia = idx_adj_ref[...]
    own = jnp.full((BLOCK, BLOCK), NEG, F32)
    adj = jnp.zeros((BLOCK, BLOCK), F32)
    for b in range(NUM_BUCKETS):
        t = (relb_ref[b, h] - far_bias) * LOG2E
        own = jnp.where(io == b, t, own)
        adj = jnp.where(ia == b, t, adj)
    bown_ref[...] = own
    badj_ref[...] = adj

    def stage(g, carry):
        rows = pl.ds(pl.multiple_of(g * gk, gk), gk)
        kaug_ref[g, :, :HEAD_DIM] = ks_ref[rows, :]
        return carry

    lax.fori_loop(0, nb // GROUP, stage, 0)

    km = km_ref[...]
    km_hi = km.astype(BF16)
    km_lo = (km - km_hi.astype(F32)).astype(BF16)

    ROUTE_CLASSES = 8
    rq = ROUTE_CLASSES * gk

    def route(cc):
        q_t = q_ref[:, cc * rq:(cc + 1) * rq]
        gate = dot(km_hi, q_t) + dot(km_lo, q_t)
        blk_id = lax.broadcasted_iota(jnp.int32, gate.shape, 0)
        qb = cc * (ROUTE_CLASSES * GROUP) + lax.shift_right_logical(
            lax.broadcasted_iota(jnp.int32, gate.shape, 1), int(math.log2(BLOCK)))
        gate = jnp.where(blk_id < qb, gate, -jnp.inf)
        sel = jnp.zeros(gate.shape, jnp.bool_)
        for r in range(min(TOPK, nb)):
            mx = jnp.max(gate, axis=0, keepdims=True)
            first = jnp.min(jnp.where(gate == mx, blk_id, nb), axis=0, keepdims=True)
            pick = (blk_id == first) & (qb > r)
            sel = sel | pick
            gate = jnp.where(pick, -jnp.inf, gate)
        sel_far = jnp.where(sel & (blk_id < qb - 1), 0.0, NEG)
        sel_rows = jnp.concatenate(
            [sel_far, jnp.zeros((HEAD_DIM - nb, rq), F32)], axis=0).astype(BF16)
        adj_sel = jnp.max(jnp.where(sel & (blk_id == qb - 1), 1.0, 0.0), axis=0, keepdims=True)
        pen = jnp.where(adj_sel > 0.0, 0.0, NEG)
        for t in range(rq // pw):
            cols = slice(t * pw, (t + 1) * pw)
            qat_ref[cc * (rq // pw) + t, :HEAD_DIM, :] = q_t[:, cols]
            qat_ref[cc * (rq // pw) + t, HEAD_DIM:, :] = sel_rows[:, cols]
        for j in range(rq // BLOCK):
            cols = slice(j * BLOCK, (j + 1) * BLOCK)
            pen_ref[cc * (rq // BLOCK) + j] = jnp.broadcast_to(pen[:, cols], (8, BLOCK))

    for cc in range(nb // (ROUTE_CLASSES * GROUP)):
        route(cc)

    def colmax(s):
        return jnp.max(s, axis=0, keepdims=True)

    def colsum(p):
        return jnp.sum(p, axis=0, keepdims=True)

    def pv(g, first, p):
        keys = slice((g * GROUP + first) * BLOCK, (g * GROUP + first) * BLOCK + p.shape[0])
        return dot(vt_ref[:, keys], p.astype(BF16))

    def do_class(c, carry):
        def far_logits(g, t):
            return dot(kaug_ref[g], qat_ref[c * (GROUP // QPAIR) + t])

        def far_park(g, t, s_ref):
            s = far_logits(g, t)
            s_ref[...] = s
            return colmax(s)

        lanes_of = [slice((j % QPAIR) * BLOCK, (j % QPAIR + 1) * BLOCK) for j in range(GROUP)]
        near = []
        for j in range(GROUP):
            qb = c * GROUP + j
            pair = c * (GROUP // QPAIR) + j // QPAIR
            q_t = qat_ref[pair, :HEAD_DIM, lanes_of[j]]
            parts = [(dot(kaug_ref[c, j * BLOCK:(j + 1) * BLOCK, :HEAD_DIM], q_t) + bown_ref[...], c, j)]
            if qb > 0:
                g_adj, t_adj = (c, j - 1) if j > 0 else (c - 1, GROUP - 1)
                k_adj = kaug_ref[g_adj, t_adj * BLOCK:(t_adj + 1) * BLOCK, :HEAD_DIM]
                parts.append((dot(k_adj, q_t) + (badj_ref[...] + pen_ref[qb, 0:1, :]), g_adj, t_adj))
            if j >= 2:
                parts.append((dot(kaug_ref[c, :(j - 1) * BLOCK, :], qat_ref[pair, :, lanes_of[j]]), c, 0))
            near.append(parts)

        if c > 0:
            cmax_a = far_park(0, 0, sa_ref)

        for j, parts in enumerate(near):
            m = functools.reduce(jnp.maximum, [colmax(s) for s, _, _ in parts])
            probs = [(jnp.exp2(s - m), g, first) for s, g, first in parts]
            m_ref[j // QPAIR, :, lanes_of[j]] = m
            l_ref[j // QPAIR, :, lanes_of[j]] = functools.reduce(jnp.add, [colsum(p) for p, _, _ in probs])
            acc_ref[j // QPAIR, :, lanes_of[j]] = functools.reduce(
                jnp.add, [pv(g, first, p) for p, g, first in probs])

        def far_update(g, t, s_ref, cmax):
            m_old = m_ref[t]
            m_new = jnp.maximum(m_old, cmax)
            alpha = jnp.exp2(m_old - m_new)
            l_new = alpha * l_ref[t]
            acc_new = alpha * acc_ref[t]
            for k in range(GROUP):
                p = jnp.exp2(s_ref[k * BLOCK:(k + 1) * BLOCK, :] - m_new)
                l_new = l_new + colsum(p)
                acc_new = acc_new + pv(g, k, p)
            l_ref[t] = l_new
            acc_ref[t] = acc_new
            m_ref[t] = m_new

        for g in range(c):
            cmax_b = far_park(g, 1, sb_ref)
            far_update(g, 0, sa_ref, cmax_a)
            if g + 1 < c:
                cmax_a = far_park(g + 1, 0, sa_ref)
            far_update(g, 1, sb_ref, cmax_b)

        for j in range(GROUP):
            rows = pl.ds((c * GROUP + j) * BLOCK, BLOCK)
            out_t = acc_ref[j // QPAIR, :, lanes_of[j]] * (1.0 / l_ref[j // QPAIR, :, lanes_of[j]])
            o_ref[rows, :] = (out_t.T * sz_ref[rows, :]).astype(o_ref.dtype)
        return carry

    one = jnp.minimum(h + 1, 1)
    for c in range(nb // GROUP):
        lax.fori_loop(0, one, lambda _, carry, c=c: do_class(c, carry), 0)


def _moba_attention(q_t, ks, v_t, kmean, sz, rel_bias):
    s, attn_w = ks.shape
    nh = attn_w // HEAD_DIM
    nb = s // BLOCK
    assert nb % GROUP == 0 and nb <= HEAD_DIM
    idx_own, idx_adj = _bucket_tables()
    head_cols = pl.BlockSpec((s, HEAD_DIM), lambda h: (0, h))
    const_tile = pl.BlockSpec((BLOCK, BLOCK), lambda h: (0, 0))
    return pl.pallas_call(
        functools.partial(_attn_kernel, nb=nb),
        grid=(nh,),
        in_specs=[pl.BlockSpec(memory_space=pltpu.SMEM),
                  const_tile, const_tile,
                  pl.BlockSpec((HEAD_DIM, s), lambda h: (h, 0)),
                  head_cols,
                  pl.BlockSpec((HEAD_DIM, s), lambda h: (h, 0)),
                  pl.BlockSpec((nb, HEAD_DIM), lambda h: (0, h)),
                  head_cols],
        out_specs=head_cols,
        out_shape=jax.ShapeDtypeStruct((s, attn_w), BF16),
        scratch_shapes=[pltpu.VMEM((nb // GROUP, GROUP * BLOCK, 2 * HEAD_DIM), BF16),
                        pltpu.VMEM((BLOCK, BLOCK), F32),
                        pltpu.VMEM((BLOCK, BLOCK), F32),
                        pltpu.VMEM((nb // QPAIR, 2 * HEAD_DIM, QPAIR * BLOCK), BF16),
                        pltpu.VMEM((nb, 8, BLOCK), F32),
                        pltpu.VMEM((GROUP // QPAIR, 1, QPAIR * BLOCK), F32),
                        pltpu.VMEM((GROUP // QPAIR, 1, QPAIR * BLOCK), F32),
                        pltpu.VMEM((GROUP // QPAIR, HEAD_DIM, QPAIR * BLOCK), F32),
                        pltpu.VMEM((GROUP * BLOCK, QPAIR * BLOCK), F32),
                        pltpu.VMEM((GROUP * BLOCK, QPAIR * BLOCK), F32)],
        compiler_params=pltpu.CompilerParams(
            dimension_semantics=("arbitrary",),
            vmem_limit_bytes=V7X_VMEM_LIMIT_BYTES),
        name="moba_attn",
    )(rel_bias, jnp.asarray(idx_own), jnp.asarray(idx_adj), q_t, ks, v_t, kmean, sz)


def _out_kernel(a_ref, c_ref, wa_ref, wc_ref, x_ref, g_ref, o_ref, *, n_chunks):
    d = o_ref.shape[1]
    w = d // n_chunks
    a = a_ref[...]
    c = c_ref[...]
    hs = []
    ss = jnp.zeros((o_ref.shape[0], 1), F32)
    for n in range(n_chunks):
        cols = slice(n * w, (n + 1) * w)
        h = (x_ref[:, cols]
             + jnp.dot(a, wa_ref[:, cols], preferred_element_type=F32)
             + jnp.dot(c, wc_ref[:, cols], preferred_element_type=F32))
        ss = ss + jnp.sum(h * h, axis=-1, keepdims=True)
        hs.append(h)
    inv = lax.rsqrt(ss / d + EPS)
    for n, h in enumerate(hs):
        cols = slice(n * w, (n + 1) * w)
        o_ref[:, cols] = h * inv * g_ref[:, cols]


def _out_proj(attn, conv, w_out, x, g, tm=256):
    s, d = x.shape
    ka = attn.shape[1]
    kc = conv.shape[1]
    assert ka == kc
    resident = pl.Buffered(1)
    return pl.pallas_call(
        functools.partial(_out_kernel, n_chunks=4),
        grid=(s // tm,),
        in_specs=[pl.BlockSpec((tm, ka), lambda i: (i, 0)),
                  pl.BlockSpec((tm, kc), lambda i: (i, 0)),
                  pl.BlockSpec((ka, d), lambda i: (0, 0), pipeline_mode=resident),
                  pl.BlockSpec((kc, d), lambda i: (1, 0), pipeline_mode=resident),
                  pl.BlockSpec((tm, d), lambda i: (i, 0)),
                  pl.BlockSpec((1, d), lambda i: (0, 0))],
        out_specs=pl.BlockSpec((tm, d), lambda i: (i, 0)),
        out_shape=jax.ShapeDtypeStruct((s, d), F32),
        compiler_params=pltpu.CompilerParams(
            dimension_semantics=("arbitrary",),
            vmem_limit_bytes=V7X_OUT_PROJ_VMEM_LIMIT_BYTES),
        name="out_proj_norm",
    )(attn, conv, w_out, w_out, x, g.reshape(1, d))


def kernel(x, norm_gain, w_in, conv_w, w_out, rel_bias, final_gain):
    b, s, d = x.shape
    depth = norm_gain.shape[0]
    assert b == 1 and depth == 1, "kernel is written for one sequence and one layer"
    attn_w = d // 2
    conv_width = d - attn_w
    assert rel_bias.shape == (NUM_BUCKETS, attn_w // HEAD_DIM)
    assert w_in.shape[2] == 4 * attn_w + 4 * conv_width

    x2 = x[0]
    assert attn_w == conv_width
    pair_w = 2 * attn_w
    w_qk_b = w_in[0, :, :pair_w].astype(BF16)

    u = _rmsnorm_bf16(x2, norm_gain[0])
    tm, tn = 1024, 512
    tile = pl.BlockSpec((tm, tn), lambda j, i: (i, j))
    act = jax.ShapeDtypeStruct((s, attn_w), BF16)
    tile_t = pl.BlockSpec((tn, tm), lambda j, i: (j, i))
    act_t = jax.ShapeDtypeStruct((attn_w, s), BF16)
    q_t, ks, kmean, w_vz_b, w_hb_b = _proj_pair(
        functools.partial(_proj_qk_kernel, scale=HEAD_DIM ** -0.5 * LOG2E), "proj_qk", u, w_qk_b, attn_w,
        [tile_t, tile, pl.BlockSpec((1, tm // BLOCK, tn), lambda j, i: (i, 0, j))],
        [act_t, act, jax.ShapeDtypeStruct((s // tm, tm // BLOCK, attn_w), F32)],
        [(w_in[0], 1, pair_w), (w_in[0], 2, pair_w)], tm, tn)
    v_t, sz, w_cz_b = _proj_pair(
        _proj_vz_kernel, "proj_vz", u, w_vz_b, attn_w,
        [pl.BlockSpec((tn, tm), lambda j, i: (j, i)), tile],
        [jax.ShapeDtypeStruct((attn_w, s), BF16), jax.ShapeDtypeStruct((s, attn_w), F32)],
        [(w_in[0], 3, pair_w)], tm, tn)
    conv, w_out_b = _proj_conv(u, w_hb_b, w_cz_b, conv_w[0], conv_width, w_out[0])
    attn = _moba_attention(q_t, ks, v_t, kmean.reshape(s // BLOCK, attn_w), sz, rel_bias)
    out = _out_proj(attn, conv, w_out_b, x2, final_gain)
    return out[None]
```

```python
import functools
import math

import numpy as np
import jax
import jax.numpy as jnp
from jax import lax
from jax.experimental import pallas as pl
from jax.experimental.pallas import tpu as pltpu

HEAD_DIM = 128
BLOCK = 256
TOPK = 3
NUM_BUCKETS = 32
MAX_DISTANCE = 128
EPS = 1e-6
NEG = -1e30
LOG2E = math.log2(math.e)

F32 = jnp.float32
BF16 = jnp.bfloat16

V7X_VMEM_LIMIT_BYTES = 56 * 1024 * 1024
V7X_VMEM_LIMIT_HIGH_BYTES = 60 * 1024 * 1024


def _silu(z):
    return z * (1.0 / (1.0 + jnp.exp(-z)))


def _rmsnorm_kernel(x_ref, g_ref, o_ref):
    x = x_ref[...]
    ms = jnp.mean(x * x, axis=-1, keepdims=True)
    o_ref[...] = (x * lax.rsqrt(ms + EPS) * g_ref[...]).astype(o_ref.dtype)


def _rmsnorm_bf16(x, g, tm=512):
    s, d = x.shape
    return pl.pallas_call(
        _rmsnorm_kernel,
        grid=(s // tm,),
        in_specs=[pl.BlockSpec((tm, d), lambda i: (i, 0)),
                  pl.BlockSpec((1, d), lambda i: (0, 0))],
        out_specs=pl.BlockSpec((tm, d), lambda i: (i, 0)),
        out_shape=jax.ShapeDtypeStruct((s, d), BF16),
        compiler_params=pltpu.CompilerParams(dimension_semantics=("arbitrary",)),
        name="rmsnorm_in",
    )(x, g.reshape(1, d))


def _run_side_jobs(side_refs, side_out_refs):
    for src, dst in zip(side_refs, side_out_refs):
        dst[...] = src[...].astype(dst.dtype)


def _side_cast_specs(side, col_block, cols, n_steps, step_of):
    rows = side.shape[0]
    chunk = rows // n_steps
    assert chunk * n_steps == rows and chunk % 16 == 0 and side.shape[1] % cols == 0
    return (pl.BlockSpec((chunk, cols), lambda j, i: (step_of(j, i), col_block)),
            pl.BlockSpec((chunk, cols), lambda j, i: (step_of(j, i), 0)),
            jax.ShapeDtypeStruct((rows, cols), BF16))


def _proj_qk_kernel(u_ref, wq_ref, wk_ref, *refs, scale, n_side):
    side_refs, (q_ref, ks_ref, km_ref), side_out_refs = refs[:n_side], refs[n_side:n_side + 3], refs[n_side + 3:]
    _run_side_jobs(side_refs, side_out_refs)
    u = u_ref[...]
    tm = u.shape[0]
    q = jnp.dot(u, wq_ref[...], preferred_element_type=F32)
    q_ref[...] = q.T.astype(BF16)
    k = jnp.dot(u, wk_ref[...], preferred_element_type=F32)
    ks = (k * scale).astype(BF16)
    blk0 = pl.program_id(1) * (tm // BLOCK)
    row_blk = blk0 + lax.shift_right_logical(
        lax.broadcasted_iota(jnp.int32, (tm, HEAD_DIM), 0), int(math.log2(BLOCK)))
    lane = lax.broadcasted_iota(jnp.int32, (tm, HEAD_DIM), 1)
    one_hot = jnp.where(lane == row_blk, 1.0, 0.0).astype(BF16)
    for hh in range(ks.shape[1] // HEAD_DIM):
        ks_ref[:, 2 * hh * HEAD_DIM:(2 * hh + 1) * HEAD_DIM] = ks[:, hh * HEAD_DIM:(hh + 1) * HEAD_DIM]
        ks_ref[:, (2 * hh + 1) * HEAD_DIM:(2 * hh + 2) * HEAD_DIM] = one_hot
    km_ref[0] = jnp.mean(k.reshape(tm // BLOCK, BLOCK, k.shape[1]), axis=1)


def _proj_vz_kernel(u_ref, wv_ref, wz_ref, *refs, n_side):
    side_refs, (v_ref, sz_ref), side_out_refs = refs[:n_side], refs[n_side:n_side + 2], refs[n_side + 2:]
    _run_side_jobs(side_refs, side_out_refs)
    u = u_ref[...]
    z = jnp.dot(u, wz_ref[...], preferred_element_type=F32)
    sz_ref[...] = _silu(z)
    v = jnp.dot(u, wv_ref[...], preferred_element_type=F32)
    v_ref[...] = v.T.astype(BF16)


def _proj_pair(kernel_fn, name, u, w, width, out_specs, out_shape, side_jobs, tm, tn):
    s, d = u.shape
    nj = width // tn
    ni = s // tm
    jobs = [_side_cast_specs(m, cb, cols, nj * ni, lambda j, i: j * ni + i) for m, cb, cols in side_jobs]
    w_spec = lambda part: pl.BlockSpec((d, tn), lambda j, i, part=part: (0, part * nj + j))
    return pl.pallas_call(
        functools.partial(kernel_fn, n_side=len(jobs)),
        grid=(nj, ni),
        in_specs=[pl.BlockSpec((tm, d), lambda j, i: (i, 0)), w_spec(0), w_spec(1)] + [jb[0] for jb in jobs],
        out_specs=out_specs + [jb[1] for jb in jobs],
        out_shape=out_shape + [jb[2] for jb in jobs],
        compiler_params=pltpu.CompilerParams(
            dimension_semantics=("arbitrary", "arbitrary"),
            vmem_limit_bytes=V7X_VMEM_LIMIT_HIGH_BYTES),
        name=name,
    )(u, w, w, *[m for m, _, _ in side_jobs])


def _proj_conv_kernel(u_ref, wh_ref, wb_ref, wc_ref, wz_ref, cw_ref, side_ref,
                      o_ref, side_out_ref, carry_ref):
    _run_side_jobs([side_ref], [side_out_ref])
    @pl.when(pl.program_id(1) == 0)
    def _():
        carry_ref[...] = jnp.zeros_like(carry_ref)

    u = u_ref[...]
    tm = u.shape[0]
    hc = jnp.dot(u, wh_ref[...], preferred_element_type=F32)
    c = jnp.dot(u, wc_ref[...], preferred_element_type=F32)
    p = c * hc
    rows = lax.broadcasted_iota(jnp.int32, p.shape, 0)
    prev = carry_ref[...]
    prev1 = prev[7:8, :]
    prev2 = prev[6:7, :]
    p1 = jnp.where(rows == 0, prev1, pltpu.roll(p, 1, 0))
    p2 = jnp.where(rows == 0, prev2, jnp.where(rows == 1, prev1, pltpu.roll(p, 2, 0)))
    carry_ref[...] = p[tm - 8:, :]
    cw = cw_ref[...]
    y = p2 * cw[0:1, :] + p1 * cw[1:2, :] + p * cw[2:3, :]
    z = jnp.dot(u, wz_ref[...], preferred_element_type=F32)
    gate = _silu(z)
    b = jnp.dot(u, wb_ref[...], preferred_element_type=F32)
    o_ref[...] = (b * y * gate).astype(o_ref.dtype)


def _proj_conv(u, w_hb, w_cz, conv_w, conv_width, side, tm=1024, tn=256):
    s, d = u.shape
    nj = conv_width // tn
    ni = s // tm
    side_in, side_out, side_shape = _side_cast_specs(
        side, 0, side.shape[1], nj * ni, lambda j, i: j * ni + i)
    w_spec = lambda part: pl.BlockSpec((d, tn), lambda j, i, part=part: (0, part * nj + j))
    return pl.pallas_call(
        _proj_conv_kernel,
        grid=(nj, ni),
        in_specs=[pl.BlockSpec((tm, d), lambda j, i: (i, 0)),
                  w_spec(0), w_spec(1), w_spec(0), w_spec(1),
                  pl.BlockSpec((conv_w.shape[0], tn), lambda j, i: (0, j)), side_in],
        out_specs=[pl.BlockSpec((tm, tn), lambda j, i: (i, j)), side_out],
        out_shape=[jax.ShapeDtypeStruct((s, conv_width), BF16), side_shape],
        scratch_shapes=[pltpu.VMEM((8, tn), F32)],
        compiler_params=pltpu.CompilerParams(
            dimension_semantics=("arbitrary", "arbitrary"),
            vmem_limit_bytes=V7X_VMEM_LIMIT_BYTES),
        name="proj_conv",
    )(u, w_hb, w_hb, w_cz, w_cz, conv_w, side)


def _rel_bucket_np(dist):
    n = np.maximum(dist, 0)
    max_exact = NUM_BUCKETS // 2
    nf = np.maximum(n, 1).astype(np.float32)
    ratio = np.log(nf / np.float32(max_exact)) / np.float32(math.log(MAX_DISTANCE / max_exact))
    large = max_exact + (ratio * np.float32(NUM_BUCKETS - max_exact)).astype(np.int32)
    large = np.minimum(large, NUM_BUCKETS - 1)
    return np.where(n < max_exact, n, large).astype(np.int32)


def _bucket_tables():
    key = np.arange(BLOCK)[:, None]
    qry = np.arange(BLOCK)[None, :]
    d_own = qry - key
    own = np.where(d_own >= 0, _rel_bucket_np(d_own), NUM_BUCKETS).astype(np.int32)
    adj = _rel_bucket_np(BLOCK + qry - key)
    return own, adj


GROUP = 4
QPAIR = 2


def _attn_kernel(relb_ref, idx_own_ref, idx_adj_ref, q_ref, kaug_ref, vt_ref, km_ref, sz_ref,
                 o_ref,
                 bown_ref, badj_ref, qat_ref, pen_ref, m_ref, l_ref, acc_ref,
                 sa_ref, sb_ref, *, nb):
    h = pl.program_id(0)
    gk = GROUP * BLOCK
    pw = QPAIR * BLOCK
    dot = functools.partial(jnp.dot, preferred_element_type=F32)

    far_bias = relb_ref[NUM_BUCKETS - 1, h]
    io = idx_own_ref[...]
    ia = idx_adj_ref[...]
    own = jnp.full((BLOCK, BLOCK), NEG, F32)
    adj = jnp.zeros((BLOCK, BLOCK), F32)
    for b in range(NUM_BUCKETS):
        t = (relb_ref[b, h] - far_bias) * LOG2E
        own = jnp.where(io == b, t, own)
        adj = jnp.where(ia == b, t, adj)
    bown_ref[...] = own
    badj_ref[...] = adj

    km = km_ref[...]
    km_hi = km.astype(BF16)
    km_lo = (km - km_hi.astype(F32)).astype(BF16)

    ROUTE_CLASSES = 8
    rq = ROUTE_CLASSES * gk

    def route(cc):
        q_t = q_ref[:, cc * rq:(cc + 1) * rq]
        gate = dot(km_hi, q_t) + dot(km_lo, q_t)
        blk_id = lax.broadcasted_iota(jnp.int32, gate.shape, 0)
        qb = cc * (ROUTE_CLASSES * GROUP) + lax.shift_right_logical(
            lax.broadcasted_iota(jnp.int32, gate.shape, 1), int(math.log2(BLOCK)))
        gate = jnp.where(blk_id < qb, gate, -jnp.inf)
        sel = jnp.zeros(gate.shape, jnp.bool_)
        for r in range(min(TOPK, nb)):
            mx = jnp.max(gate, axis=0, keepdims=True)
            first = jnp.min(jnp.where(gate == mx, blk_id, nb), axis=0, keepdims=True)
            pick = (blk_id == first) & (qb > r)
            sel = sel | pick
            gate = jnp.where(pick, -jnp.inf, gate)
        sel_far = jnp.where(sel & (blk_id < qb - 1), 0.0, NEG)
        sel_rows = jnp.concatenate(
            [sel_far, jnp.zeros((HEAD_DIM - nb, rq), F32)], axis=0).astype(BF16)
        adj_sel = jnp.max(jnp.where(sel & (blk_id == qb - 1), 1.0, 0.0), axis=0, keepdims=True)
        pen = jnp.where(adj_sel > 0.0, 0.0, NEG)
        for t in range(rq // pw):
            cols = slice(t * pw, (t + 1) * pw)
            qat_ref[cc * (rq // pw) + t, :HEAD_DIM, :] = q_t[:, cols]
            qat_ref[cc * (rq // pw) + t, HEAD_DIM:, :] = sel_rows[:, cols]
        for j in range(rq // BLOCK):
            cols = slice(j * BLOCK, (j + 1) * BLOCK)
            pen_ref[cc * (rq // BLOCK) + j] = jnp.broadcast_to(pen[:, cols], (8, BLOCK))

    for cc in range(nb // (ROUTE_CLASSES * GROUP)):
        route(cc)

    def colmax(s):
        return jnp.max(s, axis=0, keepdims=True)

    def colsum(p):
        return jnp.sum(p, axis=0, keepdims=True)

    def pv(g, first, p):
        keys = slice((g * GROUP + first) * BLOCK, (g * GROUP + first) * BLOCK + p.shape[0])
        return dot(vt_ref[:, keys], p.astype(BF16))

    def do_class(c, carry):
        def far_logits(g, t):
            return dot(kaug_ref[g], qat_ref[c * (GROUP // QPAIR) + t])

        def far_park(g, t, s_ref):
            s = far_logits(g, t)
            s_ref[...] = s
            return colmax(s)

        lanes_of = [slice((j % QPAIR) * BLOCK, (j % QPAIR + 1) * BLOCK) for j in range(GROUP)]
        near = []
        for j in range(GROUP):
            qb = c * GROUP + j
            pair = c * (GROUP // QPAIR) + j // QPAIR
            q_t = qat_ref[pair, :HEAD_DIM, lanes_of[j]]
            parts = [(dot(kaug_ref[c, j * BLOCK:(j + 1) * BLOCK, :HEAD_DIM], q_t) + bown_ref[...], c, j)]
            if qb > 0:
                g_adj, t_adj = (c, j - 1) if j > 0 else (c - 1, GROUP - 1)
                k_adj = kaug_ref[g_adj, t_adj * BLOCK:(t_adj + 1) * BLOCK, :HEAD_DIM]
                parts.append((dot(k_adj, q_t) + (badj_ref[...] + pen_ref[qb, 0:1, :]), g_adj, t_adj))
            if j >= 2:
                parts.append((dot(kaug_ref[c, :(j - 1) * BLOCK, :], qat_ref[pair, :, lanes_of[j]]), c, 0))
            near.append(parts)

        if c > 0:
            cmax_a = far_park(0, 0, sa_ref)

        for j, parts in enumerate(near):
            m = functools.reduce(jnp.maximum, [colmax(s) for s, _, _ in parts])
            probs = [(jnp.exp2(s - m), g, first) for s, g, first in parts]
            m_ref[j // QPAIR, :, lanes_of[j]] = m
            l_ref[j // QPAIR, :, lanes_of[j]] = functools.reduce(jnp.add, [colsum(p) for p, _, _ in probs])
            acc_ref[j // QPAIR, :, lanes_of[j]] = functools.reduce(
                jnp.add, [pv(g, first, p) for p, g, first in probs])

        def far_update(g, t, s_ref, cmax):
            m_old = m_ref[t]
            m_new = jnp.maximum(m_old, cmax)
            alpha = jnp.exp2(m_old - m_new)
            l_new = alpha * l_ref[t]
            acc_new = alpha * acc_ref[t]
            for k in range(GROUP):
                p = jnp.exp2(s_ref[k * BLOCK:(k + 1) * BLOCK, :] - m_new)
                l_new = l_new + colsum(p)
                acc_new = acc_new + pv(g, k, p)
            l_ref[t] = l_new
            acc_ref[t] = acc_new
            m_ref[t] = m_new

        for g in range(c):
            cmax_b = far_park(g, 1, sb_ref)
            far_update(g, 0, sa_ref, cmax_a)
            if g + 1 < c:
                cmax_a = far_park(g + 1, 0, sa_ref)
            far_update(g, 1, sb_ref, cmax_b)

        for j in range(GROUP):
            rows = pl.ds((c * GROUP + j) * BLOCK, BLOCK)
            out_t = acc_ref[j // QPAIR, :, lanes_of[j]] * (1.0 / l_ref[j // QPAIR, :, lanes_of[j]])
            o_ref[rows, :] = (out_t.T * sz_ref[rows, :]).astype(o_ref.dtype)
        return carry

    one = jnp.minimum(h + 1, 1)
    for c in range(nb // GROUP):
        lax.fori_loop(0, one, lambda _, carry, c=c: do_class(c, carry), 0)


def _moba_attention(q_t, kaug, v_t, kmean, sz, rel_bias):
    attn_w, s = q_t.shape
    nh = attn_w // HEAD_DIM
    nb = s // BLOCK
    assert nb % GROUP == 0 and nb <= HEAD_DIM
    kaug = kaug.reshape(nb // GROUP, GROUP * BLOCK, nh * 2 * HEAD_DIM)
    idx_own, idx_adj = _bucket_tables()
    head_cols = pl.BlockSpec((s, HEAD_DIM), lambda h: (0, h))
    const_tile = pl.BlockSpec((BLOCK, BLOCK), lambda h: (0, 0))
    return pl.pallas_call(
        functools.partial(_attn_kernel, nb=nb),
        grid=(nh,),
        in_specs=[pl.BlockSpec(memory_space=pltpu.SMEM),
                  const_tile, const_tile,
                  pl.BlockSpec((HEAD_DIM, s), lambda h: (h, 0)),
                  pl.BlockSpec((nb // GROUP, GROUP * BLOCK, 2 * HEAD_DIM), lambda h: (0, 0, h)),
                  pl.BlockSpec((HEAD_DIM, s), lambda h: (h, 0)),
                  pl.BlockSpec((nb, HEAD_DIM), lambda h: (0, h)),
                  head_cols],
        out_specs=head_cols,
        out_shape=jax.ShapeDtypeStruct((s, attn_w), BF16),
        scratch_shapes=[pltpu.VMEM((BLOCK, BLOCK), F32),
                        pltpu.VMEM((BLOCK, BLOCK), F32),
                        pltpu.VMEM((nb // QPAIR, 2 * HEAD_DIM, QPAIR * BLOCK), BF16),
                        pltpu.VMEM((nb, 8, BLOCK), F32),
                        pltpu.VMEM((GROUP // QPAIR, 1, QPAIR * BLOCK), F32),
                        pltpu.VMEM((GROUP // QPAIR, 1, QPAIR * BLOCK), F32),
                        pltpu.VMEM((GROUP // QPAIR, HEAD_DIM, QPAIR * BLOCK), F32),
                        pltpu.VMEM((GROUP * BLOCK, QPAIR * BLOCK), F32),
                        pltpu.VMEM((GROUP * BLOCK, QPAIR * BLOCK), F32)],
        compiler_params=pltpu.CompilerParams(
            dimension_semantics=("arbitrary",),
            vmem_limit_bytes=V7X_VMEM_LIMIT_BYTES),
        name="moba_attn",
    )(rel_bias, jnp.asarray(idx_own), jnp.asarray(idx_adj), q_t, kaug, v_t, kmean, sz)


def _out_kernel(a_ref, c_ref, wa_ref, wc_ref, x_ref, g_ref, o_ref, *, n_chunks):
    d = o_ref.shape[1]
    w = d // n_chunks
    a = a_ref[...]
    c = c_ref[...]
    hs = []
    ss = jnp.zeros((o_ref.shape[0], 1), F32)
    for n in range(n_chunks):
        cols = slice(n * w, (n + 1) * w)
        h = (x_ref[:, cols]
             + jnp.dot(a, wa_ref[:, cols], preferred_element_type=F32)
             + jnp.dot(c, wc_ref[:, cols], preferred_element_type=F32))
        ss = ss + jnp.sum(h * h, axis=-1, keepdims=True)
        hs.append(h)
    inv = lax.rsqrt(ss / d + EPS)
    for n, h in enumerate(hs):
        cols = slice(n * w, (n + 1) * w)
        o_ref[:, cols] = h * inv * g_ref[:, cols]


def _out_proj(attn, conv, w_out, x, g, tm=256):
    s, d = x.shape
    ka = attn.shape[1]
    kc = conv.shape[1]
    assert ka == kc
    resident = pl.Buffered(1)
    return pl.pallas_call(
        functools.partial(_out_kernel, n_chunks=4),
        grid=(s // tm,),
        in_specs=[pl.BlockSpec((tm, ka), lambda i: (i, 0)),
                  pl.BlockSpec((tm, kc), lambda i: (i, 0)),
                  pl.BlockSpec((ka, d), lambda i: (0, 0), pipeline_mode=resident),
                  pl.BlockSpec((kc, d), lambda i: (1, 0), pipeline_mode=resident),
                  pl.BlockSpec((tm, d), lambda i: (i, 0)),
                  pl.BlockSpec((1, d), lambda i: (0, 0))],
        out_specs=pl.BlockSpec((tm, d), lambda i: (i, 0)),
        out_shape=jax.ShapeDtypeStruct((s, d), F32),
        compiler_params=pltpu.CompilerParams(
            dimension_semantics=("arbitrary",),
            vmem_limit_bytes=V7X_VMEM_LIMIT_HIGH_BYTES),
        name="out_proj_norm",
    )(attn, conv, w_out, w_out, x, g.reshape(1, d))


def kernel(x, norm_gain, w_in, conv_w, w_out, rel_bias, final_gain):
    b, s, d = x.shape
    depth = norm_gain.shape[0]
    assert b == 1 and depth == 1, "kernel is written for one sequence and one layer"
    attn_w = d // 2
    conv_width = d - attn_w
    assert rel_bias.shape == (NUM_BUCKETS, attn_w // HEAD_DIM)
    assert w_in.shape[2] == 4 * attn_w + 4 * conv_width

    x2 = x[0]
    assert attn_w == conv_width
    pair_w = 2 * attn_w
    w_qk_b = w_in[0, :, :pair_w].astype(BF16)

    u = _rmsnorm_bf16(x2, norm_gain[0])
    tm, tn = 1024, 512
    tile = pl.BlockSpec((tm, tn), lambda j, i: (i, j))
    act = jax.ShapeDtypeStruct((s, attn_w), BF16)
    tile_t = pl.BlockSpec((tn, tm), lambda j, i: (j, i))
    act_t = jax.ShapeDtypeStruct((attn_w, s), BF16)
    q_t, kaug, kmean, w_vz_b, w_hb_b = _proj_pair(
        functools.partial(_proj_qk_kernel, scale=HEAD_DIM ** -0.5 * LOG2E), "proj_qk", u, w_qk_b, attn_w,
        [tile_t, pl.BlockSpec((tm, 2 * tn), lambda j, i: (i, j)),
         pl.BlockSpec((1, tm // BLOCK, tn), lambda j, i: (i, 0, j))],
        [act_t, jax.ShapeDtypeStruct((s, 2 * attn_w), BF16),
         jax.ShapeDtypeStruct((s // tm, tm // BLOCK, attn_w), F32)],
        [(w_in[0], 1, pair_w), (w_in[0], 2, pair_w)], tm, tn)
    v_t, sz, w_cz_b = _proj_pair(
        _proj_vz_kernel, "proj_vz", u, w_vz_b, attn_w,
        [pl.BlockSpec((tn, tm), lambda j, i: (j, i)), tile],
        [jax.ShapeDtypeStruct((attn_w, s), BF16), jax.ShapeDtypeStruct((s, attn_w), F32)],
        [(w_in[0], 3, pair_w)], tm, tn)
    conv, w_out_b = _proj_conv(u, w_hb_b, w_cz_b, conv_w[0], conv_width, w_out[0])
    attn = _moba_attention(q_t, kaug, v_t, kmean.reshape(s // BLOCK, attn_w), sz, rel_bias)
    out = _out_proj(attn, conv, w_out_b, x2, final_gain)
    return out[None]
```

```python
import functools
import math

import numpy as np
import jax
import jax.numpy as jnp
from jax import lax
from jax.experimental import pallas as pl
from jax.experimental.pallas import tpu as pltpu

HEAD_DIM = 128
BLOCK = 256
TOPK = 3
NUM_BUCKETS = 32
MAX_DISTANCE = 128
EPS = 1e-6
NEG = -1e30
LOG2E = math.log2(math.e)

F32 = jnp.float32
BF16 = jnp.bfloat16

V7X_VMEM_LIMIT_BYTES = 56 * 1024 * 1024
V7X_OUT_PROJ_VMEM_LIMIT_BYTES = 60 * 1024 * 1024


def _silu(z):
    return z * (1.0 / (1.0 + jnp.exp(-z)))


def _rmsnorm_kernel(x_ref, g_ref, o_ref):
    x = x_ref[...]
    ms = jnp.mean(x * x, axis=-1, keepdims=True)
    o_ref[...] = (x * lax.rsqrt(ms + EPS) * g_ref[...]).astype(o_ref.dtype)


def _rmsnorm_bf16(x, g, tm=512):
    s, d = x.shape
    return pl.pallas_call(
        _rmsnorm_kernel,
        grid=(s // tm,),
        in_specs=[pl.BlockSpec((tm, d), lambda i: (i, 0)),
                  pl.BlockSpec((1, d), lambda i: (0, 0))],
        out_specs=pl.BlockSpec((tm, d), lambda i: (i, 0)),
        out_shape=jax.ShapeDtypeStruct((s, d), BF16),
        compiler_params=pltpu.CompilerParams(dimension_semantics=("arbitrary",)),
        name="rmsnorm_in",
    )(x, g.reshape(1, d))


def _run_side_jobs(side_refs, side_out_refs):
    for src, dst in zip(side_refs, side_out_refs):
        dst[...] = src[...].astype(dst.dtype)


def _side_cast_specs(side, col_block, cols, n_steps, step_of):
    rows = side.shape[0]
    chunk = rows // n_steps
    assert chunk * n_steps == rows and chunk % 16 == 0 and side.shape[1] % cols == 0
    return (pl.BlockSpec((chunk, cols), lambda j, i: (step_of(j, i), col_block)),
            pl.BlockSpec((chunk, cols), lambda j, i: (step_of(j, i), 0)),
            jax.ShapeDtypeStruct((rows, cols), BF16))


def _proj_qk_kernel(u_ref, wq_ref, wk_ref, *refs, scale, n_side):
    side_refs, (q_ref, ks_ref, km_ref), side_out_refs = refs[:n_side], refs[n_side:n_side + 3], refs[n_side + 3:]
    _run_side_jobs(side_refs, side_out_refs)
    u = u_ref[...]
    tm = u.shape[0]
    q = jnp.dot(u, wq_ref[...], preferred_element_type=F32)
    q_ref[...] = q.T.astype(BF16)
    k = jnp.dot(u, wk_ref[...], preferred_element_type=F32)
    ks_ref[...] = (k * scale).astype(BF16)
    km_ref[0] = jnp.mean(k.reshape(tm // BLOCK, BLOCK, k.shape[1]), axis=1)


def _proj_vz_kernel(u_ref, wv_ref, wz_ref, *refs, n_side):
    side_refs, (v_ref, sz_ref), side_out_refs = refs[:n_side], refs[n_side:n_side + 2], refs[n_side + 2:]
    _run_side_jobs(side_refs, side_out_refs)
    u = u_ref[...]
    v = jnp.dot(u, wv_ref[...], preferred_element_type=F32)
    v_ref[...] = v.T.astype(BF16)
    z = jnp.dot(u, wz_ref[...], preferred_element_type=F32)
    sz_ref[...] = _silu(z)


def _proj_pair(kernel_fn, name, u, w, width, out_specs, out_shape, side_jobs, tm, tn):
    s, d = u.shape
    nj = width // tn
    ni = s // tm
    jobs = [_side_cast_specs(m, cb, cols, nj * ni, lambda j, i: j * ni + i) for m, cb, cols in side_jobs]
    w_spec = lambda part: pl.BlockSpec((d, tn), lambda j, i, part=part: (0, part * nj + j))
    return pl.pallas_call(
        functools.partial(kernel_fn, n_side=len(jobs)),
        grid=(nj, ni),
        in_specs=[pl.BlockSpec((tm, d), lambda j, i: (i, 0)), w_spec(0), w_spec(1)] + [jb[0] for jb in jobs],
        out_specs=out_specs + [jb[1] for jb in jobs],
        out_shape=out_shape + [jb[2] for jb in jobs],
        compiler_params=pltpu.CompilerParams(
            dimension_semantics=("arbitrary", "arbitrary"),
            vmem_limit_bytes=V7X_VMEM_LIMIT_BYTES),
        name=name,
    )(u, w, w, *[m for m, _, _ in side_jobs])


def _proj_conv_kernel(u_ref, wh_ref, wb_ref, wc_ref, wz_ref, cw_ref, side_ref,
                      o_ref, side_out_ref, carry_ref):
    _run_side_jobs([side_ref], [side_out_ref])
    @pl.when(pl.program_id(1) == 0)
    def _():
        carry_ref[...] = jnp.zeros_like(carry_ref)

    u = u_ref[...]
    tm = u.shape[0]
    hc = jnp.dot(u, wh_ref[...], preferred_element_type=F32)
    c = jnp.dot(u, wc_ref[...], preferred_element_type=F32)
    p = c * hc
    rows = lax.broadcasted_iota(jnp.int32, p.shape, 0)
    prev = carry_ref[...]
    prev1 = prev[7:8, :]
    prev2 = prev[6:7, :]
    p1 = jnp.where(rows == 0, prev1, pltpu.roll(p, 1, 0))
    p2 = jnp.where(rows == 0, prev2, jnp.where(rows == 1, prev1, pltpu.roll(p, 2, 0)))
    carry_ref[...] = p[tm - 8:, :]
    cw = cw_ref[...]
    y = p2 * cw[0:1, :] + p1 * cw[1:2, :] + p * cw[2:3, :]
    z = jnp.dot(u, wz_ref[...], preferred_element_type=F32)
    gate = _silu(z)
    b = jnp.dot(u, wb_ref[...], preferred_element_type=F32)
    o_ref[...] = (b * y * gate).astype(o_ref.dtype)


def _proj_conv(u, w_hb, w_cz, conv_w, conv_width, side, tm=1024, tn=256):
    s, d = u.shape
    nj = conv_width // tn
    ni = s // tm
    side_in, side_out, side_shape = _side_cast_specs(
        side, 0, side.shape[1], nj * ni, lambda j, i: j * ni + i)
    w_spec = lambda part: pl.BlockSpec((d, tn), lambda j, i, part=part: (0, part * nj + j))
    return pl.pallas_call(
        _proj_conv_kernel,
        grid=(nj, ni),
        in_specs=[pl.BlockSpec((tm, d), lambda j, i: (i, 0)),
                  w_spec(0), w_spec(1), w_spec(0), w_spec(1),
                  pl.BlockSpec((conv_w.shape[0], tn), lambda j, i: (0, j)), side_in],
        out_specs=[pl.BlockSpec((tm, tn), lambda j, i: (i, j)), side_out],
        out_shape=[jax.ShapeDtypeStruct((s, conv_width), BF16), side_shape],
        scratch_shapes=[pltpu.VMEM((8, tn), F32)],
        compiler_params=pltpu.CompilerParams(
            dimension_semantics=("arbitrary", "arbitrary"),
            vmem_limit_bytes=V7X_VMEM_LIMIT_BYTES),
        name="proj_conv",
    )(u, w_hb, w_hb, w_cz, w_cz, conv_w, side)


def _rel_bucket_np(dist):
    n = np.maximum(dist, 0)
    max_exact = NUM_BUCKETS // 2
    nf = np.maximum(n, 1).astype(np.float32)
    ratio = np.log(nf / np.float32(max_exact)) / np.float32(math.log(MAX_DISTANCE / max_exact))
    large = max_exact + (ratio * np.float32(NUM_BUCKETS - max_exact)).astype(np.int32)
    large = np.minimum(large, NUM_BUCKETS - 1)
    return np.where(n < max_exact, n, large).astype(np.int32)


def _bucket_tables():
    key = np.arange(BLOCK)[:, None]
    qry = np.arange(BLOCK)[None, :]
    d_own = qry - key
    own = np.where(d_own >= 0, _rel_bucket_np(d_own), NUM_BUCKETS).astype(np.int32)
    adj = _rel_bucket_np(BLOCK + qry - key)
    return own, adj


GROUP = 4
QPAIR = 2


def _attn_kernel(relb_ref, idx_own_ref, idx_adj_ref, q_ref, ks_ref, vt_ref, km_ref, sz_ref,
                 o_ref,
                 kaug_ref, bown_ref, badj_ref, qat_ref, pen_ref, m_ref, l_ref, acc_ref,
                 sa_ref, sb_ref, *, nb):
    h = pl.program_id(0)
    gk = GROUP * BLOCK
    pw = QPAIR * BLOCK
    dot = functools.partial(jnp.dot, preferred_element_type=F32)

    @pl.when(h == 0)
    def _():
        lane = lax.broadcasted_iota(jnp.int32, (BLOCK, HEAD_DIM), 1)
        for blk in range(nb):
            r0 = (blk % GROUP) * BLOCK
            kaug_ref[blk // GROUP, r0:r0 + BLOCK, HEAD_DIM:] = jnp.where(lane == blk, 1.0, 0.0).astype(BF16)

    far_bias = relb_ref[NUM_BUCKETS - 1, h]
    io = idx_own_ref[...]
    ia = idx_adj_ref[...]
    own = jnp.full((BLOCK, BLOCK), NEG, F32)
    adj = jnp.zeros((BLOCK, BLOCK), F32)
    for b in range(NUM_BUCKETS):
        t = (relb_ref[b, h] - far_bias) * LOG2E
        own = jnp.where(io == b, t, own)
        adj = jnp.where(ia == b, t, adj)
    bown_ref[...] = own
    badj_ref[...] = adj

    def stage(g, carry):
        rows = pl.ds(pl.multiple_of(g * gk, gk), gk)
        kaug_ref[g, :, :HEAD_DIM] = ks_ref[rows, :]
        return carry

    lax.fori_loop(0, nb // GROUP, stage, 0)

    km = km_ref[...]
    km_hi = km.astype(BF16)
    km_lo = (km - km_hi.astype(F32)).astype(BF16)

    ROUTE_CLASSES = 8
    rq = ROUTE_CLASSES * gk

    def route(cc):
        q_t = q_ref[:, cc * rq:(cc + 1) * rq]
        gate = dot(km_hi, q_t) + dot(km_lo, q_t)
        blk_id = lax.broadcasted_iota(jnp.int32, gate.shape, 0)
        qb = cc * (ROUTE_CLASSES * GROUP) + lax.shift_right_logical(
            lax.broadcasted_iota(jnp.int32, gate.shape, 1), int(math.log2(BLOCK)))
        gate = jnp.where(blk_id < qb, gate, -jnp.inf)
        sel = jnp.zeros(gate.shape, jnp.bool_)
        for r in range(min(TOPK, nb)):
            mx = jnp.max(gate, axis=0, keepdims=True)
            first = jnp.min(jnp.where(gate == mx, blk_id, nb), axis=0, keepdims=True)
            pick = (blk_id == first) & (qb > r)
            sel = sel | pick
            gate = jnp.where(pick, -jnp.inf, gate)
        sel_far = jnp.where(sel & (blk_id < qb - 1), 0.0, NEG)
        sel_rows = jnp.concatenate(
            [sel_far, jnp.zeros((HEAD_DIM - nb, rq), F32)], axis=0).astype(BF16)
        adj_sel = jnp.max(jnp.where(sel & (blk_id == qb - 1), 1.0, 0.0), axis=0, keepdims=True)
        pen = jnp.where(adj_sel > 0.0, 0.0, NEG)
        for t in range(rq // pw):
            cols = slice(t * pw, (t + 1) * pw)
            qat_ref[cc * (rq // pw) + t, :HEAD_DIM, :] = q_t[:, cols]
            qat_ref[cc * (rq // pw) + t, HEAD_DIM:, :] = sel_rows[:, cols]
        for j in range(rq // BLOCK):
            cols = slice(j * BLOCK, (j + 1) * BLOCK)
            pen_ref[cc * (rq // BLOCK) + j] = jnp.broadcast_to(pen[:, cols], (8, BLOCK))

    for cc in range(nb // (ROUTE_CLASSES * GROUP)):
        route(cc)

    def colmax(s):
        return jnp.max(s, axis=0, keepdims=True)

    def colsum(p):
        return jnp.sum(p, axis=0, keepdims=True)

    def pv(g, first, p):
        keys = slice((g * GROUP + first) * BLOCK, (g * GROUP + first) * BLOCK + p.shape[0])
        return dot(vt_ref[:, keys], p.astype(BF16))

    def do_class(c, carry):
        def far_logits(g, t):
            return dot(kaug_ref[g], qat_ref[c * (GROUP // QPAIR) + t])

        def far_park(g, t, s_ref):
            s = far_logits(g, t)
            s_ref[...] = s
            return colmax(s)

        lanes_of = [slice((j % QPAIR) * BLOCK, (j % QPAIR + 1) * BLOCK) for j in range(GROUP)]
        near = []
        for j in range(GROUP):
            qb = c * GROUP + j
            pair = c * (GROUP // QPAIR) + j // QPAIR
            q_t = qat_ref[pair, :HEAD_DIM, lanes_of[j]]
            parts = [(dot(kaug_ref[c, j * BLOCK:(j + 1) * BLOCK, :HEAD_DIM], q_t) + bown_ref[...], c, j)]
            if qb > 0:
                g_adj, t_adj = (c, j - 1) if j > 0 else (c - 1, GROUP - 1)
                k_adj = kaug_ref[g_adj, t_adj * BLOCK:(t_adj + 1) * BLOCK, :HEAD_DIM]
                parts.append((dot(k_adj, q_t) + (badj_ref[...] + pen_ref[qb, 0:1, :]), g_adj, t_adj))
            if j >= 2:
                parts.append((dot(kaug_ref[c, :(j - 1) * BLOCK, :], qat_ref[pair, :, lanes_of[j]]), c, 0))
            near.append(parts)

        if c > 0:
            cmax_a = far_park(0, 0, sa_ref)

        for j, parts in enumerate(near):
            m = functools.reduce(jnp.maximum, [colmax(s) for s, _, _ in parts])
            probs = [(jnp.exp2(s - m), g, first) for s, g, first in parts]
            m_ref[j // QPAIR, :, lanes_of[j]] = m
            l_ref[j // QPAIR, :, lanes_of[j]] = functools.reduce(jnp.add, [colsum(p) for p, _, _ in probs])
            acc_ref[j // QPAIR, :, lanes_of[j]] = functools.reduce(
                jnp.add, [pv(g, first, p) for p, g, first in probs])

        def far_update(g, t, s_ref, cmax):
            m_old = m_ref[t]
            m_new = jnp.maximum(m_old, cmax)
            alpha = jnp.exp2(m_old - m_new)
            l_new = alpha * l_ref[t]
            acc_new = alpha * acc_ref[t]
            for k in range(GROUP):
                p = jnp.exp2(s_ref[k * BLOCK:(k + 1) * BLOCK, :] - m_new)
                l_new = l_new + colsum(p)
                acc_new = acc_new + pv(g, k, p)
            l_ref[t] = l_new
            acc_ref[t] = acc_new
            m_ref[t] = m_new

        for g in range(c):
            cmax_b = far_park(g, 1, sb_ref)
            far_update(g, 0, sa_ref, cmax_a)
            if g + 1 < c:
                cmax_a = far_park(g + 1, 0, sa_ref)
            far_update(g, 1, sb_ref, cmax_b)

        for j in range(GROUP):
            rows = pl.ds((c * GROUP + j) * BLOCK, BLOCK)
            out_t = acc_ref[j // QPAIR, :, lanes_of[j]] * (1.0 / l_ref[j // QPAIR, :, lanes_of[j]])
            o_ref[rows, :] = (out_t.T * sz_ref[rows, :]).astype(o_ref.dtype)
        return carry

    one = jnp.minimum(h + 1, 1)
    for c in range(nb // GROUP):
        lax.fori_loop(0, one, lambda _, carry, c=c: do_class(c, carry), 0)


def _moba_attention(q_t, ks, v_t, kmean, sz, rel_bias):
    s, attn_w = ks.shape
    nh = attn_w // HEAD_DIM
    nb = s // BLOCK
    assert nb % GROUP == 0 and nb <= HEAD_DIM
    idx_own, idx_adj = _bucket_tables()
    head_cols = pl.BlockSpec((s, HEAD_DIM), lambda h: (0, h))
    const_tile = pl.BlockSpec((BLOCK, BLOCK), lambda h: (0, 0))
    return pl.pallas_call(
        functools.partial(_attn_kernel, nb=nb),
        grid=(nh,),
        in_specs=[pl.BlockSpec(memory_space=pltpu.SMEM),
                  const_tile, const_tile,
                  pl.BlockSpec((HEAD_DIM, s), lambda h: (h, 0)),
                  head_cols,
                  pl.BlockSpec((HEAD_DIM, s), lambda h: (h, 0)),
                  pl.BlockSpec((nb, HEAD_DIM), lambda h: (0, h)),
                  head_cols],
        out_specs=head_cols,
        out_shape=jax.ShapeDtypeStruct((s, attn_w), BF16),
        scratch_shapes=[pltpu.VMEM((nb // GROUP, GROUP * BLOCK, 2 * HEAD_DIM), BF16),
                        pltpu.VMEM((BLOCK, BLOCK), F32),
                        pltpu.VMEM((BLOCK, BLOCK), F32),
                        pltpu.VMEM((nb // QPAIR, 2 * HEAD_DIM, QPAIR * BLOCK), BF16),
                        pltpu.VMEM((nb, 8, BLOCK), F32),
                        pltpu.VMEM((GROUP // QPAIR, 1, QPAIR * BLOCK), F32),
                        pltpu.VMEM((GROUP // QPAIR, 1, QPAIR * BLOCK), F32),
                        pltpu.VMEM((GROUP // QPAIR, HEAD_DIM, QPAIR * BLOCK), F32),
                        pltpu.VMEM((GROUP * BLOCK, QPAIR * BLOCK), F32),
                        pltpu.VMEM((GROUP * BLOCK, QPAIR * BLOCK), F32)],
        compiler_params=pltpu.CompilerParams(
            dimension_semantics=("arbitrary",),
            vmem_limit_bytes=V7X_VMEM_LIMIT_BYTES),
        name="moba_attn",
    )(rel_bias, jnp.asarray(idx_own), jnp.asarray(idx_adj), q_t, ks, v_t, kmean, sz)


def _out_kernel(a_ref, c_ref, wa_ref, wc_ref, x_ref, g_ref, o_ref, *, n_chunks):
    d = o_ref.shape[1]
    w = d // n_chunks
    a = a_ref[...]
    c = c_ref[...]
    hs = []
    ss = jnp.zeros((o_ref.shape[0], 1), F32)
    for n in range(n_chunks):
        cols = slice(n * w, (n + 1) * w)
        h = (x_ref[:, cols]
             + jnp.dot(a, wa_ref[:, cols], preferred_element_type=F32)
             + jnp.dot(c, wc_ref[:, cols], preferred_element_type=F32))
        ss = ss + jnp.sum(h * h, axis=-1, keepdims=True)
        hs.append(h)
    inv = lax.rsqrt(ss / d + EPS)
    for n, h in enumerate(hs):
        cols = slice(n * w, (n + 1) * w)
        o_ref[:, cols] = h * inv * g_ref[:, cols]


def _out_proj(attn, conv, w_out, x, g, tm=256):
    s, d = x.shape
    ka = attn.shape[1]
    kc = conv.shape[1]
    assert ka == kc
    resident = pl.Buffered(1)
    return pl.pallas_call(
        functools.partial(_out_kernel, n_chunks=4),
        grid=(s // tm,),
        in_specs=[pl.BlockSpec((tm, ka), lambda i: (i, 0)),
                  pl.BlockSpec((tm, kc), lambda i: (i, 0)),
                  pl.BlockSpec((ka, d), lambda i: (0, 0), pipeline_mode=resident),
                  pl.BlockSpec((kc, d), lambda i: (1, 0), pipeline_mode=resident),
                  pl.BlockSpec((tm, d), lambda i: (i, 0)),
                  pl.BlockSpec((1, d), lambda i: (0, 0))],
        out_specs=pl.BlockSpec((tm, d), lambda i: (i, 0)),
        out_shape=jax.ShapeDtypeStruct((s, d), F32),
        compiler_params=pltpu.CompilerParams(
            dimension_semantics=("arbitrary",),
            vmem_limit_bytes=V7X_OUT_PROJ_VMEM_LIMIT_BYTES),
        name="out_proj_norm",
    )(attn, conv, w_out, w_out, x, g.reshape(1, d))


def kernel(x, norm_gain, w_in, conv_w, w_out, rel_bias, final_gain):
    b, s, d = x.shape
    depth = norm_gain.shape[0]
    assert b == 1 and depth == 1, "kernel is written for one sequence and one layer"
    attn_w = d // 2
    conv_width = d - attn_w
    assert rel_bias.shape == (NUM_BUCKETS, attn_w // HEAD_DIM)
    assert w_in.shape[2] == 4 * attn_w + 4 * conv_width

    x2 = x[0]
    assert attn_w == conv_width
    pair_w = 2 * attn_w
    w_qk_b = w_in[0, :, :pair_w].astype(BF16)

    u = _rmsnorm_bf16(x2, norm_gain[0])
    tm, tn = 1024, 512
    tile = pl.BlockSpec((tm, tn), lambda j, i: (i, j))
    act = jax.ShapeDtypeStruct((s, attn_w), BF16)
    tile_t = pl.BlockSpec((tn, tm), lambda j, i: (j, i))
    act_t = jax.ShapeDtypeStruct((attn_w, s), BF16)
    q_t, ks, kmean, w_vz_b, w_hb_b = _proj_pair(
        functools.partial(_proj_qk_kernel, scale=HEAD_DIM ** -0.5 * LOG2E), "proj_qk", u, w_qk_b, attn_w,
        [tile_t, tile, pl.BlockSpec((1, tm // BLOCK, tn), lambda j, i: (i, 0, j))],
        [act_t, act, jax.ShapeDtypeStruct((s // tm, tm // BLOCK, attn_w), F32)],
        [(w_in[0], 1, pair_w), (w_in[0], 2, pair_w)], tm, tn)
    v_t, sz, w_cz_b = _proj_pair(
        _proj_vz_kernel, "proj_vz", u, w_vz_b, attn_w,
        [pl.BlockSpec((tn, tm), lambda j, i: (j, i)), tile],
        [jax.ShapeDtypeStruct((attn_w, s), BF16), jax.ShapeDtypeStruct((s, attn_w), F32)],
        [(w_in[0], 3, pair_w)], tm, tn)
    conv, w_out_b = _proj_conv(u, w_hb_b, w_cz_b, conv_w[0], conv_width, w_out[0])
    attn = _moba_attention(q_t, ks, v_t, kmean.reshape(s // BLOCK, attn_w), sz, rel_bias)
    out = _out_proj(attn, conv, w_out_b, x2, final_gain)
    return out[None]
```
